```python
import math
import jax, jax.numpy as jnp
from jax import lax
import numpy as np

D_MODEL = 2048
BATCH = 2
SEQ = 4096
DEPTH = 2

GRID_W = 64
CTX_LEN = 256
EPS = 1e-6

D_MIX = D_MODEL
D_POOL = D_MODEL // 4
POOL_WINDOWS = (2, 4, 8, 16)
POOL_GW = D_POOL // len(POOL_WINDOWS)
D_SSM = D_MODEL // 4
SSM_GROUP = 16
SSM_GROUPS = D_SSM // SSM_GROUP
SSM_STATE = 64
SSM_DT_MIN = 0.001
SSM_DT_MAX = 0.1
QK_NOPE = 128
QK_ROPE = 64
QK_HEAD = QK_NOPE + QK_ROPE
V_HEAD = 128
D_MLA = D_MIX - D_POOL - D_SSM
MLA_HEADS = D_MLA // V_HEAD
Q_LORA = D_MODEL // 4
KV_LORA = D_MODEL // 8
ROPE_BASE = 10000.0
Q_BLOCK = 128
OFF_POOL = 0
OFF_Q = OFF_POOL + D_POOL
OFF_SSM = OFF_Q + Q_LORA
OFF_KV = OFF_SSM + D_SSM
OFF_KR = OFF_KV + KV_LORA
IN_W = OFF_KR + QK_ROPE
N_EXPERTS = 32
TOP_K = 4
D_EXPERT = D_MODEL
SWIGLU_ALPHA = 1.702
SWIGLU_LIMIT = 7.0
EXPERT_BLOCK = 256

kernel_name = "hybrid_pool_s5_mla_moe_diffusion_trunk"


def rms_norm(x, g):
    xf = x.astype(jnp.float32)
    y = xf * lax.rsqrt(jnp.mean(jnp.square(xf), axis=-1, keepdims=True) + EPS)
    return (y * g.astype(jnp.float32)).astype(x.dtype)


def ada_chunks(cvec, w_ada, b_ada, n_chunks):
    m = jax.nn.silu(cvec) @ w_ada[:, :n_chunks * D_MODEL] + b_ada[:n_chunks * D_MODEL]
    return jnp.split(m, n_chunks, axis=-1)


def axial_rope_tables(n_tokens):
    rows = n_tokens // GRID_W
    row = jnp.broadcast_to(jnp.arange(rows)[:, None], (rows, GRID_W)).reshape(-1).astype(jnp.float32)
    col = jnp.broadcast_to(jnp.arange(GRID_W)[None, :], (rows, GRID_W)).reshape(-1).astype(jnp.float32)
    half = QK_ROPE // 2
    inv_freq = ROPE_BASE ** (-jnp.arange(0, half, 2, dtype=jnp.float32) / half)
    ang = jnp.concatenate([row[:, None] * inv_freq, col[:, None] * inv_freq], axis=-1)
    return jnp.cos(ang), jnp.sin(ang)


def apply_rope(t, cos, sin):
    half = QK_ROPE // 2
    tf = t.astype(jnp.float32)
    t1, t2 = tf[..., :half], tf[..., half:]
    cc, ss = cos[None, :, None, :], sin[None, :, None, :]
    return jnp.concatenate([t1 * cc - t2 * ss, t1 * ss + t2 * cc], axis=-1).astype(t.dtype)


def rope_tail(t, cos, sin):
    return jnp.concatenate([t[..., :QK_NOPE], apply_rope(t[..., QK_NOPE:], cos, sin)], axis=-1)


def pool_mixer(u, w_pool, pool_scale):
    B, L, _ = u.shape
    uf = u.astype(jnp.float32)
    cs = jnp.pad(jnp.cumsum(uf, axis=1), ((0, 0), (1, 0), (0, 0)))
    t = jnp.arange(L)
    parts = []
    for gi, w in enumerate(POOL_WINDOWS):
        lo = jnp.clip(t - w // 2, 0, L)
        hi = jnp.clip(t + (w - w // 2), 0, L)
        cs_g = cs[..., gi * POOL_GW:(gi + 1) * POOL_GW]
        mean = (cs_g[:, hi] - cs_g[:, lo]) / (hi - lo).astype(jnp.float32)[None, :, None]
        parts.append(mean - uf[..., gi * POOL_GW:(gi + 1) * POOL_GW])
    d = jnp.stack(parts, axis=2).astype(u.dtype)
    y = jnp.einsum('blgc,gcd->blgd', d, w_pool).reshape(B, L, D_POOL)
    return y * pool_scale


def s5_discretise(a_re, a_im, log_step):
    step = jnp.exp(log_step.astype(jnp.float32))[:, None]
    ar, ai = a_re.astype(jnp.float32), a_im.astype(jnp.float32)
    mag = jnp.exp(ar * step)
    ab_re, ab_im = mag * jnp.cos(ai * step), mag * jnp.sin(ai * step)
    den = ar * ar + ai * ai
    nr, ni = ab_re - 1.0, ab_im
    coef_re = (nr * ar + ni * ai) / den
    coef_im = (ni * ar - nr * ai) / den
    return ab_re, ab_im, coef_re, coef_im


def _complex_linear_combine(e1, e2):
    a1r, a1i, b1r, b1i = e1
    a2r, a2i, b2r, b2i = e2
    return (a2r * a1r - a2i * a1i, a2r * a1i + a2i * a1r,
            a2r * b1r - a2i * b1i + b2r, a2r * b1i + a2i * b1r + b2i)


def s5_states(u, disc, b_re, b_im, h0=None):
    ab_re, ab_im, coef_re, coef_im = disc
    bu_re = jnp.einsum('blgc,gpc->blgp', u, b_re.astype(jnp.float32))
    bu_im = jnp.einsum('blgc,gpc->blgp', u, b_im.astype(jnp.float32))
    x_re = coef_re * bu_re - coef_im * bu_im
    x_im = coef_re * bu_im + coef_im * bu_re
    a_re = jnp.broadcast_to(ab_re, x_re.shape)
    a_im = jnp.broadcast_to(ab_im, x_re.shape)
    pr, pi, hr, hi = lax.associative_scan(_complex_linear_combine, (a_re, a_im, x_re, x_im), axis=1)
    if h0 is not None:
        h0r, h0i = h0[0][:, None], h0[1][:, None]
        hr = hr + pr * h0r - pi * h0i
        hi = hi + pr * h0i + pi * h0r
    return hr, hi


def s5_readout(hr, hi, c_re, c_im):
    return (jnp.einsum('blgp,gcp->blgc', hr, c_re.astype(jnp.float32))
            - jnp.einsum('blgp,gcp->blgc', hi, c_im.astype(jnp.float32)))


def _flip(z, direction):
    return z if direction == 0 else jnp.flip(z, axis=1)


def bidirectional_s5(u_lat, u_ctx, a_re, a_im, log_step, b_re, b_im, c_re, c_im, ctx_out):
    B, L, _ = u_lat.shape
    Lc = u_ctx.shape[1]
    ul = u_lat.astype(jnp.float32).reshape(B, L, SSM_GROUPS, SSM_GROUP)
    uc = u_ctx.astype(jnp.float32).reshape(B, Lc, SSM_GROUPS, SSM_GROUP)
    y_lat, y_ctx = 0.0, 0.0
    for d in range(2):
        disc = s5_discretise(a_re[d], a_im[d], log_step[d])
        hc_r, hc_i = s5_states(_flip(uc, d), disc, b_re[d], b_im[d])
        hl_r, hl_i = s5_states(_flip(ul, d), disc, b_re[d], b_im[d], h0=(hc_r[:, -1], hc_i[:, -1]))
        y_lat = y_lat + _flip(s5_readout(hl_r, hl_i, c_re[d], c_im[d]), d)
        if ctx_out:
            y_ctx = y_ctx + _flip(s5_readout(hc_r, hc_i, c_re[d], c_im[d]), d)
    return y_lat, (y_ctx if ctx_out else None)


def s5_output(y, u, d_skip, w_glu, b_glu):
    B, L, _ = u.shape
    z = y.reshape(B, L, D_SSM) + d_skip.astype(jnp.float32) * u.astype(jnp.float32)
    g = jax.nn.gelu(z).astype(u.dtype)
    return g * jax.nn.sigmoid(g @ w_glu + b_glu)


def mla_queries(c_q, q_a_norm, w_uq, q_norm):
    B, L, _ = c_q.shape
    q = (rms_norm(c_q, q_a_norm) @ w_uq).reshape(B, L, MLA_HEADS, QK_HEAD)
    return rms_norm(q, q_norm)


def mla_keys_values(c_kv, k_r, kv_a_norm, w_ukv, k_norm):
    B, L, _ = c_kv.shape
    kv = (rms_norm(c_kv, kv_a_norm) @ w_ukv).reshape(B, L, MLA_HEADS, QK_NOPE + V_HEAD)
    k_nope, v = kv[..., :QK_NOPE], kv[..., QK_NOPE:]
    k = jnp.concatenate([k_nope, jnp.broadcast_to(k_r[:, :, None, :], (B, L, MLA_HEADS, QK_ROPE))], axis=-1)
    return rms_norm(k, k_norm), v


def attend(q, k, v):
    s = jnp.einsum('bqhd,bkhd->bhqk', q, k, preferred_element_type=jnp.float32) * (QK_HEAD ** -0.5)
    p = jax.nn.softmax(s, axis=-1).astype(v.dtype)
    return jnp.einsum('bhqk,bkhd->bqhd', p, v)


def blocked_attend(q, k, v):
    B, L, H, dk = q.shape
    nb = L // Q_BLOCK
    qb = q.reshape(B, nb, Q_BLOCK, H, dk).transpose(1, 0, 2, 3, 4)
    out = lax.map(lambda blk: attend(blk, k, v), qb)
    return out.transpose(1, 0, 2, 3, 4).reshape(B, L, H * V_HEAD)


def merge_groups(y_pool, y_ssm, y_mla, out_norm, w_out):
    parts = [rms_norm(y_pool, out_norm[:D_POOL]),
             rms_norm(y_ssm, out_norm[D_POOL:D_POOL + D_SSM]),
             rms_norm(y_mla, out_norm[D_POOL + D_SSM:])]
    return jnp.concatenate(parts, axis=-1) @ w_out


def moe(h, w_router, b_router, w_gate, b_gate, w_up, b_up, w_down, b_down):
    T, D = h.shape
    logits = (h @ w_router + b_router).astype(jnp.float32)
    top_val, top_idx = lax.top_k(logits, TOP_K)
    gates = jax.nn.softmax(top_val, axis=-1)
    n = T * TOP_K
    flat_e = top_idx.reshape(-1).astype(jnp.int32)
    flat_tok = (jnp.arange(n) // TOP_K).astype(jnp.int32)
    flat_g = gates.reshape(-1)
    order = jnp.argsort(flat_e)
    se = flat_e[order]
    counts = jnp.bincount(flat_e, length=N_EXPERTS)
    starts = jnp.cumsum(counts) - counts
    padded = ((counts + EXPERT_BLOCK - 1) // EXPERT_BLOCK) * EXPERT_BLOCK
    pad_ends = jnp.cumsum(padded)
    pad_starts = pad_ends - padded
    dest = pad_starts[se] + jnp.arange(n) - starts[se]
    n_blocks = -(-n // EXPERT_BLOCK) + N_EXPERTS
    cap = n_blocks * EXPERT_BLOCK
    row_tok = jnp.full((cap,), T, jnp.int32).at[dest].set(flat_tok[order])
    row_gate = jnp.zeros((cap,), jnp.float32).at[dest].set(flat_g[order])
    block_e = jnp.minimum(jnp.searchsorted(pad_ends, jnp.arange(n_blocks) * EXPERT_BLOCK, side='right'),
                          N_EXPERTS - 1)
    h_pad = jnp.concatenate([h, jnp.zeros((1, D), h.dtype)], axis=0)

    def expert_block(args):
        tok, e = args
        xb = h_pad[tok]
        g = jnp.minimum(xb @ w_gate[e] + b_gate[e], SWIGLU_LIMIT)
        u = jnp.clip(xb @ w_up[e] + b_up[e], -SWIGLU_LIMIT, SWIGLU_LIMIT)
        act = g * jax.nn.sigmoid(SWIGLU_ALPHA * g) * (u + 1.0)
        return act @ w_down[e] + b_down[e]

    out = lax.map(expert_block, (row_tok.reshape(n_blocks, EXPERT_BLOCK), block_e))
    out = out.reshape(cap, D).astype(jnp.float32) * row_gate[:, None]
    y = jnp.zeros((T + 1, D), jnp.float32).at[row_tok].add(out)[:T]
    return y.astype(h.dtype)


def hybrid_layer(x, ctx, c, c_ctx, cos, sin, prm, ctx_out):
    (w_ada, b_ada, norm_mix, norm_ffn, w_in, w_pool, pool_scale,
     ssm_a_re, ssm_a_im, ssm_log_step, ssm_b_re, ssm_b_im, ssm_c_re, ssm_c_im,
     ssm_d, w_glu, b_glu, q_a_norm, w_uq, kv_a_norm, w_ukv, q_norm, k_norm,
     out_norm, w_out, w_router, b_router, w_gate, b_gate, w_up, b_up, w_down, b_down) = prm
    B, L, D = x.shape
    Lc = ctx.shape[1]
    sh1, sc1, g1, sh2, sc2, g2 = [m[:, None, :] for m in ada_chunks(c, w_ada, b_ada, 6)]
    cmod = ada_chunks(c_ctx, w_ada, b_ada, 6 if ctx_out else 2)

    h = rms_norm(x, norm_mix) * (1.0 + sc1) + sh1
    hc = rms_norm(ctx, norm_mix) * (1.0 + cmod[1]) + cmod[0]
    proj = h @ w_in
    c0 = OFF_POOL if ctx_out else OFF_SSM
    proj_c = hc @ w_in[:, c0:]
    pc = lambda a, b: proj_c[..., a - c0:b - c0]

    y_pool = pool_mixer(proj[..., OFF_POOL:OFF_Q], w_pool, pool_scale)
    u_ssm, u_ssm_c = proj[..., OFF_SSM:OFF_KV], pc(OFF_SSM, OFF_KV)
    ys_lat, ys_ctx = bidirectional_s5(u_ssm, u_ssm_c, ssm_a_re, ssm_a_im, ssm_log_step,
                                      ssm_b_re, ssm_b_im, ssm_c_re, ssm_c_im, ctx_out)
    y_ssm = s5_output(ys_lat, u_ssm, ssm_d, w_glu, b_glu)
    q = rope_tail(mla_queries(proj[..., OFF_Q:OFF_SSM], q_a_norm, w_uq, q_norm), cos, sin)
    k, v = mla_keys_values(proj[..., OFF_KV:OFF_KR], proj[..., OFF_KR:], kv_a_norm, w_ukv, k_norm)
    k = rope_tail(k, cos, sin)
    k_c, v_c = mla_keys_values(pc(OFF_KV, OFF_KR), pc(OFF_KR, IN_W), kv_a_norm, w_ukv, k_norm)
    y_mla = blocked_attend(q, jnp.concatenate([k_c, k], axis=1), jnp.concatenate([v_c, v], axis=1))
    x = x + g1 * merge_groups(y_pool, y_ssm, y_mla, out_norm, w_out)

    if ctx_out:
        y_pool_c = pool_mixer(pc(OFF_POOL, OFF_Q), w_pool, pool_scale)
        y_ssm_c = s5_output(ys_ctx, u_ssm_c, ssm_d, w_glu, b_glu)
        q_c = mla_queries(pc(OFF_Q, OFF_SSM), q_a_norm, w_uq, q_norm)
        y_mla_c = attend(q_c, k_c, v_c).reshape(B, Lc, D_MLA)
        ctx = ctx + cmod[2] * merge_groups(y_pool_c, y_ssm_c, y_mla_c, out_norm, w_out)

    h2 = rms_norm(x, norm_ffn) * (1.0 + sc2) + sh2
    tokens = h2.reshape(B * L, D)
    if ctx_out:
        h2c = rms_norm(ctx, norm_ffn) * (1.0 + cmod[4]) + cmod[3]
        tokens = jnp.concatenate([tokens, h2c.reshape(B * Lc, D)], axis=0)
    y = moe(tokens, w_router, b_router, w_gate, b_gate, w_up, b_up, w_down, b_down)
    x = x + g2 * y[:B * L].reshape(B, L, D)
    if ctx_out:
        ctx = ctx + cmod[5] * y[B * L:].reshape(B, Lc, D)
    return x, ctx


def setup_inputs(seed: int = 0) -> dict:
    key = jax.random.key(seed)
    keys = jax.random.split(key, 40)

    def nrm(i, shape, scale):
        return scale * jax.random.normal(keys[i], shape, jnp.float32)

    L, G, P, C, E, F, D = DEPTH, SSM_GROUPS, SSM_STATE, SSM_GROUP, N_EXPERTS, D_EXPERT, D_MODEL
    a_im_init = jnp.pi * jnp.arange(P, dtype=jnp.float32)
    return {
        "x": nrm(0, (BATCH, SEQ, D), 1.0),
        "c": nrm(1, (BATCH, D), 1.0),
        "ctx": nrm(2, (BATCH, CTX_LEN, D), 1.0),
        "c_ctx": nrm(3, (D,), 1.0),
        "w_ada": nrm(4, (L, D, 6 * D), 0.5 * D ** -0.5),
        "b_ada": nrm(5, (L, 6 * D), 0.02),
        "norm_mix": 1.0 + nrm(6, (L, D), 0.05),
        "norm_ffn": 1.0 + nrm(7, (L, D), 0.05),
        "w_in": nrm(8, (L, D, IN_W), D ** -0.5),
        "w_pool": nrm(9, (L, len(POOL_WINDOWS), POOL_GW, POOL_GW), POOL_GW ** -0.5),
        "pool_scale": 1.0 + nrm(10, (L, D_POOL), 0.1),
        "ssm_a_re": -0.5 + nrm(11, (L, 2, G, P), 0.01),
        "ssm_a_im": a_im_init + nrm(12, (L, 2, G, P), 0.01),
        "ssm_log_step": jax.random.uniform(keys[13], (L, 2, G), jnp.float32,
                                           math.log(SSM_DT_MIN), math.log(SSM_DT_MAX)),
        "ssm_b_re": nrm(14, (L, 2, G, P, C), (2 * C) ** -0.5),
        "ssm_b_im": nrm(15, (L, 2, G, P, C), (2 * C) ** -0.5),
        "ssm_c_re": nrm(16, (L, 2, G, C, P), P ** -0.5),
        "ssm_c_im": nrm(17, (L, 2, G, C, P), P ** -0.5),
        "ssm_d": nrm(18, (L, D_SSM), 1.0),
        "w_glu": nrm(19, (L, D_SSM, D_SSM), D_SSM ** -0.5),
        "b_glu": nrm(20, (L, D_SSM), 0.01),
        "q_a_norm": 1.0 + nrm(21, (L, Q_LORA), 0.05),
        "w_uq": nrm(22, (L, Q_LORA, MLA_HEADS * QK_HEAD), Q_LORA ** -0.5),
        "kv_a_norm": 1.0 + nrm(23, (L, KV_LORA), 0.05),
        "w_ukv": nrm(24, (L, KV_LORA, MLA_HEADS * (QK_NOPE + V_HEAD)), KV_LORA ** -0.5),
        "q_norm": 1.0 + nrm(25, (L, QK_HEAD), 0.05),
        "k_norm": 1.0 + nrm(26, (L, QK_HEAD), 0.05),
        "out_norm": 1.0 + nrm(27, (L, D_MIX), 0.05),
        "w_out": nrm(28, (L, D_MIX, D), D_MIX ** -0.5),
        "w_router": nrm(29, (L, D, E), D ** -0.5),
        "b_router": nrm(30, (L, E), 0.01),
        "w_gate": nrm(31, (L, E, D, F), D ** -0.5),
        "b_gate": nrm(32, (L, E, F), 0.01),
        "w_up": nrm(33, (L, E, D, F), D ** -0.5),
        "b_up": nrm(34, (L, E, F), 0.01),
        "w_down": nrm(35, (L, E, F, D), F ** -0.5),
        "b_down": nrm(36, (L, E, D), 0.01),
    }


def reference(x, c, ctx, c_ctx, w_ada, b_ada, norm_mix, norm_ffn, w_in, w_pool, pool_scale,
              ssm_a_re, ssm_a_im, ssm_log_step, ssm_b_re, ssm_b_im, ssm_c_re, ssm_c_im,
              ssm_d, w_glu, b_glu, q_a_norm, w_uq, kv_a_norm, w_ukv, q_norm, k_norm,
              out_norm, w_out, w_router, b_router, w_gate, b_gate, w_up, b_up, w_down, b_down):
    cos, sin = axial_rope_tables(x.shape[1])
    params = (w_ada, b_ada, norm_mix, norm_ffn, w_in, w_pool, pool_scale,
              ssm_a_re, ssm_a_im, ssm_log_step, ssm_b_re, ssm_b_im, ssm_c_re, ssm_c_im,
              ssm_d, w_glu, b_glu, q_a_norm, w_uq, kv_a_norm, w_ukv, q_norm, k_norm,
              out_norm, w_out, w_router, b_router, w_gate, b_gate, w_up, b_up, w_down, b_down)
    for layer in range(DEPTH):
        prm = tuple(p[layer] for p in params)
        x, ctx = hybrid_layer(x, ctx, c, c_ctx, cos, sin, prm, ctx_out=(layer < DEPTH - 1))
    return x
```

```python
import functools
import math

import jax
import jax.numpy as jnp
from jax import lax
from jax.experimental import pallas as pl
from jax.experimental.pallas import tpu as pltpu

F32 = jnp.float32
BF16 = jnp.bfloat16

EPS = 1e-6
GRID_W = 64
POOL_WINDOWS = (2, 4, 8, 16)
POOL_HALO = 8
SSM_GROUP = 16
SSM_STATE = 64
S5_CHUNK = 32
QK_NOPE = 128
QK_ROPE = 64
QK_HEAD = QK_NOPE + QK_ROPE
QK_PAD = 256
V_HEAD = 128
ROPE_BASE = 10000.0
TOP_K = 4
SWIGLU_ALPHA = 1.702
SWIGLU_LIMIT = 7.0

LANES = 128
ROW_TILE = 256
ATTN_Q_TILE = 256
EXPERT_ITEM_ROWS = 1024
EXPERT_SUB_ROWS = 256
EXPERT_F_TILE = 256
COMBINE_TILE = 128
VMEM_LIMIT = 52 * 1024 * 1024
NEG_BIG = -1e30


def _cparams(sem, vmem=VMEM_LIMIT):
    return pltpu.CompilerParams(dimension_semantics=sem, vmem_limit_bytes=vmem)


def _rms(x, w):
    return x * lax.rsqrt(jnp.mean(x * x, axis=-1, keepdims=True) + EPS) * w


def _full(shape):
    nd = len(shape)
    return pl.BlockSpec(shape, lambda *_: (0,) * nd)


def _ada_kernel(cv_ref, w_ref, b_ref, o_ref):
    cv = cv_ref[...]
    s = (cv * jax.nn.sigmoid(cv)).astype(BF16)
    o_ref[0] = jnp.dot(s, w_ref[0].astype(BF16), preferred_element_type=F32) + b_ref[0]


def _ada(cvecs, w_ada, b_ada):
    n_layers, d, n6 = w_ada.shape
    tn = 512
    return pl.pallas_call(
        _ada_kernel,
        grid=(n_layers, n6 // tn),
        in_specs=[
            pl.BlockSpec((8, d), lambda l, j: (0, 0)),
            pl.BlockSpec((1, d, tn), lambda l, j: (l, 0, j)),
            pl.BlockSpec((1, 1, tn), lambda l, j: (l, 0, j)),
        ],
        out_specs=pl.BlockSpec((1, 8, tn), lambda l, j: (l, 0, j)),
        out_shape=jax.ShapeDtypeStruct((n_layers, 8, n6), F32),
        compiler_params=_cparams(("arbitrary", "arbitrary")),
        name="ada",
    )(cvecs, w_ada, b_ada.reshape(n_layers, 1, n6))


def _rope128(r, cos, sin):
    lane = lax.broadcasted_iota(jnp.int32, r.shape, 1)
    sw = jnp.where((lane % QK_ROPE) < QK_ROPE // 2, pltpu.roll(r, LANES - 32, 1), pltpu.roll(r, 32, 1))
    return r * cos + sw * sin


def _proj_kernel(x_ref, mod_ref, cos_ref, sin_ref, nmix_ref, win_ref, qan_ref, wuq_ref, kvan_ref,
                 wk_ref, wv_ref, qn_ref, kn_ref,
                 upool_ref, ussm_ref, q_ref, k_ref, v_ref, *, tiles_per_seq, n_batch, n_heads, d_model):
    d = d_model
    r = jnp.minimum(pl.program_id(0) // tiles_per_seq, n_batch)
    sh = mod_ref[pl.ds(r, 1), pl.ds(0, d)]
    sc = mod_ref[pl.ds(r, 1), pl.ds(d, d)]
    h = _rms(x_ref[...], nmix_ref[...]) * (1.0 + sc) + sh
    proj = jnp.dot(h.astype(BF16), win_ref[...], preferred_element_type=F32)
    d4 = d // 4
    upool_ref[...] = proj[:, 0:d4]
    ussm_ref[...] = proj[:, 2 * d4:3 * d4]
    cos = cos_ref[...]
    sin = sin_ref[...]

    qa = _rms(proj[:, d4:2 * d4], qan_ref[...]).astype(BF16)
    qf = jnp.dot(qa, wuq_ref[...], preferred_element_type=F32)
    qn_w = qn_ref[...]
    for hd in range(n_heads):
        blk = qf[:, hd * QK_PAD:(hd + 1) * QK_PAD]
        ss = jnp.sum(blk * blk, axis=-1, keepdims=True) * (1.0 / QK_HEAD)
        qn = blk * lax.rsqrt(ss + EPS) * qn_w
        q_ref[:, hd * QK_PAD:hd * QK_PAD + LANES] = qn[:, :LANES].astype(BF16)
        q_ref[:, hd * QK_PAD + LANES:(hd + 1) * QK_PAD] = _rope128(qn[:, LANES:], cos, sin).astype(BF16)

    kv_lo = 3 * d4
    kv_w = d // 8
    ka = _rms(proj[:, kv_lo:kv_lo + kv_w], kvan_ref[...]).astype(BF16)
    kn_all = jnp.dot(ka, wk_ref[...], preferred_element_type=F32)
    v_ref[...] = jnp.dot(ka, wv_ref[...], preferred_element_type=F32).astype(BF16)
    krp = proj[:, kv_lo + kv_w:kv_lo + kv_w + LANES]
    kr_ss = jnp.sum(krp * krp, axis=-1, keepdims=True)
    kn_w = kn_ref[...]
    for hd in range(n_heads):
        kn = kn_all[:, hd * QK_NOPE:(hd + 1) * QK_NOPE]
        ss = (jnp.sum(kn * kn, axis=-1, keepdims=True) + kr_ss) * (1.0 / QK_HEAD)
        rinv = lax.rsqrt(ss + EPS)
        k_ref[:, hd * QK_PAD:hd * QK_PAD + LANES] = (kn * rinv * kn_w[:, :LANES]).astype(BF16)
        k_ref[:, hd * QK_PAD + LANES:(hd + 1) * QK_PAD] = _rope128(
            krp * rinv * kn_w[:, LANES:], cos, sin).astype(BF16)


def _proj(xt, mods, cos_t, sin_t, nmix, win, qan, wuq, kvan, wk, wv, qn, kn, *, n_batch, seq):
    t, d = xt.shape
    tm = ROW_TILE
    n_heads = wv.shape[1] // V_HEAD
    row = lambda w: pl.BlockSpec((tm, w), lambda i: (i, 0))
    kern = functools.partial(_proj_kernel, tiles_per_seq=seq // tm, n_batch=n_batch,
                             n_heads=n_heads, d_model=d)
    return pl.pallas_call(
        kern,
        grid=(t // tm,),
        in_specs=[row(d), _full(mods.shape), row(LANES), row(LANES), _full(nmix.shape), _full(win.shape),
                  _full(qan.shape), _full(wuq.shape), _full(kvan.shape), _full(wk.shape), _full(wv.shape),
                  _full(qn.shape), _full(kn.shape)],
        out_specs=[row(d // 4), row(d // 4), row(n_heads * QK_PAD), row(n_heads * QK_PAD),
                   row(n_heads * V_HEAD)],
        out_shape=[jax.ShapeDtypeStruct((t, d // 4), F32), jax.ShapeDtypeStruct((t, d // 4), F32),
                   jax.ShapeDtypeStruct((t, n_heads * QK_PAD), BF16),
                   jax.ShapeDtypeStruct((t, n_heads * QK_PAD), BF16),
                   jax.ShapeDtypeStruct((t, n_heads * V_HEAD), BF16)],
        compiler_params=_cparams(("arbitrary",)),
        name="mixer_in",
    )(xt, mods, cos_t, sin_t, nmix, win, qan, wuq, kvan, wk, wv, qn, kn)


def _pool_kernel(prev_ref, cur_ref, next_ref, w_ref, scale_ref, o_ref, pad_ref, *,
                 tm, n_lat_tiles, seq, ctx_len):
    i = pl.program_id(0)
    hw = POOL_HALO
    pad_ref[0:hw, :] = prev_ref[...]
    pad_ref[hw:hw + tm, :] = cur_ref[...]
    pad_ref[hw + tm:2 * hw + tm, :] = next_ref[...]
    is_lat = i < n_lat_tiles
    row0 = i * tm
    s0 = jnp.where(is_lat, (row0 // seq) * seq,
                   n_lat_tiles * tm + ((row0 - n_lat_tiles * tm) // ctx_len) * ctx_len)
    slen = jnp.where(is_lat, seq, ctx_len)
    t = row0 - s0 + lax.broadcasted_iota(jnp.int32, (tm, 1), 0)
    for g, w in enumerate(POOL_WINDOWS):
        lanes = slice(g * LANES, (g + 1) * LANES)
        acc = jnp.zeros((tm, LANES), F32)
        for kk in range(-(w // 2), w - w // 2):
            valid = (t + kk >= 0) & (t + kk < slen)
            acc = acc + jnp.where(valid, pad_ref[hw + kk:hw + kk + tm, lanes], 0.0)
        lo = jnp.maximum(t - w // 2, 0)
        hi = jnp.minimum(t + (w - w // 2), slen)
        mean = acc / (hi - lo).astype(F32)
        dlt = (mean - cur_ref[:, lanes]).astype(BF16)
        o_ref[:, lanes] = jnp.dot(dlt, w_ref[g], preferred_element_type=F32) * scale_ref[:, lanes]


def _pool(u_pool, w_pool, pool_scale, *, n_batch, seq, ctx_len):
    t, dp = u_pool.shape
    tm = ROW_TILE
    hb = tm // POOL_HALO
    n_hblocks = t // POOL_HALO
    kern = functools.partial(_pool_kernel, tm=tm, n_lat_tiles=n_batch * seq // tm, seq=seq, ctx_len=ctx_len)
    return pl.pallas_call(
        kern,
        grid=(t // tm,),
        in_specs=[
            pl.BlockSpec((POOL_HALO, dp), lambda i: (jnp.maximum(i * hb - 1, 0), 0)),
            pl.BlockSpec((tm, dp), lambda i: (i, 0)),
            pl.BlockSpec((POOL_HALO, dp), lambda i: (jnp.minimum((i + 1) * hb, n_hblocks - 1), 0)),
            _full(w_pool.shape), _full(pool_scale.shape),
        ],
        out_specs=pl.BlockSpec((tm, dp), lambda i: (i, 0)),
        out_shape=jax.ShapeDtypeStruct((t, dp), F32),
        scratch_shapes=[pltpu.VMEM((tm + 2 * POOL_HALO, dp), F32)],
        compiler_params=_cparams(("arbitrary",)),
        name="pool",
    )(u_pool, u_pool, u_pool, w_pool, pool_scale)


def _s5_prep_kernel(are_ref, aim_ref, ls_ref, btr_ref, bti_ref, ctr_ref, cti_ref, crr_ref, cri_ref,
                    mt_ref, bcr_ref, bci_ref, g1r_ref, g1i_ref, lr_ref, li_ref):
    nrow = S5_CHUNK * SSM_GROUP
    jr = (lax.broadcasted_iota(jnp.int32, (nrow, 1), 0) // SSM_GROUP).astype(F32)
    lane_blk = lax.broadcasted_iota(jnp.int32, (nrow, LANES), 1) // SSM_GROUP
    hi = lax.Precision.HIGHEST
    for d in range(2):
        step = jnp.exp(ls_ref[d, 0])
        ar = are_ref[d, 0]
        ai = aim_ref[d, 0]
        la = ar * step
        th = ai * step
        mag = jnp.exp(la)
        lam_r = mag * jnp.cos(th)
        lam_i = mag * jnp.sin(th)
        den = ar * ar + ai * ai
        nr = lam_r - 1.0
        coef_r = (nr * ar + lam_i * ai) / den
        coef_i = (lam_i * ar - nr * ai) / den

        def powtab(m):
            mg = jnp.exp(m * la)
            an = m * th
            return mg * jnp.cos(an), mg * jnp.sin(an)

        up_r, up_i = powtab(jr)
        dn_r, dn_i = powtab((S5_CHUNK - 1.0) - jr)
        if d == 0:
            (mr, mi), (orr, oi) = (dn_r, dn_i), (up_r, up_i)
        else:
            (mr, mi), (orr, oi) = (up_r, up_i), (dn_r, dn_i)
        btr = btr_ref[d, 0]
        bti = bti_ref[d, 0]
        bpr = coef_r * btr - coef_i * bti
        bpi = coef_r * bti + coef_i * btr
        bmr = bpr * mr - bpi * mi
        bmi = bpr * mi + bpi * mr
        bcr_ref[d, 0] = bmr.astype(BF16)
        bci_ref[d, 0] = bmi.astype(BF16)
        krep = (jnp.dot(bmr, ctr_ref[d, 0], precision=hi, preferred_element_type=F32)
                - jnp.dot(bmi, cti_ref[d, 0], precision=hi, preferred_element_type=F32))
        o_r = orr * lam_r - oi * lam_i
        o_i = orr * lam_i + oi * lam_r
        crr = crr_ref[d, 0]
        cri = cri_ref[d, 0]
        g1r_ref[d, 0] = (crr * o_r - cri * o_i).astype(BF16)
        g1i_ref[d, 0] = (-(crr * o_i + cri * o_r)).astype(BF16)
        lr_ref[d, 0], li_ref[d, 0] = powtab(float(S5_CHUNK))
        blocks_per_tile = LANES // SSM_GROUP
        for qt in range(nrow // LANES):
            src = krep[:, qt * LANES:(qt + 1) * LANES]
            acc = jnp.zeros((nrow, LANES), F32)
            for jj in range(blocks_per_tile):
                j = qt * blocks_per_tile + jj
                if d == 0:
                    s = SSM_GROUP * (S5_CHUNK - 1 - j)
                    sh = src if s == 0 else jnp.concatenate([src[s:], jnp.zeros((s, LANES), F32)], axis=0)
                else:
                    s = SSM_GROUP * j
                    sh = src if s == 0 else jnp.concatenate([jnp.zeros((s, LANES), F32), src[:nrow - s]], axis=0)
                acc = jnp.where(lane_blk == jj, sh, acc)
            mt_ref[d, 0, :, qt * LANES:(qt + 1) * LANES] = acc.astype(BF16)


def _s5_prep(a_re, a_im, log_step, b_re, b_im, c_re, c_im):
    _, g, p = a_re.shape
    nrow = S5_CHUNK * SSM_GROUP
    bt = lambda b: jnp.tile(jnp.swapaxes(b, 2, 3), (1, 1, S5_CHUNK, 1))
    ct = lambda c: jnp.tile(jnp.swapaxes(c, 2, 3), (1, 1, 1, S5_CHUNK))
    cr = lambda c: jnp.tile(c, (1, 1, S5_CHUNK, 1))
    blk = lambda *s: pl.BlockSpec((2, 1) + s, lambda i: (0, i, 0, 0))
    out_shapes = [jax.ShapeDtypeStruct((2, g, nrow, nrow), BF16)] + \
        [jax.ShapeDtypeStruct((2, g, nrow, p), BF16)] * 4 + [jax.ShapeDtypeStruct((2, g, 1, p), F32)] * 2
    return pl.pallas_call(
        _s5_prep_kernel,
        grid=(g,),
        in_specs=[blk(1, p), blk(1, p), blk(1, 1), blk(nrow, p), blk(nrow, p), blk(p, nrow), blk(p, nrow),
                  blk(nrow, p), blk(nrow, p)],
        out_specs=[blk(nrow, nrow)] + [blk(nrow, p)] * 4 + [blk(1, p)] * 2,
        out_shape=out_shapes,
        compiler_params=_cparams(("arbitrary",)),
        name="s5_prep",
    )(a_re.reshape(2, g, 1, p), a_im.reshape(2, g, 1, p), log_step.reshape(2, g, 1, 1),
      bt(b_re), bt(b_im), ct(c_re), ct(c_im), cr(c_re), cr(c_im))


def _s5_kernel(u_ref, mt_ref, bcr_ref, bci_ref, g1r_ref, g1i_ref, lr_ref, li_ref, y_ref,
               er_ref, ei_ref, sr_ref, si_ref, *, n_batch, n_lat_ch, n_ctx_ch):
    ub = u_ref[0].astype(BF16)
    ctx0 = n_batch * n_lat_ch
    total = None
    for d in range(2):
        rev = d == 1
        er_ref[...] = jnp.dot(ub, bcr_ref[d, 0], preferred_element_type=F32)
        ei_ref[...] = jnp.dot(ub, bci_ref[d, 0], preferred_element_type=F32)
        lam_r = lr_ref[d, 0]
        lam_i = li_ref[d, 0]

        def segment(base, stride, n, carry):
            def body(s, c):
                pos = (n - 1 - s) if rev else s
                new = []
                for b in range(n_batch):
                    row = base + b * stride + pos
                    s_r, s_i = c[2 * b], c[2 * b + 1]
                    sr_ref[pl.ds(row, 1), :] = s_r
                    si_ref[pl.ds(row, 1), :] = s_i
                    new.append(lam_r * s_r - lam_i * s_i + er_ref[pl.ds(row, 1), :])
                    new.append(lam_r * s_i + lam_i * s_r + ei_ref[pl.ds(row, 1), :])
                return tuple(new)
            return lax.fori_loop(0, n, body, carry)

        zero = jnp.zeros((1, SSM_STATE), F32)
        carry = segment(ctx0, n_ctx_ch, n_ctx_ch, (zero,) * (2 * n_batch))
        segment(0, n_lat_ch, n_lat_ch, carry)
        dn = (((1,), (1,)), ((), ()))
        y = (jnp.dot(ub, mt_ref[d, 0], preferred_element_type=F32)
             + lax.dot_general(sr_ref[...].astype(BF16), g1r_ref[d, 0], dn, preferred_element_type=F32)
             + lax.dot_general(si_ref[...].astype(BF16), g1i_ref[d, 0], dn, preferred_element_type=F32))
        total = y if total is None else total + y
    y_ref[0] = total


def _s5(u_g, prep, *, n_batch, seq, ctx_len):
    g, nch, w = u_g.shape
    mt, bcr, bci, g1r, g1i, lr, li = prep
    p = SSM_STATE
    blk = lambda *s: pl.BlockSpec((2, 1) + s, lambda i: (0, i, 0, 0))
    kern = functools.partial(_s5_kernel, n_batch=n_batch, n_lat_ch=seq // S5_CHUNK, n_ctx_ch=ctx_len // S5_CHUNK)
    return pl.pallas_call(
        kern,
        grid=(g,),
        in_specs=[pl.BlockSpec((1, nch, w), lambda i: (i, 0, 0)), blk(w, w), blk(w, p), blk(w, p), blk(w, p),
                  blk(w, p), blk(1, p), blk(1, p)],
        out_specs=pl.BlockSpec((1, nch, w), lambda i: (i, 0, 0)),
        out_shape=jax.ShapeDtypeStruct((g, nch, w), F32),
        scratch_shapes=[pltpu.VMEM((nch, p), F32)] * 4,
        compiler_params=_cparams(("arbitrary",)),
        name="s5_scan",
    )(u_g, mt, bcr, bci, g1r, g1i, lr, li)


_NT = (((1,), (1,)), ((), ()))


def _attn_lat_kernel(q_ref, kl_ref, vl_ref, kc_ref, vc_ref, o_ref):
    q = q_ref[...]
    s1 = lax.dot_general(q, kl_ref[...], _NT, preferred_element_type=F32)
    s2 = lax.dot_general(q, kc_ref[...], _NT, preferred_element_type=F32)
    m = jnp.maximum(jnp.max(s1, axis=-1, keepdims=True), jnp.max(s2, axis=-1, keepdims=True))
    p1 = jnp.exp(s1 - m)
    p2 = jnp.exp(s2 - m)
    l = jnp.sum(p1, axis=-1, keepdims=True) + jnp.sum(p2, axis=-1, keepdims=True)
    o = (jnp.dot(p1.astype(BF16), vl_ref[...], preferred_element_type=F32)
         + jnp.dot(p2.astype(BF16), vc_ref[...], preferred_element_type=F32))
    o_ref[...] = o / l


def _attn_ctx_kernel(q_ref, kc_ref, vc_ref, o_ref):
    s = lax.dot_general(q_ref[...], kc_ref[...], _NT, preferred_element_type=F32)
    p = jnp.exp(s - jnp.max(s, axis=-1, keepdims=True))
    l = jnp.sum(p, axis=-1, keepdims=True)
    o_ref[...] = jnp.dot(p.astype(BF16), vc_ref[...], preferred_element_type=F32) / l


def _attention(q, k, v, *, n_batch, seq, ctx_len, ctx_queries):
    n_heads = v.shape[1] // V_HEAD
    tq = ATTN_Q_TILE
    nq = seq // tq
    cblk = n_batch * seq // ctx_len
    y_lat = pl.pallas_call(
        _attn_lat_kernel,
        grid=(n_batch, n_heads, nq),
        in_specs=[
            pl.BlockSpec((tq, QK_PAD), lambda b, h, i: (b * nq + i, h)),
            pl.BlockSpec((seq, QK_PAD), lambda b, h, i: (b, h)),
            pl.BlockSpec((seq, V_HEAD), lambda b, h, i: (b, h)),
            pl.BlockSpec((ctx_len, QK_PAD), lambda b, h, i: (cblk + b, h)),
            pl.BlockSpec((ctx_len, V_HEAD), lambda b, h, i: (cblk + b, h)),
        ],
        out_specs=pl.BlockSpec((tq, V_HEAD), lambda b, h, i: (b * nq + i, h)),
        out_shape=jax.ShapeDtypeStruct((n_batch * seq, n_heads * V_HEAD), F32),
        compiler_params=_cparams(("arbitrary", "arbitrary", "arbitrary")),
        name="attn_latent",
    )(q, k, v, k, v)
    if not ctx_queries:
        return y_lat, None
    y_ctx = pl.pallas_call(
        _attn_ctx_kernel,
        grid=(n_batch, n_heads),
        in_specs=[
            pl.BlockSpec((ctx_len, QK_PAD), lambda b, h: (cblk + b, h)),
            pl.BlockSpec((ctx_len, QK_PAD), lambda b, h: (cblk + b, h)),
            pl.BlockSpec((ctx_len, V_HEAD), lambda b, h: (cblk + b, h)),
        ],
        out_specs=pl.BlockSpec((ctx_len, V_HEAD), lambda b, h: (b, h)),
        out_shape=jax.ShapeDtypeStruct((n_batch * ctx_len, n_heads * V_HEAD), F32),
        compiler_params=_cparams(("arbitrary", "arbitrary")),
        name="attn_context",
    )(q, k, v)
    return y_lat, y_ctx


def _merge_kernel(x_ref, yp_ref, ys_ref, us_ref, yml_ref, *rest, tiles_per_seq, n_batch, d_model, has_ctx):
    if has_ctx:
        ymc_ref, rest = rest[0], rest[1:]
    (mod_ref, sd_ref, wglu_ref, bglu_ref, on_ref, wout_ref, nffn_ref, wr_ref, br_ref,
     x1_ref, h2_ref, idx_ref, gate_ref, rank_ref, cnt_ref, carry_ref) = rest
    d = d_model
    d4 = d // 4
    i = pl.program_id(0)
    tm = x_ref.shape[0]
    r = jnp.minimum(i // tiles_per_seq, n_batch)
    g1 = mod_ref[pl.ds(r, 1), pl.ds(2 * d, d)]
    sh2 = mod_ref[pl.ds(r, 1), pl.ds(3 * d, d)]
    sc2 = mod_ref[pl.ds(r, 1), pl.ds(4 * d, d)]

    z = ys_ref[...] + sd_ref[...] * us_ref[...]
    g = jax.nn.gelu(z, approximate=True)
    glu = jnp.dot(g.astype(BF16), wglu_ref[...], preferred_element_type=F32) + bglu_ref[...]
    y_ssm = g * jax.nn.sigmoid(glu)
    on = on_ref[...]
    y_mla = yml_ref[...]
    if has_ctx:
        y_mla = jnp.where(i < n_batch * tiles_per_seq, y_mla, ymc_ref[...])
    cat = jnp.concatenate([_rms(yp_ref[...], on[:, 0:d4]), _rms(y_ssm, on[:, d4:2 * d4]),
                           _rms(y_mla, on[:, 2 * d4:])], axis=-1).astype(BF16)
    x1 = x_ref[...] + g1 * jnp.dot(cat, wout_ref[...], preferred_element_type=F32)
    x1_ref[...] = x1
    h2 = (_rms(x1, nffn_ref[...]) * (1.0 + sc2) + sh2).astype(BF16)
    h2_ref[...] = h2

    logits = jnp.dot(h2, wr_ref[...], preferred_element_type=F32) + br_ref[...]
    lane = lax.broadcasted_iota(jnp.int32, (tm, LANES), 1)
    vals, hots = [], []
    idx_out = jnp.zeros((tm, LANES), jnp.int32)
    for kk in range(TOP_K):
        mx = jnp.max(logits, axis=-1, keepdims=True)
        sel = jnp.min(jnp.where(logits == mx, lane, LANES), axis=-1, keepdims=True)
        hot = lane == sel
        vals.append(mx)
        hots.append(hot)
        idx_out = jnp.where(lane == kk, sel, idx_out)
        logits = jnp.where(hot, NEG_BIG * 2.0, logits)
    es = [jnp.exp(vv - vals[0]) for vv in vals]
    den = es[0] + es[1] + es[2] + es[3]
    gate_out = jnp.zeros((tm, LANES), F32)
    for kk in range(TOP_K):
        gate_out = jnp.where(lane == kk, es[kk] / den, gate_out)
    idx_ref[...] = idx_out
    gate_ref[...] = gate_out

    @pl.when(i == 0)
    def _():
        carry_ref[...] = jnp.zeros_like(carry_ref)

    onehot = (hots[0] | hots[1] | hots[2] | hots[3]).astype(BF16)
    rr = lax.broadcasted_iota(jnp.int32, (tm, tm), 0)
    cc = lax.broadcasted_iota(jnp.int32, (tm, tm), 1)
    tri = (cc < rr).astype(BF16)
    pos = jnp.dot(tri, onehot, preferred_element_type=F32) + carry_ref[...]
    rank_out = jnp.zeros((tm, LANES), jnp.int32)
    for kk in range(TOP_K):
        rk = jnp.sum(jnp.where(hots[kk], pos, 0.0), axis=-1, keepdims=True).astype(jnp.int32)
        rank_out = jnp.where(lane == kk, rk, rank_out)
    rank_ref[...] = rank_out
    carry = carry_ref[...] + jnp.sum(onehot.astype(F32), axis=0, keepdims=True)
    carry_ref[...] = carry
    cnt_ref[...] = carry.astype(jnp.int32)


def _merge(xt, y_pool, y_s5, u_ssm, y_mla_lat, y_mla_ctx, mods, ssm_d, wglu, bglu, out_norm, wout, nffn, wr, br, *,
           n_rows, n_batch, seq):
    d = xt.shape[1]
    tm = ROW_TILE
    n_lat_tiles = n_batch * seq // tm
    has_ctx = y_mla_ctx is not None
    row = lambda w: pl.BlockSpec((tm, w), lambda i: (i, 0))
    kern = functools.partial(_merge_kernel, tiles_per_seq=seq // tm, n_batch=n_batch, d_model=d, has_ctx=has_ctx)
    small = [mods, ssm_d, wglu, bglu, out_norm, wout, nffn, wr, br]
    y_mla = [y_mla_lat] + ([y_mla_ctx] if has_ctx else [])
    y_specs = [pl.BlockSpec((tm, d // 2), lambda i: (jnp.minimum(i, n_lat_tiles - 1), 0))]
    if has_ctx:
        y_specs.append(pl.BlockSpec((tm, d // 2), lambda i: (jnp.maximum(i - n_lat_tiles, 0), 0)))
    return pl.pallas_call(
        kern,
        grid=(n_rows // tm,),
        in_specs=[row(d), row(d // 4), row(d // 4), row(d // 4)] + y_specs + [_full(a.shape) for a in small],
        out_specs=[row(d), row(d), row(LANES), row(LANES), row(LANES), _full((1, LANES))],
        out_shape=[jax.ShapeDtypeStruct((n_rows, d), F32), jax.ShapeDtypeStruct((n_rows, d), BF16),
                   jax.ShapeDtypeStruct((n_rows, LANES), jnp.int32), jax.ShapeDtypeStruct((n_rows, LANES), F32),
                   jax.ShapeDtypeStruct((n_rows, LANES), jnp.int32), jax.ShapeDtypeStruct((1, LANES), jnp.int32)],
        scratch_shapes=[pltpu.VMEM((1, LANES), F32)],
        compiler_params=_cparams(("arbitrary",)),
        name="merge_router",
    )(xt, y_pool, y_s5, u_ssm, *y_mla, *small)


def _row_copy(src_hbm, dst_hbm, sem, src_row, dst_row):
    return pltpu.make_async_copy(src_hbm.at[src_row], dst_hbm.at[dst_row], sem)


def _dispatch_kernel(dest_hbm, h2_hbm, zeros_hbm, xs_hbm, dest_smem, sem_idx, sem_rows, *, tm):
    del zeros_hbm
    i = pl.program_id(0)
    n = tm * TOP_K
    idx_copy = pltpu.make_async_copy(dest_hbm.at[i], dest_smem, sem_idx)
    idx_copy.start()
    idx_copy.wait()

    def issue(j, _):
        _row_copy(h2_hbm, xs_hbm, sem_rows, i * tm + j // TOP_K, dest_smem[j]).start()
        return 0

    lax.fori_loop(0, n, issue, 0)

    def drain(j, _):
        _row_copy(h2_hbm, xs_hbm, sem_rows, i * tm + j // TOP_K, dest_smem[j]).wait()
        return 0

    lax.fori_loop(0, n, drain, 0)


def _dispatch(dest_tiles, h2_3d, n_sorted_rows):
    n_tok = h2_3d.shape[0]
    tm = ROW_TILE
    zeros = jnp.zeros((n_sorted_rows,) + h2_3d.shape[1:], h2_3d.dtype)
    return pl.pallas_call(
        functools.partial(_dispatch_kernel, tm=tm),
        grid=(n_tok // tm,),
        in_specs=[pl.BlockSpec(memory_space=pl.ANY)] * 3,
        out_specs=pl.BlockSpec(memory_space=pl.ANY),
        out_shape=jax.ShapeDtypeStruct(zeros.shape, zeros.dtype),
        input_output_aliases={2: 0},
        scratch_shapes=[pltpu.SMEM((tm * TOP_K,), jnp.int32), pltpu.SemaphoreType.DMA, pltpu.SemaphoreType.DMA],
        compiler_params=_cparams(("arbitrary",)),
        name="moe_dispatch",
    )(dest_tiles, h2_3d, zeros)


def _expert_kernel(ie_ref, ib_ref, ns_ref, x_ref, wg_ref, bg_ref, wu_ref, bu_ref, wd_ref, bd_ref, o_ref,
                   wg_s, wu_s, wd_s):
    it = pl.program_id(0)
    f = pl.program_id(1)
    ns = ns_ref[it]
    sub = EXPERT_SUB_ROWS

    @pl.when(ns > 0)
    def _():
        wg_s[...] = wg_ref[0].astype(BF16)
        wu_s[...] = wu_ref[0].astype(BF16)
        wd_s[...] = wd_ref[0].astype(BF16)

    for sb in range(EXPERT_ITEM_ROWS // sub):
        rows = slice(sb * sub, (sb + 1) * sub)

        @pl.when(sb < ns)
        def _():
            x = x_ref[rows, :]
            g = jnp.minimum(jnp.dot(x, wg_s[...], preferred_element_type=F32) + bg_ref[0], SWIGLU_LIMIT)
            u = jnp.clip(jnp.dot(x, wu_s[...], preferred_element_type=F32) + bu_ref[0],
                         -SWIGLU_LIMIT, SWIGLU_LIMIT)
            act = (g * jax.nn.sigmoid(SWIGLU_ALPHA * g) * (u + 1.0)).astype(BF16)
            contrib = jnp.dot(act, wd_s[...], preferred_element_type=F32)

            @pl.when(f == 0)
            def _():
                o_ref[rows, :] = contrib + bd_ref[0]

            @pl.when(f > 0)
            def _():
                o_ref[rows, :] += contrib

        @pl.when((sb >= ns) & (f == 0))
        def _():
            o_ref[rows, :] = jnp.zeros((sub, o_ref.shape[1]), F32)


def _experts(item_e, item_blk, item_nsub, xs, w_gate, b_gate, w_up, b_up, w_down, b_down):
    n_items = item_e.shape[0]
    n_exp, d, f_dim = w_gate.shape
    tf = EXPERT_F_TILE
    nf = f_dim // tf
    rows = EXPERT_ITEM_ROWS

    def fidx(f, ns, it):
        return jnp.where(ns[it] > 0, f, nf - 1)

    return pl.pallas_call(
        _expert_kernel,
        grid_spec=pltpu.PrefetchScalarGridSpec(
            num_scalar_prefetch=3,
            grid=(n_items, nf),
            in_specs=[
                pl.BlockSpec((rows, d), lambda it, f, ie, ib, ns: (ib[it], 0)),
                pl.BlockSpec((1, d, tf), lambda it, f, ie, ib, ns: (ie[it], 0, fidx(f, ns, it))),
                pl.BlockSpec((1, 1, tf), lambda it, f, ie, ib, ns: (ie[it], 0, fidx(f, ns, it))),
                pl.BlockSpec((1, d, tf), lambda it, f, ie, ib, ns: (ie[it], 0, fidx(f, ns, it))),
                pl.BlockSpec((1, 1, tf), lambda it, f, ie, ib, ns: (ie[it], 0, fidx(f, ns, it))),
                pl.BlockSpec((1, tf, d), lambda it, f, ie, ib, ns: (ie[it], fidx(f, ns, it), 0)),
                pl.BlockSpec((1, 1, d), lambda it, f, ie, ib, ns: (ie[it], 0, 0)),
            ],
            out_specs=pl.BlockSpec((rows, d), lambda it, f, ie, ib, ns: (it, 0)),
            scratch_shapes=[pltpu.VMEM((d, tf), BF16), pltpu.VMEM((d, tf), BF16), pltpu.VMEM((tf, d), BF16)],
        ),
        out_shape=jax.ShapeDtypeStruct((n_items * rows, d), F32),
        compiler_params=_cparams(("arbitrary", "arbitrary")),
        name="moe_experts",
    )(item_e, item_blk, item_nsub, xs, w_gate, b_gate.reshape(n_exp, 1, f_dim), w_up,
      b_up.reshape(n_exp, 1, f_dim), w_down, b_down.reshape(n_exp, 1, d))


def _combine_kernel(dest_hbm, gate_hbm, os_hbm, x_ref, g2_ref, o_ref, dest_smem, gate_smem, buf, sem_idx, sem_rows,
                    *, tc, tiles_per_seq, n_batch):
    i = pl.program_id(0)
    n = tc * TOP_K
    c1 = pltpu.make_async_copy(dest_hbm.at[i], dest_smem, sem_idx)
    c2 = pltpu.make_async_copy(gate_hbm.at[i], gate_smem, sem_idx)
    c1.start()
    c2.start()
    c1.wait()
    c2.wait()

    def issue(j, _):
        _row_copy(os_hbm, buf, sem_rows, dest_smem[j], j).start()
        return 0

    lax.fori_loop(0, n, issue, 0)

    def drain(j, _):
        _row_copy(os_hbm, buf, sem_rows, dest_smem[j], j).wait()
        return 0

    lax.fori_loop(0, n, drain, 0)
    g2 = g2_ref[jnp.minimum(i // tiles_per_seq, n_batch)]

    def row(rw, _):
        acc = gate_smem[rw * TOP_K] * buf[rw * TOP_K]
        for kk in range(1, TOP_K):
            acc = acc + gate_smem[rw * TOP_K + kk] * buf[rw * TOP_K + kk]
        o_ref[rw] = x_ref[rw] + g2 * acc
        return 0

    lax.fori_loop(0, tc, row, 0)


def _combine(dest_tiles, gate_tiles, out_sorted_3d, x1_3d, g2_3d, *, n_batch, seq):
    n_tok, sl, ln = x1_3d.shape
    tc = COMBINE_TILE
    kern = functools.partial(_combine_kernel, tc=tc, tiles_per_seq=seq // tc, n_batch=n_batch)
    return pl.pallas_call(
        kern,
        grid=(n_tok // tc,),
        in_specs=[pl.BlockSpec(memory_space=pl.ANY)] * 3 + [
            pl.BlockSpec((tc, sl, ln), lambda i: (i, 0, 0)), _full(g2_3d.shape)],
        out_specs=pl.BlockSpec((tc, sl, ln), lambda i: (i, 0, 0)),
        out_shape=jax.ShapeDtypeStruct(x1_3d.shape, F32),
        scratch_shapes=[pltpu.SMEM((tc * TOP_K,), jnp.int32), pltpu.SMEM((tc * TOP_K,), F32),
                        pltpu.VMEM((tc * TOP_K, sl, ln), F32), pltpu.SemaphoreType.DMA, pltpu.SemaphoreType.DMA],
        compiler_params=_cparams(("arbitrary",)),
        name="moe_combine",
    )(dest_tiles, gate_tiles, out_sorted_3d, x1_3d, g2_3d)


def _rope_tables(n_batch, seq, ctx_len):
    rows = seq // GRID_W
    row = jnp.broadcast_to(jnp.arange(rows)[:, None], (rows, GRID_W)).reshape(-1).astype(F32)
    col = jnp.broadcast_to(jnp.arange(GRID_W)[None, :], (rows, GRID_W)).reshape(-1).astype(F32)
    half = QK_ROPE // 2
    inv_freq = ROPE_BASE ** (-jnp.arange(0, half, 2, dtype=F32) / half)
    ang = jnp.concatenate([row[:, None] * inv_freq, col[:, None] * inv_freq], axis=-1)
    cos, sin = jnp.cos(ang), jnp.sin(ang)
    zpad = jnp.zeros((seq, LANES - QK_ROPE), F32)
    cos_l = jnp.concatenate([cos, cos, zpad], axis=-1)
    sin_l = jnp.concatenate([-sin, sin, zpad], axis=-1)
    n_ctx = n_batch * ctx_len
    cos_c = jnp.concatenate([jnp.ones((n_ctx, QK_ROPE), F32), jnp.zeros((n_ctx, LANES - QK_ROPE), F32)], axis=-1)
    cos_t = jnp.concatenate([jnp.tile(cos_l, (n_batch, 1)), cos_c], axis=0)
    sin_t = jnp.concatenate([jnp.tile(sin_l, (n_batch, 1)), jnp.zeros((n_ctx, LANES), F32)], axis=0)
    return cos_t, sin_t


def _pad_head_vec(vec):
    return jnp.concatenate([vec, jnp.zeros((QK_PAD - QK_HEAD,), vec.dtype)])[None, :]


def _routing(idx, rank, counts, n_tok, n_exp):
    rows = EXPERT_ITEM_ROWS
    n_items = -(-(n_tok * TOP_K) // rows) + n_exp
    cnt = counts[0, :n_exp]
    blocks = (cnt + rows - 1) // rows
    blk_end = jnp.cumsum(blocks)
    blk_start = blk_end - blocks
    dest = blk_start[idx] * rows + rank
    total = blk_end[-1]
    it = jnp.arange(n_items, dtype=jnp.int32)
    it_c = jnp.minimum(it, total - 1)
    e_of = jnp.minimum(jnp.searchsorted(blk_end, it_c, side="right"), n_exp - 1).astype(jnp.int32)
    left = cnt[e_of] - (it_c - blk_start[e_of]) * rows
    nsub = jnp.clip((left + EXPERT_SUB_ROWS - 1) // EXPERT_SUB_ROWS, 0, rows // EXPERT_SUB_ROWS)
    nsub = jnp.where(it < total, nsub, 0).astype(jnp.int32)
    return dest.astype(jnp.int32), e_of, it_c.astype(jnp.int32), nsub, n_items


def _layer(xt, mods, cos_t, sin_t, p, *, n_batch, seq, ctx_len, ctx_out):
    t, d = xt.shape
    d4 = d // 4
    n_heads = (d - 2 * d4) // V_HEAD
    kv_lora = d // 8
    bf = lambda a: a.astype(BF16)
    row = lambda a: a[None, :]

    win = bf(jnp.pad(p["w_in"], ((0, 0), (0, LANES - QK_ROPE))))
    wuq = p["w_uq"].reshape(-1, n_heads, QK_HEAD)
    wuq = bf(jnp.pad(wuq, ((0, 0), (0, 0), (0, QK_PAD - QK_HEAD))).reshape(-1, n_heads * QK_PAD))
    wukv = p["w_ukv"].reshape(kv_lora, n_heads, QK_NOPE + V_HEAD)
    wk = bf(wukv[:, :, :QK_NOPE].reshape(kv_lora, n_heads * QK_NOPE))
    wv = bf(wukv[:, :, QK_NOPE:].reshape(kv_lora, n_heads * V_HEAD))
    qn = _pad_head_vec(p["q_norm"]) * (QK_HEAD ** -0.5)
    kn = _pad_head_vec(p["k_norm"])

    u_pool, u_ssm, q, k, v = _proj(xt, mods, cos_t, sin_t, row(p["norm_mix"]), win, row(p["q_a_norm"]), wuq,
                                   row(p["kv_a_norm"]), wk, wv, qn, kn, n_batch=n_batch, seq=seq)

    y_pool = _pool(u_pool, bf(p["w_pool"]), row(p["pool_scale"]), n_batch=n_batch, seq=seq, ctx_len=ctx_len)

    n_groups = d4 // SSM_GROUP
    nch = t // S5_CHUNK
    prep = _s5_prep(p["ssm_a_re"], p["ssm_a_im"], p["ssm_log_step"], p["ssm_b_re"], p["ssm_b_im"],
                    p["ssm_c_re"], p["ssm_c_im"])
    u_g = u_ssm.reshape(nch, S5_CHUNK, n_groups, SSM_GROUP).transpose(2, 0, 1, 3).reshape(
        n_groups, nch, S5_CHUNK * SSM_GROUP)
    y_g = _s5(u_g, prep, n_batch=n_batch, seq=seq, ctx_len=ctx_len)
    y_s5 = y_g.reshape(n_groups, nch, S5_CHUNK, SSM_GROUP).transpose(1, 2, 0, 3).reshape(t, d4)

    y_mla_lat, y_mla_ctx = _attention(q, k, v, n_batch=n_batch, seq=seq, ctx_len=ctx_len, ctx_queries=ctx_out)

    n_rows = t if ctx_out else n_batch * seq
    n_exp = p["w_router"].shape[1]
    wr = bf(jnp.pad(p["w_router"], ((0, 0), (0, LANES - n_exp))))
    br = jnp.concatenate([p["b_router"], jnp.full((LANES - n_exp,), NEG_BIG, F32)])[None, :]
    x1, h2, idx, gates, rank, counts = _merge(
        xt, y_pool, y_s5, u_ssm, y_mla_lat, y_mla_ctx, mods, row(p["ssm_d"]), bf(p["w_glu"]), row(p["b_glu"]),
        row(p["out_norm"]), bf(p["w_out"]), row(p["norm_ffn"]), wr, br, n_rows=n_rows, n_batch=n_batch, seq=seq)

    dest, item_e, item_blk, item_nsub, n_items = _routing(idx[:, :TOP_K], rank[:, :TOP_K], counts, n_rows, n_exp)
    sl = d // LANES
    xs = _dispatch(dest.reshape(n_rows // ROW_TILE, ROW_TILE * TOP_K), h2.reshape(n_rows, sl, LANES),
                   n_items * EXPERT_ITEM_ROWS)
    out_sorted = _experts(item_e, item_blk, item_nsub, xs.reshape(-1, d), p["w_gate"], p["b_gate"], p["w_up"],
                          p["b_up"], p["w_down"], p["b_down"])
    g2 = mods[:, 5 * d:6 * d].reshape(mods.shape[0], sl, LANES)
    nct = n_rows // COMBINE_TILE
    x2 = _combine(dest.reshape(nct, COMBINE_TILE * TOP_K), gates[:, :TOP_K].reshape(nct, COMBINE_TILE * TOP_K),
                  out_sorted.reshape(-1, sl, LANES), x1.reshape(n_rows, sl, LANES), g2, n_batch=n_batch, seq=seq)
    return x2.reshape(n_rows, d)


_PARAM_NAMES = ("norm_mix", "norm_ffn", "w_in", "w_pool", "pool_scale", "ssm_a_re", "ssm_a_im", "ssm_log_step",
                "ssm_b_re", "ssm_b_im", "ssm_c_re", "ssm_c_im", "ssm_d", "w_glu", "b_glu", "q_a_norm", "w_uq",
                "kv_a_norm", "w_ukv", "q_norm", "k_norm", "out_norm", "w_out", "w_router", "b_router", "w_gate",
                "b_gate", "w_up", "b_up", "w_down", "b_down")


def kernel(x, c, ctx, c_ctx, w_ada, b_ada, norm_mix, norm_ffn, w_in, w_pool, pool_scale, ssm_a_re, ssm_a_im, ssm_log_step, ssm_b_re, ssm_b_im, ssm_c_re, ssm_c_im, ssm_d, w_glu, b_glu, q_a_norm, w_uq, kv_a_norm, w_ukv, q_norm, k_norm, out_norm, w_out, w_router, b_router, w_gate, b_gate, w_up, b_up, w_down, b_down):
    n_batch, seq, d = x.shape
    ctx_len = ctx.shape[1]
    depth = w_ada.shape[0]
    assert seq % ROW_TILE == 0 and ctx_len % ROW_TILE == 0 and seq % GRID_W == 0 and n_batch + 1 <= 8
    stacked = dict(zip(_PARAM_NAMES, (norm_mix, norm_ffn, w_in, w_pool, pool_scale, ssm_a_re, ssm_a_im,
                                      ssm_log_step, ssm_b_re, ssm_b_im, ssm_c_re, ssm_c_im, ssm_d, w_glu, b_glu,
                                      q_a_norm, w_uq, kv_a_norm, w_ukv, q_norm, k_norm, out_norm, w_out, w_router,
                                      b_router, w_gate, b_gate, w_up, b_up, w_down, b_down)))
    cvecs = jnp.concatenate([c, c_ctx[None, :], jnp.zeros((8 - n_batch - 1, d), F32)], axis=0)
    mods = _ada(cvecs, w_ada, b_ada)
    cos_t, sin_t = _rope_tables(n_batch, seq, ctx_len)
    n_lat = n_batch * seq
    xt = jnp.concatenate([x.reshape(n_lat, d), ctx.reshape(n_batch * ctx_len, d)], axis=0)
    for layer in range(depth):
        p = {name: val[layer] for name, val in stacked.items()}
        xt = _layer(xt, mods[layer], cos_t, sin_t, p, n_batch=n_batch, seq=seq, ctx_len=ctx_len,
                    ctx_out=layer < depth - 1)
    return xt[:n_lat].reshape(n_batch, seq, d)
```

```python
import functools
import math

import jax
import jax.numpy as jnp
from jax import lax
from jax.experimental import pallas as pl
from jax.experimental.pallas import tpu as pltpu

F32 = jnp.float32
BF16 = jnp.bfloat16

EPS = 1e-6
GRID_W = 64
POOL_WINDOWS = (2, 4, 8, 16)
POOL_HALO = 8
SSM_GROUP = 16
SSM_STATE = 64
S5_CHUNK = 32
QK_NOPE = 128
QK_ROPE = 64
QK_HEAD = QK_NOPE + QK_ROPE
QK_PAD = 256
V_HEAD = 128
ROPE_BASE = 10000.0
TOP_K = 4
SWIGLU_ALPHA = 1.702
SWIGLU_LIMIT = 7.0

LANES = 128
ROW_TILE = 256
ATTN_Q_TILE = 256
EXPERT_ITEM_ROWS = 1280
EXPERT_SUB_ROWS = 256
EXPERT_TILE = 256
COMBINE_TILE = 128
VMEM_LIMIT = 52 * 1024 * 1024
NEG_BIG = -1e30


def _cparams(sem, vmem=VMEM_LIMIT):
    return pltpu.CompilerParams(dimension_semantics=sem, vmem_limit_bytes=vmem)


def _rms(x, w):
    return x * lax.rsqrt(jnp.mean(x * x, axis=-1, keepdims=True) + EPS) * w


def _full(shape):
    nd = len(shape)
    return pl.BlockSpec(shape, lambda *_: (0,) * nd)


def _ada_kernel(cv_ref, w_ref, b_ref, o_ref):
    cv = cv_ref[...]
    s = (cv * jax.nn.sigmoid(cv)).astype(BF16)
    o_ref[0] = jnp.dot(s, w_ref[0].astype(BF16), preferred_element_type=F32) + b_ref[0]


def _ada(cvecs, w_ada, b_ada):
    n_layers, d, n6 = w_ada.shape
    tn = 512
    return pl.pallas_call(
        _ada_kernel,
        grid=(n_layers, n6 // tn),
        in_specs=[
            pl.BlockSpec((8, d), lambda l, j: (0, 0)),
            pl.BlockSpec((1, d, tn), lambda l, j: (l, 0, j)),
            pl.BlockSpec((1, 1, tn), lambda l, j: (l, 0, j)),
        ],
        out_specs=pl.BlockSpec((1, 8, tn), lambda l, j: (l, 0, j)),
        out_shape=jax.ShapeDtypeStruct((n_layers, 8, n6), F32),
        compiler_params=_cparams(("arbitrary", "arbitrary")),
        name="ada",
    )(cvecs, w_ada, b_ada.reshape(n_layers, 1, n6))


def _rope128(r, cos, sin):
    lane = lax.broadcasted_iota(jnp.int32, r.shape, 1)
    sw = jnp.where((lane % QK_ROPE) < QK_ROPE // 2, pltpu.roll(r, LANES - 32, 1), pltpu.roll(r, 32, 1))
    return r * cos + sw * sin


def _proj_kernel(x_ref, mod_ref, cos_ref, sin_ref, nmix_ref, win_ref, qan_ref, wuq_ref, kvan_ref,
                 wk_ref, wv_ref, qn_ref, kn_ref,
                 upool_ref, ussm_ref, q_ref, k_ref, v_ref, *, tiles_per_seq, n_batch, n_heads, d_model):
    d = d_model
    r = jnp.minimum(pl.program_id(0) // tiles_per_seq, n_batch)
    sh = mod_ref[pl.ds(r, 1), pl.ds(0, d)]
    sc = mod_ref[pl.ds(r, 1), pl.ds(d, d)]
    h = _rms(x_ref[...], nmix_ref[...]) * (1.0 + sc) + sh
    proj = jnp.dot(h.astype(BF16), win_ref[...], preferred_element_type=F32)
    d4 = d // 4
    upool_ref[...] = proj[:, 0:d4]
    ussm_ref[...] = proj[:, 2 * d4:3 * d4]
    cos = cos_ref[...]
    sin = sin_ref[...]

    qa = _rms(proj[:, d4:2 * d4], qan_ref[...]).astype(BF16)
    qf = jnp.dot(qa, wuq_ref[...], preferred_element_type=F32)
    qn_w = qn_ref[...]
    for hd in range(n_heads):
        blk = qf[:, hd * QK_PAD:(hd + 1) * QK_PAD]
        ss = jnp.sum(blk * blk, axis=-1, keepdims=True) * (1.0 / QK_HEAD)
        qn = blk * lax.rsqrt(ss + EPS) * qn_w
        q_ref[:, hd * QK_PAD:hd * QK_PAD + LANES] = qn[:, :LANES].astype(BF16)
        q_ref[:, hd * QK_PAD + LANES:(hd + 1) * QK_PAD] = _rope128(qn[:, LANES:], cos, sin).astype(BF16)

    kv_lo = 3 * d4
    kv_w = d // 8
    ka = _rms(proj[:, kv_lo:kv_lo + kv_w], kvan_ref[...]).astype(BF16)
    kn_all = jnp.dot(ka, wk_ref[...], preferred_element_type=F32)
    v_ref[...] = jnp.dot(ka, wv_ref[...], preferred_element_type=F32).astype(BF16)
    krp = proj[:, kv_lo + kv_w:kv_lo + kv_w + LANES]
    kr_ss = jnp.sum(krp * krp, axis=-1, keepdims=True)
    kn_w = kn_ref[...]
    for hd in range(n_heads):
        kn = kn_all[:, hd * QK_NOPE:(hd + 1) * QK_NOPE]
        ss = (jnp.sum(kn * kn, axis=-1, keepdims=True) + kr_ss) * (1.0 / QK_HEAD)
        rinv = lax.rsqrt(ss + EPS)
        k_ref[:, hd * QK_PAD:hd * QK_PAD + LANES] = (kn * rinv * kn_w[:, :LANES]).astype(BF16)
        k_ref[:, hd * QK_PAD + LANES:(hd + 1) * QK_PAD] = _rope128(
            krp * rinv * kn_w[:, LANES:], cos, sin).astype(BF16)


def _proj(xt, mods, cos_t, sin_t, nmix, win, qan, wuq, kvan, wk, wv, qn, kn, *, n_batch, seq):
    t, d = xt.shape
    tm = ROW_TILE
    n_heads = wv.shape[1] // V_HEAD
    row = lambda w: pl.BlockSpec((tm, w), lambda i: (i, 0))
    kern = functools.partial(_proj_kernel, tiles_per_seq=seq // tm, n_batch=n_batch,
                             n_heads=n_heads, d_model=d)
    return pl.pallas_call(
        kern,
        grid=(t // tm,),
        in_specs=[row(d), _full(mods.shape), row(LANES), row(LANES), _full(nmix.shape), _full(win.shape),
                  _full(qan.shape), _full(wuq.shape), _full(kvan.shape), _full(wk.shape), _full(wv.shape),
                  _full(qn.shape), _full(kn.shape)],
        out_specs=[row(d // 4), row(d // 4), row(n_heads * QK_PAD), row(n_heads * QK_PAD),
                   row(n_heads * V_HEAD)],
        out_shape=[jax.ShapeDtypeStruct((t, d // 4), F32), jax.ShapeDtypeStruct((t, d // 4), F32),
                   jax.ShapeDtypeStruct((t, n_heads * QK_PAD), BF16),
                   jax.ShapeDtypeStruct((t, n_heads * QK_PAD), BF16),
                   jax.ShapeDtypeStruct((t, n_heads * V_HEAD), BF16)],
        compiler_params=_cparams(("arbitrary",)),
        name="mixer_in",
    )(xt, mods, cos_t, sin_t, nmix, win, qan, wuq, kvan, wk, wv, qn, kn)


def _pool_kernel(prev_ref, cur_ref, next_ref, w_ref, scale_ref, o_ref, pad_ref, *,
                 tm, n_lat_tiles, seq, ctx_len):
    i = pl.program_id(0)
    hw = POOL_HALO
    pad_ref[0:hw, :] = prev_ref[...]
    pad_ref[hw:hw + tm, :] = cur_ref[...]
    pad_ref[hw + tm:2 * hw + tm, :] = next_ref[...]
    is_lat = i < n_lat_tiles
    row0 = i * tm
    s0 = jnp.where(is_lat, (row0 // seq) * seq,
                   n_lat_tiles * tm + ((row0 - n_lat_tiles * tm) // ctx_len) * ctx_len)
    slen = jnp.where(is_lat, seq, ctx_len)
    t = row0 - s0 + lax.broadcasted_iota(jnp.int32, (tm, 1), 0)
    for g, w in enumerate(POOL_WINDOWS):
        lanes = slice(g * LANES, (g + 1) * LANES)
        acc = jnp.zeros((tm, LANES), F32)
        for kk in range(-(w // 2), w - w // 2):
            valid = (t + kk >= 0) & (t + kk < slen)
            acc = acc + jnp.where(valid, pad_ref[hw + kk:hw + kk + tm, lanes], 0.0)
        lo = jnp.maximum(t - w // 2, 0)
        hi = jnp.minimum(t + (w - w // 2), slen)
        mean = acc / (hi - lo).astype(F32)
        dlt = (mean - cur_ref[:, lanes]).astype(BF16)
        o_ref[:, lanes] = jnp.dot(dlt, w_ref[g], preferred_element_type=F32) * scale_ref[:, lanes]


def _pool(u_pool, w_pool, pool_scale, *, n_batch, seq, ctx_len):
    t, dp = u_pool.shape
    tm = ROW_TILE
    hb = tm // POOL_HALO
    n_hblocks = t // POOL_HALO
    kern = functools.partial(_pool_kernel, tm=tm, n_lat_tiles=n_batch * seq // tm, seq=seq, ctx_len=ctx_len)
    return pl.pallas_call(
        kern,
        grid=(t // tm,),
        in_specs=[
            pl.BlockSpec((POOL_HALO, dp), lambda i: (jnp.maximum(i * hb - 1, 0), 0)),
            pl.BlockSpec((tm, dp), lambda i: (i, 0)),
            pl.BlockSpec((POOL_HALO, dp), lambda i: (jnp.minimum((i + 1) * hb, n_hblocks - 1), 0)),
            _full(w_pool.shape), _full(pool_scale.shape),
        ],
        out_specs=pl.BlockSpec((tm, dp), lambda i: (i, 0)),
        out_shape=jax.ShapeDtypeStruct((t, dp), F32),
        scratch_shapes=[pltpu.VMEM((tm + 2 * POOL_HALO, dp), F32)],
        compiler_params=_cparams(("arbitrary",)),
        name="pool",
    )(u_pool, u_pool, u_pool, w_pool, pool_scale)


def _s5_prep_kernel(are_ref, aim_ref, ls_ref, btr_ref, bti_ref, ctr_ref, cti_ref, crr_ref, cri_ref,
                    mt_ref, bcr_ref, bci_ref, g1r_ref, g1i_ref, lr_ref, li_ref):
    nrow = S5_CHUNK * SSM_GROUP
    jr = (lax.broadcasted_iota(jnp.int32, (nrow, 1), 0) // SSM_GROUP).astype(F32)
    lane_blk = lax.broadcasted_iota(jnp.int32, (nrow, LANES), 1) // SSM_GROUP
    hi = lax.Precision.HIGHEST
    for d in range(2):
        step = jnp.exp(ls_ref[d, 0])
        ar = are_ref[d, 0]
        ai = aim_ref[d, 0]
        la = ar * step
        th = ai * step
        mag = jnp.exp(la)
        lam_r = mag * jnp.cos(th)
        lam_i = mag * jnp.sin(th)
        den = ar * ar + ai * ai
        nr = lam_r - 1.0
        coef_r = (nr * ar + lam_i * ai) / den
        coef_i = (lam_i * ar - nr * ai) / den

        def powtab(m):
            mg = jnp.exp(m * la)
            an = m * th
            return mg * jnp.cos(an), mg * jnp.sin(an)

        up_r, up_i = powtab(jr)
        dn_r, dn_i = powtab((S5_CHUNK - 1.0) - jr)
        if d == 0:
            (mr, mi), (orr, oi) = (dn_r, dn_i), (up_r, up_i)
        else:
            (mr, mi), (orr, oi) = (up_r, up_i), (dn_r, dn_i)
        btr = btr_ref[d, 0]
        bti = bti_ref[d, 0]
        bpr = coef_r * btr - coef_i * bti
        bpi = coef_r * bti + coef_i * btr
        bmr = bpr * mr - bpi * mi
        bmi = bpr * mi + bpi * mr
        bcr_ref[d, 0] = bmr.astype(BF16)
        bci_ref[d, 0] = bmi.astype(BF16)
        krep = (jnp.dot(bmr, ctr_ref[d, 0], precision=hi, preferred_element_type=F32)
                - jnp.dot(bmi, cti_ref[d, 0], precision=hi, preferred_element_type=F32))
        o_r = orr * lam_r - oi * lam_i
        o_i = orr * lam_i + oi * lam_r
        crr = crr_ref[d, 0]
        cri = cri_ref[d, 0]
        g1r_ref[d, 0] = (crr * o_r - cri * o_i).astype(BF16)
        g1i_ref[d, 0] = (-(crr * o_i + cri * o_r)).astype(BF16)
        lr_ref[d, 0], li_ref[d, 0] = powtab(float(S5_CHUNK))
        blocks_per_tile = LANES // SSM_GROUP
        for qt in range(nrow // LANES):
            src = krep[:, qt * LANES:(qt + 1) * LANES]
            acc = jnp.zeros((nrow, LANES), F32)
            for jj in range(blocks_per_tile):
                j = qt * blocks_per_tile + jj
                if d == 0:
                    s = SSM_GROUP * (S5_CHUNK - 1 - j)
                    sh = src if s == 0 else jnp.concatenate([src[s:], jnp.zeros((s, LANES), F32)], axis=0)
                else:
                    s = SSM_GROUP * j
                    sh = src if s == 0 else jnp.concatenate([jnp.zeros((s, LANES), F32), src[:nrow - s]], axis=0)
                acc = jnp.where(lane_blk == jj, sh, acc)
            mt_ref[d, 0, :, qt * LANES:(qt + 1) * LANES] = acc.astype(BF16)


def _s5_prep(a_re, a_im, log_step, b_re, b_im, c_re, c_im):
    _, g, p = a_re.shape
    nrow = S5_CHUNK * SSM_GROUP
    bt = lambda b: jnp.tile(jnp.swapaxes(b, 2, 3), (1, 1, S5_CHUNK, 1))
    ct = lambda c: jnp.tile(jnp.swapaxes(c, 2, 3), (1, 1, 1, S5_CHUNK))
    cr = lambda c: jnp.tile(c, (1, 1, S5_CHUNK, 1))
    blk = lambda *s: pl.BlockSpec((2, 1) + s, lambda i: (0, i, 0, 0))
    out_shapes = [jax.ShapeDtypeStruct((2, g, nrow, nrow), BF16)] + \
        [jax.ShapeDtypeStruct((2, g, nrow, p), BF16)] * 4 + [jax.ShapeDtypeStruct((2, g, 1, p), F32)] * 2
    return pl.pallas_call(
        _s5_prep_kernel,
        grid=(g,),
        in_specs=[blk(1, p), blk(1, p), blk(1, 1), blk(nrow, p), blk(nrow, p), blk(p, nrow), blk(p, nrow),
                  blk(nrow, p), blk(nrow, p)],
        out_specs=[blk(nrow, nrow)] + [blk(nrow, p)] * 4 + [blk(1, p)] * 2,
        out_shape=out_shapes,
        compiler_params=_cparams(("arbitrary",)),
        name="s5_prep",
    )(a_re.reshape(2, g, 1, p), a_im.reshape(2, g, 1, p), log_step.reshape(2, g, 1, 1),
      bt(b_re), bt(b_im), ct(c_re), ct(c_im), cr(c_re), cr(c_im))


def _s5_kernel(u_ref, mt_ref, bcr_ref, bci_ref, g1r_ref, g1i_ref, lr_ref, li_ref, y_ref,
               er_ref, ei_ref, sr_ref, si_ref, *, n_batch, n_lat_ch, n_ctx_ch):
    ub = u_ref[0].astype(BF16)
    ctx0 = n_batch * n_lat_ch
    total = None
    for d in range(2):
        rev = d == 1
        er_ref[...] = jnp.dot(ub, bcr_ref[d, 0], preferred_element_type=F32)
        ei_ref[...] = jnp.dot(ub, bci_ref[d, 0], preferred_element_type=F32)
        lam_r = lr_ref[d, 0]
        lam_i = li_ref[d, 0]

        def segment(base, stride, n, carry):
            def body(s, c):
                pos = (n - 1 - s) if rev else s
                new = []
                for b in range(n_batch):
                    row = base + b * stride + pos
                    s_r, s_i = c[2 * b], c[2 * b + 1]
                    sr_ref[pl.ds(row, 1), :] = s_r
                    si_ref[pl.ds(row, 1), :] = s_i
                    new.append(lam_r * s_r - lam_i * s_i + er_ref[pl.ds(row, 1), :])
                    new.append(lam_r * s_i + lam_i * s_r + ei_ref[pl.ds(row, 1), :])
                return tuple(new)
            return lax.fori_loop(0, n, body, carry)

        zero = jnp.zeros((1, SSM_STATE), F32)
        carry = segment(ctx0, n_ctx_ch, n_ctx_ch, (zero,) * (2 * n_batch))
        segment(0, n_lat_ch, n_lat_ch, carry)
        dn = (((1,), (1,)), ((), ()))
        y = (jnp.dot(ub, mt_ref[d, 0], preferred_element_type=F32)
             + lax.dot_general(sr_ref[...].astype(BF16), g1r_ref[d, 0], dn, preferred_element_type=F32)
             + lax.dot_general(si_ref[...].astype(BF16), g1i_ref[d, 0], dn, preferred_element_type=F32))
        total = y if total is None else total + y
    y_ref[0] = total


def _s5(u_g, prep, *, n_batch, seq, ctx_len):
    g, nch, w = u_g.shape
    mt, bcr, bci, g1r, g1i, lr, li = prep
    p = SSM_STATE
    blk = lambda *s: pl.BlockSpec((2, 1) + s, lambda i: (0, i, 0, 0))
    kern = functools.partial(_s5_kernel, n_batch=n_batch, n_lat_ch=seq // S5_CHUNK, n_ctx_ch=ctx_len // S5_CHUNK)
    return pl.pallas_call(
        kern,
        grid=(g,),
        in_specs=[pl.BlockSpec((1, nch, w), lambda i: (i, 0, 0)), blk(w, w), blk(w, p), blk(w, p), blk(w, p),
                  blk(w, p), blk(1, p), blk(1, p)],
        out_specs=pl.BlockSpec((1, nch, w), lambda i: (i, 0, 0)),
        out_shape=jax.ShapeDtypeStruct((g, nch, w), F32),
        scratch_shapes=[pltpu.VMEM((nch, p), F32)] * 4,
        compiler_params=_cparams(("arbitrary",)),
        name="s5_scan",
    )(u_g, mt, bcr, bci, g1r, g1i, lr, li)


_NT = (((1,), (1,)), ((), ()))


def _attn_lat_kernel(q_ref, kl_ref, vl_ref, kc_ref, vc_ref, o_ref):
    q = q_ref[...]
    s1 = lax.dot_general(q, kl_ref[...], _NT, preferred_element_type=F32)
    s2 = lax.dot_general(q, kc_ref[...], _NT, preferred_element_type=F32)
    m = jnp.maximum(jnp.max(s1, axis=-1, keepdims=True), jnp.max(s2, axis=-1, keepdims=True))
    p1 = jnp.exp(s1 - m)
    p2 = jnp.exp(s2 - m)
    l = jnp.sum(p1, axis=-1, keepdims=True) + jnp.sum(p2, axis=-1, keepdims=True)
    o = (jnp.dot(p1.astype(BF16), vl_ref[...], preferred_element_type=F32)
         + jnp.dot(p2.astype(BF16), vc_ref[...], preferred_element_type=F32))
    o_ref[...] = o / l


def _attn_ctx_kernel(q_ref, kc_ref, vc_ref, o_ref):
    s = lax.dot_general(q_ref[...], kc_ref[...], _NT, preferred_element_type=F32)
    p = jnp.exp(s - jnp.max(s, axis=-1, keepdims=True))
    l = jnp.sum(p, axis=-1, keepdims=True)
    o_ref[...] = jnp.dot(p.astype(BF16), vc_ref[...], preferred_element_type=F32) / l


def _attention(q, k, v, *, n_batch, seq, ctx_len, ctx_queries):
    n_heads = v.shape[1] // V_HEAD
    tq = ATTN_Q_TILE
    nq = seq // tq
    cblk = n_batch * seq // ctx_len
    y_lat = pl.pallas_call(
        _attn_lat_kernel,
        grid=(n_batch, n_heads, nq),
        in_specs=[
            pl.BlockSpec((tq, QK_PAD), lambda b, h, i: (b * nq + i, h)),
            pl.BlockSpec((seq, QK_PAD), lambda b, h, i: (b, h)),
            pl.BlockSpec((seq, V_HEAD), lambda b, h, i: (b, h)),
            pl.BlockSpec((ctx_len, QK_PAD), lambda b, h, i: (cblk + b, h)),
            pl.BlockSpec((ctx_len, V_HEAD), lambda b, h, i: (cblk + b, h)),
        ],
        out_specs=pl.BlockSpec((tq, V_HEAD), lambda b, h, i: (b * nq + i, h)),
        out_shape=jax.ShapeDtypeStruct((n_batch * seq, n_heads * V_HEAD), F32),
        compiler_params=_cparams(("arbitrary", "arbitrary", "arbitrary")),
        name="attn_latent",
    )(q, k, v, k, v)
    if not ctx_queries:
        return y_lat, None
    y_ctx = pl.pallas_call(
        _attn_ctx_kernel,
        grid=(n_batch, n_heads),
        in_specs=[
            pl.BlockSpec((ctx_len, QK_PAD), lambda b, h: (cblk + b, h)),
            pl.BlockSpec((ctx_len, QK_PAD), lambda b, h: (cblk + b, h)),
            pl.BlockSpec((ctx_len, V_HEAD), lambda b, h: (cblk + b, h)),
        ],
        out_specs=pl.BlockSpec((ctx_len, V_HEAD), lambda b, h: (b, h)),
        out_shape=jax.ShapeDtypeStruct((n_batch * ctx_len, n_heads * V_HEAD), F32),
        compiler_params=_cparams(("arbitrary", "arbitrary")),
        name="attn_context",
    )(q, k, v)
    return y_lat, y_ctx


def _merge_kernel(x_ref, yp_ref, ys_ref, us_ref, yml_ref, *rest, tiles_per_seq, n_batch, d_model, has_ctx):
    if has_ctx:
        ymc_ref, rest = rest[0], rest[1:]
    (mod_ref, sd_ref, wglu_ref, bglu_ref, on_ref, wout_ref, nffn_ref, wr_ref, br_ref,
     x1_ref, h2_ref, idx_ref, gate_ref, rank_ref, cnt_ref, carry_ref) = rest
    d = d_model
    d4 = d // 4
    i = pl.program_id(0)
    tm = x_ref.shape[0]
    r = jnp.minimum(i // tiles_per_seq, n_batch)
    g1 = mod_ref[pl.ds(r, 1), pl.ds(2 * d, d)]
    sh2 = mod_ref[pl.ds(r, 1), pl.ds(3 * d, d)]
    sc2 = mod_ref[pl.ds(r, 1), pl.ds(4 * d, d)]

    z = ys_ref[...] + sd_ref[...] * us_ref[...]
    g = jax.nn.gelu(z, approximate=True)
    glu = jnp.dot(g.astype(BF16), wglu_ref[...], preferred_element_type=F32) + bglu_ref[...]
    y_ssm = g * jax.nn.sigmoid(glu)
    on = on_ref[...]
    y_mla = yml_ref[...]
    if has_ctx:
        y_mla = jnp.where(i < n_batch * tiles_per_seq, y_mla, ymc_ref[...])
    cat = jnp.concatenate([_rms(yp_ref[...], on[:, 0:d4]), _rms(y_ssm, on[:, d4:2 * d4]),
                           _rms(y_mla, on[:, 2 * d4:])], axis=-1).astype(BF16)
    x1 = x_ref[...] + g1 * jnp.dot(cat, wout_ref[...], preferred_element_type=F32)
    x1_ref[...] = x1
    h2 = (_rms(x1, nffn_ref[...]) * (1.0 + sc2) + sh2).astype(BF16)
    h2_ref[...] = h2

    logits = jnp.dot(h2, wr_ref[...], preferred_element_type=F32) + br_ref[...]
    lane = lax.broadcasted_iota(jnp.int32, (tm, LANES), 1)
    vals, hots = [], []
    idx_out = jnp.zeros((tm, LANES), jnp.int32)
    for kk in range(TOP_K):
        mx = jnp.max(logits, axis=-1, keepdims=True)
        sel = jnp.min(jnp.where(logits == mx, lane, LANES), axis=-1, keepdims=True)
        hot = lane == sel
        vals.append(mx)
        hots.append(hot)
        idx_out = jnp.where(lane == kk, sel, idx_out)
        logits = jnp.where(hot, NEG_BIG * 2.0, logits)
    es = [jnp.exp(vv - vals[0]) for vv in vals]
    den = es[0] + es[1] + es[2] + es[3]
    gate_out = jnp.zeros((tm, LANES), F32)
    for kk in range(TOP_K):
        gate_out = jnp.where(lane == kk, es[kk] / den, gate_out)
    idx_ref[...] = idx_out
    gate_ref[...] = gate_out

    @pl.when(i == 0)
    def _():
        carry_ref[...] = jnp.zeros_like(carry_ref)

    onehot = (hots[0] | hots[1] | hots[2] | hots[3]).astype(BF16)
    rr = lax.broadcasted_iota(jnp.int32, (tm, tm), 0)
    cc = lax.broadcasted_iota(jnp.int32, (tm, tm), 1)
    tri = (cc < rr).astype(BF16)
    pos = jnp.dot(tri, onehot, preferred_element_type=F32) + carry_ref[...]
    rank_out = jnp.zeros((tm, LANES), jnp.int32)
    for kk in range(TOP_K):
        rk = jnp.sum(jnp.where(hots[kk], pos, 0.0), axis=-1, keepdims=True).astype(jnp.int32)
        rank_out = jnp.where(lane == kk, rk, rank_out)
    rank_ref[...] = rank_out
    carry = carry_ref[...] + jnp.sum(onehot.astype(F32), axis=0, keepdims=True)
    carry_ref[...] = carry
    cnt_ref[...] = carry.astype(jnp.int32)


def _merge(xt, y_pool, y_s5, u_ssm, y_mla_lat, y_mla_ctx, mods, ssm_d, wglu, bglu, out_norm, wout, nffn, wr, br, *,
           n_rows, n_batch, seq):
    d = xt.shape[1]
    tm = ROW_TILE
    n_lat_tiles = n_batch * seq // tm
    has_ctx = y_mla_ctx is not None
    row = lambda w: pl.BlockSpec((tm, w), lambda i: (i, 0))
    kern = functools.partial(_merge_kernel, tiles_per_seq=seq // tm, n_batch=n_batch, d_model=d, has_ctx=has_ctx)
    small = [mods, ssm_d, wglu, bglu, out_norm, wout, nffn, wr, br]
    y_mla = [y_mla_lat] + ([y_mla_ctx] if has_ctx else [])
    y_specs = [pl.BlockSpec((tm, d // 2), lambda i: (jnp.minimum(i, n_lat_tiles - 1), 0))]
    if has_ctx:
        y_specs.append(pl.BlockSpec((tm, d // 2), lambda i: (jnp.maximum(i - n_lat_tiles, 0), 0)))
    return pl.pallas_call(
        kern,
        grid=(n_rows // tm,),
        in_specs=[row(d), row(d // 4), row(d // 4), row(d // 4)] + y_specs + [_full(a.shape) for a in small],
        out_specs=[row(d), row(d), row(LANES), row(LANES), row(LANES), _full((1, LANES))],
        out_shape=[jax.ShapeDtypeStruct((n_rows, d), F32), jax.ShapeDtypeStruct((n_rows, d), BF16),
                   jax.ShapeDtypeStruct((n_rows, LANES), jnp.int32), jax.ShapeDtypeStruct((n_rows, LANES), F32),
                   jax.ShapeDtypeStruct((n_rows, LANES), jnp.int32), jax.ShapeDtypeStruct((1, LANES), jnp.int32)],
        scratch_shapes=[pltpu.VMEM((1, LANES), F32)],
        compiler_params=_cparams(("arbitrary",)),
        name="merge_router",
    )(xt, y_pool, y_s5, u_ssm, *y_mla, *small)


def _row_copy(src_hbm, dst_hbm, sem, src_row, dst_row):
    return pltpu.make_async_copy(src_hbm.at[src_row], dst_hbm.at[dst_row], sem)


def _dispatch_kernel(dest_hbm, h2_ref, zeros_hbm, xs_hbm, dest_smem, sem_idx, sem_rows, *, tm):
    del zeros_hbm
    i = pl.program_id(0)
    n = tm * TOP_K
    idx_copy = pltpu.make_async_copy(dest_hbm.at[i], dest_smem, sem_idx)
    idx_copy.start()
    idx_copy.wait()

    def issue(j, _):
        _row_copy(h2_ref, xs_hbm, sem_rows, j // TOP_K, dest_smem[j]).start()
        return 0

    lax.fori_loop(0, n, issue, 0, unroll=8)

    def drain(j, _):
        _row_copy(h2_ref, xs_hbm, sem_rows, j // TOP_K, dest_smem[j]).wait()
        return 0

    lax.fori_loop(0, n, drain, 0, unroll=8)


def _dispatch(dest_tiles, h2_3d, n_sorted_rows):
    n_tok, sl, ln = h2_3d.shape
    tm = ROW_TILE
    zeros = jnp.zeros((n_sorted_rows,) + h2_3d.shape[1:], h2_3d.dtype)
    return pl.pallas_call(
        functools.partial(_dispatch_kernel, tm=tm),
        grid=(n_tok // tm,),
        in_specs=[pl.BlockSpec(memory_space=pl.ANY), pl.BlockSpec((tm, sl, ln), lambda i: (i, 0, 0)),
                  pl.BlockSpec(memory_space=pl.ANY)],
        out_specs=pl.BlockSpec(memory_space=pl.ANY),
        out_shape=jax.ShapeDtypeStruct(zeros.shape, zeros.dtype),
        input_output_aliases={2: 0},
        scratch_shapes=[pltpu.SMEM((tm * TOP_K,), jnp.int32), pltpu.SemaphoreType.DMA, pltpu.SemaphoreType.DMA],
        compiler_params=_cparams(("arbitrary",)),
        name="moe_dispatch",
    )(dest_tiles, h2_3d, zeros)


def _expert_kernel(ie_ref, ib_ref, ns_ref, x_ref, wg_ref, bg_ref, wu_ref, bu_ref, wd_ref, bd_ref, o_ref, act_ref,
                   *, nf):
    it = pl.program_id(0)
    j = pl.program_id(1)
    ns = ns_ref[it]
    rows_max = x_ref.shape[0]
    for m in range(1, rows_max // EXPERT_SUB_ROWS + 1):
        mr = m * EXPERT_SUB_ROWS

        @pl.when((ns == m) & (j < nf))
        def _():
            x = x_ref[0:mr, :]
            g = jnp.dot(x, wg_ref[0, 0].astype(BF16), preferred_element_type=F32) + bg_ref[0, 0]
            u = jnp.dot(x, wu_ref[0, 0].astype(BF16), preferred_element_type=F32) + bu_ref[0, 0]
            g = jnp.minimum(g, SWIGLU_LIMIT)
            u = jnp.clip(u, -SWIGLU_LIMIT, SWIGLU_LIMIT)
            act_ref[j, 0:mr, :] = (g * jax.nn.sigmoid(SWIGLU_ALPHA * g) * (u + 1.0)).astype(BF16)

        @pl.when((ns == m) & (j >= nf))
        def _():
            act = jnp.concatenate([act_ref[f, 0:mr, :] for f in range(nf)], axis=-1)
            o_ref[0:mr, :] = jnp.dot(act, wd_ref[0, 0].astype(BF16), preferred_element_type=F32) + bd_ref[0, 0]
            if mr < rows_max:
                o_ref[mr:, :] = jnp.zeros((rows_max - mr, o_ref.shape[1]), F32)

    @pl.when((ns == 0) & (j >= nf))
    def _():
        o_ref[...] = jnp.zeros(o_ref.shape, F32)


def _experts(layer, item_e, item_blk, item_nsub, xs, w_gate, b_gate, w_up, b_up, w_down, b_down):
    n_items = item_e.shape[0]
    n_layers, n_exp, d, f_dim = w_gate.shape
    tf = EXPERT_TILE
    nf = f_dim // tf
    rows = EXPERT_ITEM_ROWS
    assert d // tf == nf

    def fi(j, ns, it):
        return jnp.where(ns[it] > 0, jnp.minimum(j, nf - 1), nf - 1)

    def ni(j, ns, it):
        return jnp.where(ns[it] > 0, jnp.maximum(j - nf, 0), nf - 1)

    return pl.pallas_call(
        functools.partial(_expert_kernel, nf=nf),
        grid_spec=pltpu.PrefetchScalarGridSpec(
            num_scalar_prefetch=3,
            grid=(n_items, 2 * nf),
            in_specs=[
                pl.BlockSpec((rows, d), lambda it, j, ie, ib, ns: (ib[it], 0)),
                pl.BlockSpec((1, 1, d, tf), lambda it, j, ie, ib, ns: (layer, ie[it], 0, fi(j, ns, it))),
                pl.BlockSpec((1, 1, 1, tf), lambda it, j, ie, ib, ns: (layer, ie[it], 0, fi(j, ns, it))),
                pl.BlockSpec((1, 1, d, tf), lambda it, j, ie, ib, ns: (layer, ie[it], 0, fi(j, ns, it))),
                pl.BlockSpec((1, 1, 1, tf), lambda it, j, ie, ib, ns: (layer, ie[it], 0, fi(j, ns, it))),
                pl.BlockSpec((1, 1, f_dim, tf), lambda it, j, ie, ib, ns: (layer, ie[it], 0, ni(j, ns, it))),
                pl.BlockSpec((1, 1, 1, tf), lambda it, j, ie, ib, ns: (layer, ie[it], 0, ni(j, ns, it))),
            ],
            out_specs=pl.BlockSpec((rows, tf), lambda it, j, ie, ib, ns: (it, jnp.maximum(j - nf, 0))),
            scratch_shapes=[pltpu.VMEM((nf, rows, tf), BF16)],
        ),
        out_shape=jax.ShapeDtypeStruct((n_items * rows, d), F32),
        compiler_params=_cparams(("arbitrary", "arbitrary")),
        name="moe_experts",
    )(item_e, item_blk, item_nsub, xs, w_gate, b_gate.reshape(n_layers, n_exp, 1, f_dim), w_up,
      b_up.reshape(n_layers, n_exp, 1, f_dim), w_down, b_down.reshape(n_layers, n_exp, 1, d))


def _combine_kernel(dest_hbm, gate_hbm, os_hbm, x_ref, g2_ref, o_ref, dest_smem, gate_smem, buf, sem_idx, sem_rows,
                    *, tc, n_tiles, tiles_per_seq, n_batch):
    i = pl.program_id(0)
    n = tc * TOP_K
    slot = i % 2

    def fetch(tile, s):
        idx_copy = pltpu.make_async_copy(dest_hbm.at[tile], dest_smem.at[s], sem_idx)
        idx_copy.start()
        idx_copy.wait()

        def issue(j, _):
            _row_copy(os_hbm, buf.at[s], sem_rows.at[s], dest_smem[s, j], j).start()
            return 0

        lax.fori_loop(0, n, issue, 0, unroll=8)

    @pl.when(i == 0)
    def _():
        fetch(0, 0)

    @pl.when(i + 1 < n_tiles)
    def _():
        fetch(i + 1, 1 - slot)

    gate_copy = pltpu.make_async_copy(gate_hbm.at[i], gate_smem, sem_idx)
    gate_copy.start()
    gate_copy.wait()

    def drain(j, _):
        _row_copy(os_hbm, buf.at[slot], sem_rows.at[slot], dest_smem[slot, j], j).wait()
        return 0

    lax.fori_loop(0, n, drain, 0, unroll=8)
    g2 = g2_ref[jnp.minimum(i // tiles_per_seq, n_batch)]
    cur = buf.at[slot]

    def row(rw, _):
        acc = gate_smem[rw * TOP_K] * cur[rw * TOP_K]
        for kk in range(1, TOP_K):
            acc = acc + gate_smem[rw * TOP_K + kk] * cur[rw * TOP_K + kk]
        o_ref[rw] = x_ref[rw] + g2 * acc
        return 0

    lax.fori_loop(0, tc, row, 0, unroll=2)


def _combine(dest_tiles, gate_tiles, out_sorted_3d, x1_3d, g2_3d, *, n_batch, seq):
    n_tok, sl, ln = x1_3d.shape
    tc = COMBINE_TILE
    kern = functools.partial(_combine_kernel, tc=tc, n_tiles=n_tok // tc, tiles_per_seq=seq // tc, n_batch=n_batch)
    return pl.pallas_call(
        kern,
        grid=(n_tok // tc,),
        in_specs=[pl.BlockSpec(memory_space=pl.ANY)] * 3 + [
            pl.BlockSpec((tc, sl, ln), lambda i: (i, 0, 0)), _full(g2_3d.shape)],
        out_specs=pl.BlockSpec((tc, sl, ln), lambda i: (i, 0, 0)),
        out_shape=jax.ShapeDtypeStruct(x1_3d.shape, F32),
        scratch_shapes=[pltpu.SMEM((2, tc * TOP_K), jnp.int32), pltpu.SMEM((tc * TOP_K,), F32),
                        pltpu.VMEM((2, tc * TOP_K, sl, ln), F32), pltpu.SemaphoreType.DMA,
                        pltpu.SemaphoreType.DMA((2,))],
        compiler_params=_cparams(("arbitrary",)),
        name="moe_combine",
    )(dest_tiles, gate_tiles, out_sorted_3d, x1_3d, g2_3d)


def _rope_tables(n_batch, seq, ctx_len):
    rows = seq // GRID_W
    row = jnp.broadcast_to(jnp.arange(rows)[:, None], (rows, GRID_W)).reshape(-1).astype(F32)
    col = jnp.broadcast_to(jnp.arange(GRID_W)[None, :], (rows, GRID_W)).reshape(-1).astype(F32)
    half = QK_ROPE // 2
    inv_freq = ROPE_BASE ** (-jnp.arange(0, half, 2, dtype=F32) / half)
    ang = jnp.concatenate([row[:, None] * inv_freq, col[:, None] * inv_freq], axis=-1)
    cos, sin = jnp.cos(ang), jnp.sin(ang)
    zpad = jnp.zeros((seq, LANES - QK_ROPE), F32)
    cos_l = jnp.concatenate([cos, cos, zpad], axis=-1)
    sin_l = jnp.concatenate([-sin, sin, zpad], axis=-1)
    n_ctx = n_batch * ctx_len
    cos_c = jnp.concatenate([jnp.ones((n_ctx, QK_ROPE), F32), jnp.zeros((n_ctx, LANES - QK_ROPE), F32)], axis=-1)
    cos_t = jnp.concatenate([jnp.tile(cos_l, (n_batch, 1)), cos_c], axis=0)
    sin_t = jnp.concatenate([jnp.tile(sin_l, (n_batch, 1)), jnp.zeros((n_ctx, LANES), F32)], axis=0)
    return cos_t, sin_t


def _pad_head_vec(vec):
    return jnp.concatenate([vec, jnp.zeros((QK_PAD - QK_HEAD,), vec.dtype)])[None, :]


def _routing(idx, rank, counts, n_tok, n_exp):
    rows = EXPERT_ITEM_ROWS
    n_items = -(-(n_tok * TOP_K) // rows) + n_exp
    cnt = counts[0, :n_exp]
    blocks = (cnt + rows - 1) // rows
    blk_end = jnp.cumsum(blocks)
    blk_start = blk_end - blocks
    dest = blk_start[idx] * rows + rank
    total = blk_end[-1]
    it = jnp.arange(n_items, dtype=jnp.int32)
    it_c = jnp.minimum(it, total - 1)
    e_of = jnp.minimum(jnp.searchsorted(blk_end, it_c, side="right"), n_exp - 1).astype(jnp.int32)
    left = cnt[e_of] - (it_c - blk_start[e_of]) * rows
    nsub = jnp.clip((left + EXPERT_SUB_ROWS - 1) // EXPERT_SUB_ROWS, 0, rows // EXPERT_SUB_ROWS)
    nsub = jnp.where(it < total, nsub, 0).astype(jnp.int32)
    return dest.astype(jnp.int32), e_of, it_c.astype(jnp.int32), nsub, n_items


def _layer(layer, xt, mods, cos_t, sin_t, p, expert_params, *, n_batch, seq, ctx_len, ctx_out):
    t, d = xt.shape
    d4 = d // 4
    n_heads = (d - 2 * d4) // V_HEAD
    kv_lora = d // 8
    bf = lambda a: a.astype(BF16)
    row = lambda a: a[None, :]

    win = bf(jnp.pad(p["w_in"], ((0, 0), (0, LANES - QK_ROPE))))
    wuq = p["w_uq"].reshape(-1, n_heads, QK_HEAD)
    wuq = bf(jnp.pad(wuq, ((0, 0), (0, 0), (0, QK_PAD - QK_HEAD))).reshape(-1, n_heads * QK_PAD))
    wukv = p["w_ukv"].reshape(kv_lora, n_heads, QK_NOPE + V_HEAD)
    wk = bf(wukv[:, :, :QK_NOPE].reshape(kv_lora, n_heads * QK_NOPE))
    wv = bf(wukv[:, :, QK_NOPE:].reshape(kv_lora, n_heads * V_HEAD))
    qn = _pad_head_vec(p["q_norm"]) * (QK_HEAD ** -0.5)
    kn = _pad_head_vec(p["k_norm"])

    u_pool, u_ssm, q, k, v = _proj(xt, mods, cos_t, sin_t, row(p["norm_mix"]), win, row(p["q_a_norm"]), wuq,
                                   row(p["kv_a_norm"]), wk, wv, qn, kn, n_batch=n_batch, seq=seq)

    y_pool = _pool(u_pool, bf(p["w_pool"]), row(p["pool_scale"]), n_batch=n_batch, seq=seq, ctx_len=ctx_len)

    n_groups = d4 // SSM_GROUP
    nch = t // S5_CHUNK
    prep = _s5_prep(p["ssm_a_re"], p["ssm_a_im"], p["ssm_log_step"], p["ssm_b_re"], p["ssm_b_im"],
                    p["ssm_c_re"], p["ssm_c_im"])
    u_g = u_ssm.reshape(nch, S5_CHUNK, n_groups, SSM_GROUP).transpose(2, 0, 1, 3).reshape(
        n_groups, nch, S5_CHUNK * SSM_GROUP)
    y_g = _s5(u_g, prep, n_batch=n_batch, seq=seq, ctx_len=ctx_len)
    y_s5 = y_g.reshape(n_groups, nch, S5_CHUNK, SSM_GROUP).transpose(1, 2, 0, 3).reshape(t, d4)

    y_mla_lat, y_mla_ctx = _attention(q, k, v, n_batch=n_batch, seq=seq, ctx_len=ctx_len, ctx_queries=ctx_out)

    n_rows = t if ctx_out else n_batch * seq
    n_exp = p["w_router"].shape[1]
    wr = bf(jnp.pad(p["w_router"], ((0, 0), (0, LANES - n_exp))))
    br = jnp.concatenate([p["b_router"], jnp.full((LANES - n_exp,), NEG_BIG, F32)])[None, :]
    x1, h2, idx, gates, rank, counts = _merge(
        xt, y_pool, y_s5, u_ssm, y_mla_lat, y_mla_ctx, mods, row(p["ssm_d"]), bf(p["w_glu"]), row(p["b_glu"]),
        row(p["out_norm"]), bf(p["w_out"]), row(p["norm_ffn"]), wr, br, n_rows=n_rows, n_batch=n_batch, seq=seq)

    dest, item_e, item_blk, item_nsub, n_items = _routing(idx[:, :TOP_K], rank[:, :TOP_K], counts, n_rows, n_exp)
    sl = d // LANES
    xs = _dispatch(dest.reshape(n_rows // ROW_TILE, ROW_TILE * TOP_K), h2.reshape(n_rows, sl, LANES),
                   n_items * EXPERT_ITEM_ROWS)
    out_sorted = _experts(layer, item_e, item_blk, item_nsub, xs.reshape(-1, d), *expert_params)
    g2 = mods[:, 5 * d:6 * d].reshape(mods.shape[0], sl, LANES)
    nct = n_rows // COMBINE_TILE
    x2 = _combine(dest.reshape(nct, COMBINE_TILE * TOP_K), gates[:, :TOP_K].reshape(nct, COMBINE_TILE * TOP_K),
                  out_sorted.reshape(-1, sl, LANES), x1.reshape(n_rows, sl, LANES), g2, n_batch=n_batch, seq=seq)
    return x2.reshape(n_rows, d)


_PARAM_NAMES = ("norm_mix", "norm_ffn", "w_in", "w_pool", "pool_scale", "ssm_a_re", "ssm_a_im", "ssm_log_step",
                "ssm_b_re", "ssm_b_im", "ssm_c_re", "ssm_c_im", "ssm_d", "w_glu", "b_glu", "q_a_norm", "w_uq",
                "kv_a_norm", "w_ukv", "q_norm", "k_norm", "out_norm", "w_out", "w_router", "b_router")


def kernel(x, c, ctx, c_ctx, w_ada, b_ada, norm_mix, norm_ffn, w_in, w_pool, pool_scale, ssm_a_re, ssm_a_im, ssm_log_step, ssm_b_re, ssm_b_im, ssm_c_re, ssm_c_im, ssm_d, w_glu, b_glu, q_a_norm, w_uq, kv_a_norm, w_ukv, q_norm, k_norm, out_norm, w_out, w_router, b_router, w_gate, b_gate, w_up, b_up, w_down, b_down):
    n_batch, seq, d = x.shape
    ctx_len = ctx.shape[1]
    depth = w_ada.shape[0]
    assert seq % ROW_TILE == 0 and ctx_len % ROW_TILE == 0 and seq % GRID_W == 0 and n_batch + 1 <= 8
    stacked = dict(zip(_PARAM_NAMES, (norm_mix, norm_ffn, w_in, w_pool, pool_scale, ssm_a_re, ssm_a_im,
                                      ssm_log_step, ssm_b_re, ssm_b_im, ssm_c_re, ssm_c_im, ssm_d, w_glu, b_glu,
                                      q_a_norm, w_uq, kv_a_norm, w_ukv, q_norm, k_norm, out_norm, w_out, w_router,
                                      b_router)))
    expert_params = (w_gate, b_gate, w_up, b_up, w_down, b_down)
    cvecs = jnp.concatenate([c, c_ctx[None, :], jnp.zeros((8 - n_batch - 1, d), F32)], axis=0)
    mods = _ada(cvecs, w_ada, b_ada)
    cos_t, sin_t = _rope_tables(n_batch, seq, ctx_len)
    n_lat = n_batch * seq
    xt = jnp.concatenate([x.reshape(n_lat, d), ctx.reshape(n_batch * ctx_len, d)], axis=0)
    for layer in range(depth):
        p = {name: val[layer] for name, val in stacked.items()}
        xt = _layer(layer, xt, mods[layer], cos_t, sin_t, p, expert_params, n_batch=n_batch, seq=seq,
                    ctx_len=ctx_len, ctx_out=layer < depth - 1)
    return xt[:n_lat].reshape(n_batch, seq, d)
```

```python
import functools
import math

import jax
import jax.numpy as jnp
from jax import lax
from jax.experimental import pallas as pl
from jax.experimental.pallas import tpu as pltpu

F32 = jnp.float32
BF16 = jnp.bfloat16

EPS = 1e-6
GRID_W = 64
POOL_WINDOWS = (2, 4, 8, 16)
POOL_HALO = 8
SSM_GROUP = 16
SSM_STATE = 64
S5_CHUNK = 32
QK_NOPE = 128
QK_ROPE = 64
QK_HEAD = QK_NOPE + QK_ROPE
QK_PAD = 256
V_HEAD = 128
ROPE_BASE = 10000.0
TOP_K = 4
SWIGLU_ALPHA = 1.702
SWIGLU_LIMIT = 7.0

LANES = 128
ROW_TILE = 256
ATTN_Q_TILE = 512
ATTN_KEY_CHUNK = 1024
EXPERT_ITEM_ROWS = 1280
EXPERT_SUB_ROWS = 256
EXPERT_TILE = 512
COMBINE_TILE = 128
VMEM_LIMIT = 52 * 1024 * 1024
EXPERT_VMEM_LIMIT = 57 * 1024 * 1024
NEG_BIG = -1e30


def _cparams(sem, vmem=VMEM_LIMIT):
    return pltpu.CompilerParams(dimension_semantics=sem, vmem_limit_bytes=vmem)


def _rms(x, w):
    return x * lax.rsqrt(jnp.mean(x * x, axis=-1, keepdims=True) + EPS) * w


def _full(shape):
    nd = len(shape)
    return pl.BlockSpec(shape, lambda *_: (0,) * nd)


def _ada_kernel(cv_ref, w_ref, b_ref, o_ref):
    cv = cv_ref[...]
    s = (cv * jax.nn.sigmoid(cv)).astype(BF16)
    o_ref[0] = jnp.dot(s, w_ref[0].astype(BF16), preferred_element_type=F32) + b_ref[0]


def _ada(cvecs, w_ada, b_ada):
    n_layers, d, n6 = w_ada.shape
    tn = 512
    return pl.pallas_call(
        _ada_kernel,
        grid=(n_layers, n6 // tn),
        in_specs=[
            pl.BlockSpec((8, d), lambda l, j: (0, 0)),
            pl.BlockSpec((1, d, tn), lambda l, j: (l, 0, j)),
            pl.BlockSpec((1, 1, tn), lambda l, j: (l, 0, j)),
        ],
        out_specs=pl.BlockSpec((1, 8, tn), lambda l, j: (l, 0, j)),
        out_shape=jax.ShapeDtypeStruct((n_layers, 8, n6), F32),
        compiler_params=_cparams(("arbitrary", "arbitrary")),
        name="ada",
    )(cvecs, w_ada, b_ada.reshape(n_layers, 1, n6))


def _rope128(r, cos, sin):
    lane = lax.broadcasted_iota(jnp.int32, r.shape, 1)
    sw = jnp.where((lane % QK_ROPE) < QK_ROPE // 2, pltpu.roll(r, LANES - 32, 1), pltpu.roll(r, 32, 1))
    return r * cos + sw * sin


def _proj_kernel(x_ref, mod_ref, cos_ref, sin_ref, nmix_ref, win_ref, qan_ref, wuq_ref, kvan_ref,
                 wk_ref, wv_ref, qn_ref, kn_ref,
                 upool_ref, ussm_ref, q_ref, k_ref, v_ref, *, tiles_per_seq, n_batch, n_heads, d_model):
    d = d_model
    r = jnp.minimum(pl.program_id(0) // tiles_per_seq, n_batch)
    sh = mod_ref[pl.ds(r, 1), pl.ds(0, d)]
    sc = mod_ref[pl.ds(r, 1), pl.ds(d, d)]
    h = _rms(x_ref[...], nmix_ref[...]) * (1.0 + sc) + sh
    proj = jnp.dot(h.astype(BF16), win_ref[...], preferred_element_type=F32)
    d4 = d // 4
    upool_ref[...] = proj[:, 0:d4]
    ussm_ref[...] = proj[:, 2 * d4:3 * d4]
    cos = cos_ref[...]
    sin = sin_ref[...]

    qa = _rms(proj[:, d4:2 * d4], qan_ref[...]).astype(BF16)
    qf = jnp.dot(qa, wuq_ref[...], preferred_element_type=F32)
    qn_w = qn_ref[...]
    for hd in range(n_heads):
        blk = qf[:, hd * QK_PAD:(hd + 1) * QK_PAD]
        ss = jnp.sum(blk * blk, axis=-1, keepdims=True) * (1.0 / QK_HEAD)
        qn = blk * lax.rsqrt(ss + EPS) * qn_w
        q_ref[:, hd * QK_PAD:hd * QK_PAD + LANES] = qn[:, :LANES].astype(BF16)
        q_ref[:, hd * QK_PAD + LANES:(hd + 1) * QK_PAD] = _rope128(qn[:, LANES:], cos, sin).astype(BF16)

    kv_lo = 3 * d4
    kv_w = d // 8
    ka = _rms(proj[:, kv_lo:kv_lo + kv_w], kvan_ref[...]).astype(BF16)
    kn_all = jnp.dot(ka, wk_ref[...], preferred_element_type=F32)
    v_ref[...] = jnp.dot(ka, wv_ref[...], preferred_element_type=F32).astype(BF16)
    krp = proj[:, kv_lo + kv_w:kv_lo + kv_w + LANES]
    kr_ss = jnp.sum(krp * krp, axis=-1, keepdims=True)
    kn_w = kn_ref[...]
    for hd in range(n_heads):
        kn = kn_all[:, hd * QK_NOPE:(hd + 1) * QK_NOPE]
        ss = (jnp.sum(kn * kn, axis=-1, keepdims=True) + kr_ss) * (1.0 / QK_HEAD)
        rinv = lax.rsqrt(ss + EPS)
        k_ref[:, hd * QK_PAD:hd * QK_PAD + LANES] = (kn * rinv * kn_w[:, :LANES]).astype(BF16)
        k_ref[:, hd * QK_PAD + LANES:(hd + 1) * QK_PAD] = _rope128(
            krp * rinv * kn_w[:, LANES:], cos, sin).astype(BF16)


def _proj(xt, mods, cos_t, sin_t, nmix, win, qan, wuq, kvan, wk, wv, qn, kn, *, n_batch, seq):
    t, d = xt.shape
    tm = ROW_TILE
    n_heads = wv.shape[1] // V_HEAD
    row = lambda w: pl.BlockSpec((tm, w), lambda i: (i, 0))
    kern = functools.partial(_proj_kernel, tiles_per_seq=seq // tm, n_batch=n_batch,
                             n_heads=n_heads, d_model=d)
    return pl.pallas_call(
        kern,
        grid=(t // tm,),
        in_specs=[row(d), _full(mods.shape), row(LANES), row(LANES), _full(nmix.shape), _full(win.shape),
                  _full(qan.shape), _full(wuq.shape), _full(kvan.shape), _full(wk.shape), _full(wv.shape),
                  _full(qn.shape), _full(kn.shape)],
        out_specs=[row(d // 4), row(d // 4), row(n_heads * QK_PAD), row(n_heads * QK_PAD),
                   row(n_heads * V_HEAD)],
        out_shape=[jax.ShapeDtypeStruct((t, d // 4), F32), jax.ShapeDtypeStruct((t, d // 4), F32),
                   jax.ShapeDtypeStruct((t, n_heads * QK_PAD), BF16),
                   jax.ShapeDtypeStruct((t, n_heads * QK_PAD), BF16),
                   jax.ShapeDtypeStruct((t, n_heads * V_HEAD), BF16)],
        compiler_params=_cparams(("arbitrary",)),
        name="mixer_in",
    )(xt, mods, cos_t, sin_t, nmix, win, qan, wuq, kvan, wk, wv, qn, kn)


def _pool_kernel(prev_ref, cur_ref, next_ref, w_ref, scale_ref, o_ref, pad_ref, *,
                 tm, n_lat_tiles, seq, ctx_len):
    i = pl.program_id(0)
    hw = POOL_HALO
    pad_ref[0:hw, :] = prev_ref[...]
    pad_ref[hw:hw + tm, :] = cur_ref[...]
    pad_ref[hw + tm:2 * hw + tm, :] = next_ref[...]
    is_lat = i < n_lat_tiles
    row0 = i * tm
    s0 = jnp.where(is_lat, (row0 // seq) * seq,
                   n_lat_tiles * tm + ((row0 - n_lat_tiles * tm) // ctx_len) * ctx_len)
    slen = jnp.where(is_lat, seq, ctx_len)
    t = row0 - s0 + lax.broadcasted_iota(jnp.int32, (tm, 1), 0)
    for g, w in enumerate(POOL_WINDOWS):
        lanes = slice(g * LANES, (g + 1) * LANES)
        acc = jnp.zeros((tm, LANES), F32)
        for kk in range(-(w // 2), w - w // 2):
            valid = (t + kk >= 0) & (t + kk < slen)
            acc = acc + jnp.where(valid, pad_ref[hw + kk:hw + kk + tm, lanes], 0.0)
        lo = jnp.maximum(t - w // 2, 0)
        hi = jnp.minimum(t + (w - w // 2), slen)
        mean = acc / (hi - lo).astype(F32)
        dlt = (mean - cur_ref[:, lanes]).astype(BF16)
        o_ref[:, lanes] = jnp.dot(dlt, w_ref[g], preferred_element_type=F32) * scale_ref[:, lanes]


def _pool(u_pool, w_pool, pool_scale, *, n_batch, seq, ctx_len):
    t, dp = u_pool.shape
    tm = ROW_TILE
    hb = tm // POOL_HALO
    n_hblocks = t // POOL_HALO
    kern = functools.partial(_pool_kernel, tm=tm, n_lat_tiles=n_batch * seq // tm, seq=seq, ctx_len=ctx_len)
    return pl.pallas_call(
        kern,
        grid=(t // tm,),
        in_specs=[
            pl.BlockSpec((POOL_HALO, dp), lambda i: (jnp.maximum(i * hb - 1, 0), 0)),
            pl.BlockSpec((tm, dp), lambda i: (i, 0)),
            pl.BlockSpec((POOL_HALO, dp), lambda i: (jnp.minimum((i + 1) * hb, n_hblocks - 1), 0)),
            _full(w_pool.shape), _full(pool_scale.shape),
        ],
        out_specs=pl.BlockSpec((tm, dp), lambda i: (i, 0)),
        out_shape=jax.ShapeDtypeStruct((t, dp), F32),
        scratch_shapes=[pltpu.VMEM((tm + 2 * POOL_HALO, dp), F32)],
        compiler_params=_cparams(("arbitrary",)),
        name="pool",
    )(u_pool, u_pool, u_pool, w_pool, pool_scale)


def _s5_prep_kernel(are_ref, aim_ref, ls_ref, btr_ref, bti_ref, ctr_ref, cti_ref, crr_ref, cri_ref,
                    mt_ref, bcr_ref, bci_ref, g1r_ref, g1i_ref, lr_ref, li_ref):
    nrow = S5_CHUNK * SSM_GROUP
    mcol = lax.broadcasted_iota(jnp.int32, (S5_CHUNK, 1), 0).astype(F32)
    row_j = lax.broadcasted_iota(jnp.int32, (nrow, S5_CHUNK), 0) // SSM_GROUP
    col_m = lax.broadcasted_iota(jnp.int32, (nrow, S5_CHUNK), 1)
    rep_up = (row_j == col_m).astype(F32)
    rep_dn = (row_j == S5_CHUNK - 1 - col_m).astype(F32)
    lane_blk = lax.broadcasted_iota(jnp.int32, (nrow, LANES), 1) // SSM_GROUP
    hi = lax.Precision.HIGHEST
    for d in range(2):
        step = jnp.exp(ls_ref[d, 0])
        ar = are_ref[d, 0]
        ai = aim_ref[d, 0]
        la = ar * step
        th = ai * step
        mag = jnp.exp(la)
        lam_r = mag * jnp.cos(th)
        lam_i = mag * jnp.sin(th)
        den = ar * ar + ai * ai
        nr = lam_r - 1.0
        coef_r = (nr * ar + lam_i * ai) / den
        coef_i = (lam_i * ar - nr * ai) / den

        def powtab(m):
            mg = jnp.exp(m * la)
            an = m * th
            return mg * jnp.cos(an), mg * jnp.sin(an)

        small_r, small_i = powtab(mcol)
        up_r = jnp.dot(rep_up, small_r, precision=hi, preferred_element_type=F32)
        up_i = jnp.dot(rep_up, small_i, precision=hi, preferred_element_type=F32)
        dn_r = jnp.dot(rep_dn, small_r, precision=hi, preferred_element_type=F32)
        dn_i = jnp.dot(rep_dn, small_i, precision=hi, preferred_element_type=F32)
        if d == 0:
            (mr, mi), (orr, oi) = (dn_r, dn_i), (up_r, up_i)
        else:
            (mr, mi), (orr, oi) = (up_r, up_i), (dn_r, dn_i)
        btr = btr_ref[d, 0]
        bti = bti_ref[d, 0]
        bpr = coef_r * btr - coef_i * bti
        bpi = coef_r * bti + coef_i * btr
        bmr = bpr * mr - bpi * mi
        bmi = bpr * mi + bpi * mr
        bcr_ref[d, 0] = bmr.astype(BF16)
        bci_ref[d, 0] = bmi.astype(BF16)
        krep = (jnp.dot(bmr.astype(BF16), ctr_ref[d, 0].astype(BF16), preferred_element_type=F32)
                - jnp.dot(bmi.astype(BF16), cti_ref[d, 0].astype(BF16), preferred_element_type=F32))
        o_r = orr * lam_r - oi * lam_i
        o_i = orr * lam_i + oi * lam_r
        crr = crr_ref[d, 0]
        cri = cri_ref[d, 0]
        g1r_ref[d, 0] = (crr * o_r - cri * o_i).astype(BF16)
        g1i_ref[d, 0] = (-(crr * o_i + cri * o_r)).astype(BF16)
        lr_ref[d, 0], li_ref[d, 0] = powtab(float(S5_CHUNK))
        blocks_per_tile = LANES // SSM_GROUP
        for qt in range(nrow // LANES):
            src = krep[:, qt * LANES:(qt + 1) * LANES]
            acc = jnp.zeros((nrow, LANES), F32)
            for jj in range(blocks_per_tile):
                j = qt * blocks_per_tile + jj
                if d == 0:
                    s = SSM_GROUP * (S5_CHUNK - 1 - j)
                    sh = src if s == 0 else jnp.concatenate([src[s:], jnp.zeros((s, LANES), F32)], axis=0)
                else:
                    s = SSM_GROUP * j
                    sh = src if s == 0 else jnp.concatenate([jnp.zeros((s, LANES), F32), src[:nrow - s]], axis=0)
                acc = jnp.where(lane_blk == jj, sh, acc)
            mt_ref[d, 0, :, qt * LANES:(qt + 1) * LANES] = acc.astype(BF16)


def _s5_prep(a_re, a_im, log_step, b_re, b_im, c_re, c_im):
    _, g, p = a_re.shape
    nrow = S5_CHUNK * SSM_GROUP
    bt = lambda b: jnp.tile(jnp.swapaxes(b, 2, 3), (1, 1, S5_CHUNK, 1))
    ct = lambda c: jnp.tile(jnp.swapaxes(c, 2, 3), (1, 1, 1, S5_CHUNK))
    cr = lambda c: jnp.tile(c, (1, 1, S5_CHUNK, 1))
    blk = lambda *s: pl.BlockSpec((2, 1) + s, lambda i: (0, i, 0, 0))
    out_shapes = [jax.ShapeDtypeStruct((2, g, nrow, nrow), BF16)] + \
        [jax.ShapeDtypeStruct((2, g, nrow, p), BF16)] * 4 + [jax.ShapeDtypeStruct((2, g, 1, p), F32)] * 2
    return pl.pallas_call(
        _s5_prep_kernel,
        grid=(g,),
        in_specs=[blk(1, p), blk(1, p), blk(1, 1), blk(nrow, p), blk(nrow, p), blk(p, nrow), blk(p, nrow),
                  blk(nrow, p), blk(nrow, p)],
        out_specs=[blk(nrow, nrow)] + [blk(nrow, p)] * 4 + [blk(1, p)] * 2,
        out_shape=out_shapes,
        compiler_params=_cparams(("arbitrary",)),
        name="s5_prep",
    )(a_re.reshape(2, g, 1, p), a_im.reshape(2, g, 1, p), log_step.reshape(2, g, 1, 1),
      bt(b_re), bt(b_im), ct(c_re), ct(c_im), cr(c_re), cr(c_im))


def _s5_kernel(u_ref, mt_ref, bcr_ref, bci_ref, g1r_ref, g1i_ref, lr_ref, li_ref, y_ref,
               er_ref, ei_ref, sr_ref, si_ref, *, n_batch, n_lat_ch, n_ctx_ch):
    ub = u_ref[0].astype(BF16)
    ctx0 = n_batch * n_lat_ch
    total = None
    for d in range(2):
        rev = d == 1
        er_ref[...] = jnp.dot(ub, bcr_ref[d, 0], preferred_element_type=F32)
        ei_ref[...] = jnp.dot(ub, bci_ref[d, 0], preferred_element_type=F32)
        lam_r = lr_ref[d, 0]
        lam_i = li_ref[d, 0]

        def segment(base, stride, n, carry):
            def body(s, c):
                pos = (n - 1 - s) if rev else s
                new = []
                for b in range(n_batch):
                    row = base + b * stride + pos
                    s_r, s_i = c[2 * b], c[2 * b + 1]
                    sr_ref[pl.ds(row, 1), :] = s_r
                    si_ref[pl.ds(row, 1), :] = s_i
                    new.append(lam_r * s_r - lam_i * s_i + er_ref[pl.ds(row, 1), :])
                    new.append(lam_r * s_i + lam_i * s_r + ei_ref[pl.ds(row, 1), :])
                return tuple(new)
            return lax.fori_loop(0, n, body, carry)

        zero = jnp.zeros((1, SSM_STATE), F32)
        carry = segment(ctx0, n_ctx_ch, n_ctx_ch, (zero,) * (2 * n_batch))
        segment(0, n_lat_ch, n_lat_ch, carry)
        dn = (((1,), (1,)), ((), ()))
        y = (jnp.dot(ub, mt_ref[d, 0], preferred_element_type=F32)
             + lax.dot_general(sr_ref[...].astype(BF16), g1r_ref[d, 0], dn, preferred_element_type=F32)
             + lax.dot_general(si_ref[...].astype(BF16), g1i_ref[d, 0], dn, preferred_element_type=F32))
        total = y if total is None else total + y
    y_ref[0] = total


def _s5(u_g, prep, *, n_batch, seq, ctx_len):
    g, nch, w = u_g.shape
    mt, bcr, bci, g1r, g1i, lr, li = prep
    p = SSM_STATE
    blk = lambda *s: pl.BlockSpec((2, 1) + s, lambda i: (0, i, 0, 0))
    kern = functools.partial(_s5_kernel, n_batch=n_batch, n_lat_ch=seq // S5_CHUNK, n_ctx_ch=ctx_len // S5_CHUNK)
    return pl.pallas_call(
        kern,
        grid=(g,),
        in_specs=[pl.BlockSpec((1, nch, w), lambda i: (i, 0, 0)), blk(w, w), blk(w, p), blk(w, p), blk(w, p),
                  blk(w, p), blk(1, p), blk(1, p)],
        out_specs=pl.BlockSpec((1, nch, w), lambda i: (i, 0, 0)),
        out_shape=jax.ShapeDtypeStruct((g, nch, w), F32),
        scratch_shapes=[pltpu.VMEM((nch, p), F32)] * 4,
        compiler_params=_cparams(("arbitrary",)),
        name="s5_scan",
    )(u_g, mt, bcr, bci, g1r, g1i, lr, li)


_NT = (((1,), (1,)), ((), ()))


def _attn_lat_kernel(q_ref, kl_ref, vl_ref, kc_ref, vc_ref, o_ref):
    q = q_ref[...]
    tq = q.shape[0]
    seq = kl_ref.shape[0]
    kc = min(ATTN_KEY_CHUNK, seq)
    chunks = [(kl_ref, vl_ref, c * kc, kc) for c in range(seq // kc)] + [(kc_ref, vc_ref, 0, kc_ref.shape[0])]
    m = jnp.full((tq, 1), NEG_BIG, F32)
    l = jnp.zeros((tq, 1), F32)
    acc = jnp.zeros((tq, V_HEAD), F32)
    for k_ref, v_ref, off, n in chunks:
        s = lax.dot_general(q, k_ref[off:off + n, :], _NT, preferred_element_type=F32)
        m_new = jnp.maximum(m, jnp.max(s, axis=-1, keepdims=True))
        alpha = jnp.exp(m - m_new)
        p = jnp.exp(s - m_new)
        l = alpha * l + jnp.sum(p, axis=-1, keepdims=True)
        acc = alpha * acc + jnp.dot(p.astype(BF16), v_ref[off:off + n, :], preferred_element_type=F32)
        m = m_new
    o_ref[...] = acc / l


def _attn_ctx_kernel(q_ref, kc_ref, vc_ref, o_ref):
    s = lax.dot_general(q_ref[...], kc_ref[...], _NT, preferred_element_type=F32)
    p = jnp.exp(s - jnp.max(s, axis=-1, keepdims=True))
    l = jnp.sum(p, axis=-1, keepdims=True)
    o_ref[...] = jnp.dot(p.astype(BF16), vc_ref[...], preferred_element_type=F32) / l


def _attention(q, k, v, *, n_batch, seq, ctx_len, ctx_queries):
    n_heads = v.shape[1] // V_HEAD
    tq = ATTN_Q_TILE
    nq = seq // tq
    cblk = n_batch * seq // ctx_len
    y_lat = pl.pallas_call(
        _attn_lat_kernel,
        grid=(n_batch, n_heads, nq),
        in_specs=[
            pl.BlockSpec((tq, QK_PAD), lambda b, h, i: (b * nq + i, h)),
            pl.BlockSpec((seq, QK_PAD), lambda b, h, i: (b, h)),
            pl.BlockSpec((seq, V_HEAD), lambda b, h, i: (b, h)),
            pl.BlockSpec((ctx_len, QK_PAD), lambda b, h, i: (cblk + b, h)),
            pl.BlockSpec((ctx_len, V_HEAD), lambda b, h, i: (cblk + b, h)),
        ],
        out_specs=pl.BlockSpec((tq, V_HEAD), lambda b, h, i: (b * nq + i, h)),
        out_shape=jax.ShapeDtypeStruct((n_batch * seq, n_heads * V_HEAD), F32),
        compiler_params=_cparams(("arbitrary", "arbitrary", "arbitrary")),
        name="attn_latent",
    )(q, k, v, k, v)
    if not ctx_queries:
        return y_lat, None
    y_ctx = pl.pallas_call(
        _attn_ctx_kernel,
        grid=(n_batch, n_heads),
        in_specs=[
            pl.BlockSpec((ctx_len, QK_PAD), lambda b, h: (cblk + b, h)),
            pl.BlockSpec((ctx_len, QK_PAD), lambda b, h: (cblk + b, h)),
            pl.BlockSpec((ctx_len, V_HEAD), lambda b, h: (cblk + b, h)),
        ],
        out_specs=pl.BlockSpec((ctx_len, V_HEAD), lambda b, h: (b, h)),
        out_shape=jax.ShapeDtypeStruct((n_batch * ctx_len, n_heads * V_HEAD), F32),
        compiler_params=_cparams(("arbitrary", "arbitrary")),
        name="attn_context",
    )(q, k, v)
    return y_lat, y_ctx


def _merge_kernel(x_ref, yp_ref, ys_ref, us_ref, yml_ref, *rest, tiles_per_seq, n_batch, d_model, has_ctx):
    if has_ctx:
        ymc_ref, rest = rest[0], rest[1:]
    (mod_ref, sd_ref, wglu_ref, bglu_ref, on_ref, wout_ref, nffn_ref, wr_ref, br_ref,
     x1_ref, h2_ref, idx_ref, gate_ref, rank_ref, cnt_ref, carry_ref) = rest
    d = d_model
    d4 = d // 4
    i = pl.program_id(0)
    tm = x_ref.shape[0]
    r = jnp.minimum(i // tiles_per_seq, n_batch)
    g1 = mod_ref[pl.ds(r, 1), pl.ds(2 * d, d)]
    sh2 = mod_ref[pl.ds(r, 1), pl.ds(3 * d, d)]
    sc2 = mod_ref[pl.ds(r, 1), pl.ds(4 * d, d)]

    z = ys_ref[...] + sd_ref[...] * us_ref[...]
    g = jax.nn.gelu(z, approximate=True)
    glu = jnp.dot(g.astype(BF16), wglu_ref[...], preferred_element_type=F32) + bglu_ref[...]
    y_ssm = g * jax.nn.sigmoid(glu)
    on = on_ref[...]
    y_mla = yml_ref[...]
    if has_ctx:
        y_mla = jnp.where(i < n_batch * tiles_per_seq, y_mla, ymc_ref[...])
    cat = jnp.concatenate([_rms(yp_ref[...], on[:, 0:d4]), _rms(y_ssm, on[:, d4:2 * d4]),
                           _rms(y_mla, on[:, 2 * d4:])], axis=-1).astype(BF16)
    x1 = x_ref[...] + g1 * jnp.dot(cat, wout_ref[...], preferred_element_type=F32)
    x1_ref[...] = x1
    h2 = (_rms(x1, nffn_ref[...]) * (1.0 + sc2) + sh2).astype(BF16)
    h2_ref[...] = h2

    logits = jnp.dot(h2, wr_ref[...], preferred_element_type=F32) + br_ref[...]
    lane = lax.broadcasted_iota(jnp.int32, (tm, LANES), 1)
    vals, hots = [], []
    idx_out = jnp.zeros((tm, LANES), jnp.int32)
    for kk in range(TOP_K):
        mx = jnp.max(logits, axis=-1, keepdims=True)
        sel = jnp.min(jnp.where(logits == mx, lane, LANES), axis=-1, keepdims=True)
        hot = lane == sel
        vals.append(mx)
        hots.append(hot)
        idx_out = jnp.where(lane == kk, sel, idx_out)
        logits = jnp.where(hot, NEG_BIG * 2.0, logits)
    es = [jnp.exp(vv - vals[0]) for vv in vals]
    den = es[0] + es[1] + es[2] + es[3]
    gate_out = jnp.zeros((tm, LANES), F32)
    for kk in range(TOP_K):
        gate_out = jnp.where(lane == kk, es[kk] / den, gate_out)
    idx_ref[...] = idx_out
    gate_ref[...] = gate_out

    @pl.when(i == 0)
    def _():
        carry_ref[...] = jnp.zeros_like(carry_ref)

    onehot = (hots[0] | hots[1] | hots[2] | hots[3]).astype(BF16)
    rr = lax.broadcasted_iota(jnp.int32, (tm, tm), 0)
    cc = lax.broadcasted_iota(jnp.int32, (tm, tm), 1)
    tri = (cc < rr).astype(BF16)
    pos = jnp.dot(tri, onehot, preferred_element_type=F32) + carry_ref[...]
    rank_out = jnp.zeros((tm, LANES), jnp.int32)
    for kk in range(TOP_K):
        rk = jnp.sum(jnp.where(hots[kk], pos, 0.0), axis=-1, keepdims=True).astype(jnp.int32)
        rank_out = jnp.where(lane == kk, rk, rank_out)
    rank_ref[...] = rank_out
    carry = carry_ref[...] + jnp.sum(onehot.astype(F32), axis=0, keepdims=True)
    carry_ref[...] = carry
    cnt_ref[...] = carry.astype(jnp.int32)


def _merge(xt, y_pool, y_s5, u_ssm, y_mla_lat, y_mla_ctx, mods, ssm_d, wglu, bglu, out_norm, wout, nffn, wr, br, *,
           n_rows, n_batch, seq):
    d = xt.shape[1]
    tm = ROW_TILE
    n_lat_tiles = n_batch * seq // tm
    has_ctx = y_mla_ctx is not None
    row = lambda w: pl.BlockSpec((tm, w), lambda i: (i, 0))
    kern = functools.partial(_merge_kernel, tiles_per_seq=seq // tm, n_batch=n_batch, d_model=d, has_ctx=has_ctx)
    small = [mods, ssm_d, wglu, bglu, out_norm, wout, nffn, wr, br]
    y_mla = [y_mla_lat] + ([y_mla_ctx] if has_ctx else [])
    y_specs = [pl.BlockSpec((tm, d // 2), lambda i: (jnp.minimum(i, n_lat_tiles - 1), 0))]
    if has_ctx:
        y_specs.append(pl.BlockSpec((tm, d // 2), lambda i: (jnp.maximum(i - n_lat_tiles, 0), 0)))
    return pl.pallas_call(
        kern,
        grid=(n_rows // tm,),
        in_specs=[row(d), row(d // 4), row(d // 4), row(d // 4)] + y_specs + [_full(a.shape) for a in small],
        out_specs=[row(d), row(d), row(LANES), row(LANES), row(LANES), _full((1, LANES))],
        out_shape=[jax.ShapeDtypeStruct((n_rows, d), F32), jax.ShapeDtypeStruct((n_rows, d), BF16),
                   jax.ShapeDtypeStruct((n_rows, LANES), jnp.int32), jax.ShapeDtypeStruct((n_rows, LANES), F32),
                   jax.ShapeDtypeStruct((n_rows, LANES), jnp.int32), jax.ShapeDtypeStruct((1, LANES), jnp.int32)],
        scratch_shapes=[pltpu.VMEM((1, LANES), F32)],
        compiler_params=_cparams(("arbitrary",)),
        name="merge_router",
    )(xt, y_pool, y_s5, u_ssm, *y_mla, *small)


def _row_copy(src_hbm, dst_hbm, sem, src_row, dst_row):
    return pltpu.make_async_copy(src_hbm.at[src_row], dst_hbm.at[dst_row], sem)


def _dispatch_kernel(dest_hbm, h2_ref, zeros_hbm, xs_hbm, dest_smem, sem_idx, sem_rows, *, tm):
    del zeros_hbm
    i = pl.program_id(0)
    n = tm * TOP_K
    idx_copy = pltpu.make_async_copy(dest_hbm.at[i], dest_smem, sem_idx)
    idx_copy.start()
    idx_copy.wait()

    def issue(j, _):
        _row_copy(h2_ref, xs_hbm, sem_rows, j // TOP_K, dest_smem[j]).start()
        return 0

    lax.fori_loop(0, n, issue, 0, unroll=8)

    def drain(j, _):
        _row_copy(h2_ref, xs_hbm, sem_rows, j // TOP_K, dest_smem[j]).wait()
        return 0

    lax.fori_loop(0, n, drain, 0, unroll=8)


def _dispatch(dest_tiles, h2_3d, n_sorted_rows):
    n_tok, sl, ln = h2_3d.shape
    tm = ROW_TILE
    zeros = jnp.zeros((n_sorted_rows,) + h2_3d.shape[1:], h2_3d.dtype)
    return pl.pallas_call(
        functools.partial(_dispatch_kernel, tm=tm),
        grid=(n_tok // tm,),
        in_specs=[pl.BlockSpec(memory_space=pl.ANY), pl.BlockSpec((tm, sl, ln), lambda i: (i, 0, 0)),
                  pl.BlockSpec(memory_space=pl.ANY)],
        out_specs=pl.BlockSpec(memory_space=pl.ANY),
        out_shape=jax.ShapeDtypeStruct(zeros.shape, zeros.dtype),
        input_output_aliases={2: 0},
        scratch_shapes=[pltpu.SMEM((tm * TOP_K,), jnp.int32), pltpu.SemaphoreType.DMA, pltpu.SemaphoreType.DMA],
        compiler_params=_cparams(("arbitrary",)),
        name="moe_dispatch",
    )(dest_tiles, h2_3d, zeros)


def _expert_kernel(ie_ref, ib_ref, ns_ref, x_ref, wg_ref, bg_ref, wu_ref, bu_ref, wd_ref, bd_ref, o_ref, act_ref,
                   *, nf):
    it = pl.program_id(0)
    j = pl.program_id(1)
    ns = ns_ref[it]
    rows_max = x_ref.shape[0]
    for m in range(1, rows_max // EXPERT_SUB_ROWS + 1):
        mr = m * EXPERT_SUB_ROWS

        @pl.when((ns == m) & (j < nf))
        def _():
            x = x_ref[0:mr, :]
            g = jnp.dot(x, wg_ref[0, 0].astype(BF16), preferred_element_type=F32) + bg_ref[0, 0]
            u = jnp.dot(x, wu_ref[0, 0].astype(BF16), preferred_element_type=F32) + bu_ref[0, 0]
            g = jnp.minimum(g, SWIGLU_LIMIT)
            u = jnp.clip(u, -SWIGLU_LIMIT, SWIGLU_LIMIT)
            act_ref[j, 0:mr, :] = (g * jax.nn.sigmoid(SWIGLU_ALPHA * g) * (u + 1.0)).astype(BF16)

        @pl.when((ns == m) & (j >= nf))
        def _():
            act = jnp.concatenate([act_ref[f, 0:mr, :] for f in range(nf)], axis=-1)
            o_ref[0:mr, :] = jnp.dot(act, wd_ref[0, 0].astype(BF16), preferred_element_type=F32) + bd_ref[0, 0]
            if mr < rows_max:
                o_ref[mr:, :] = jnp.zeros((rows_max - mr, o_ref.shape[1]), F32)

    @pl.when((ns == 0) & (j >= nf))
    def _():
        o_ref[...] = jnp.zeros(o_ref.shape, F32)


def _experts(layer, item_e, item_blk, item_nsub, xs, w_gate, b_gate, w_up, b_up, w_down, b_down):
    n_items = item_e.shape[0]
    n_layers, n_exp, d, f_dim = w_gate.shape
    tf = EXPERT_TILE
    nf = f_dim // tf
    rows = EXPERT_ITEM_ROWS
    assert d // tf == nf

    def fi(j, ns, it):
        return jnp.where(ns[it] > 0, jnp.minimum(j, nf - 1), nf - 1)

    def ni(j, ns, it):
        return jnp.where(ns[it] > 0, jnp.maximum(j - nf, 0), nf - 1)

    return pl.pallas_call(
        functools.partial(_expert_kernel, nf=nf),
        grid_spec=pltpu.PrefetchScalarGridSpec(
            num_scalar_prefetch=3,
            grid=(n_items, 2 * nf),
            in_specs=[
                pl.BlockSpec((rows, d), lambda it, j, ie, ib, ns: (ib[it], 0)),
                pl.BlockSpec((1, 1, d, tf), lambda it, j, ie, ib, ns: (layer, ie[it], 0, fi(j, ns, it))),
                pl.BlockSpec((1, 1, 1, tf), lambda it, j, ie, ib, ns: (layer, ie[it], 0, fi(j, ns, it))),
                pl.BlockSpec((1, 1, d, tf), lambda it, j, ie, ib, ns: (layer, ie[it], 0, fi(j, ns, it))),
                pl.BlockSpec((1, 1, 1, tf), lambda it, j, ie, ib, ns: (layer, ie[it], 0, fi(j, ns, it))),
                pl.BlockSpec((1, 1, f_dim, tf), lambda it, j, ie, ib, ns: (layer, ie[it], 0, ni(j, ns, it))),
                pl.BlockSpec((1, 1, 1, tf), lambda it, j, ie, ib, ns: (layer, ie[it], 0, ni(j, ns, it))),
            ],
            out_specs=pl.BlockSpec((rows, tf), lambda it, j, ie, ib, ns: (it, jnp.maximum(j - nf, 0))),
            scratch_shapes=[pltpu.VMEM((nf, rows, tf), BF16)],
        ),
        out_shape=jax.ShapeDtypeStruct((n_items * rows, d), F32),
        compiler_params=_cparams(("arbitrary", "arbitrary"), EXPERT_VMEM_LIMIT),
        name="moe_experts",
    )(item_e, item_blk, item_nsub, xs, w_gate, b_gate.reshape(n_layers, n_exp, 1, f_dim), w_up,
      b_up.reshape(n_layers, n_exp, 1, f_dim), w_down, b_down.reshape(n_layers, n_exp, 1, d))


def _combine_kernel(dest_hbm, gate_hbm, os_hbm, x_ref, g2_ref, o_ref, dest_smem, gate_smem, buf, sem_idx, sem_rows,
                    *, tc, n_tiles, tiles_per_seq, n_batch):
    i = pl.program_id(0)
    n = tc * TOP_K
    slot = i % 2

    def fetch(tile, s):
        idx_copy = pltpu.make_async_copy(dest_hbm.at[tile], dest_smem.at[s], sem_idx)
        idx_copy.start()
        idx_copy.wait()

        def issue(j, _):
            _row_copy(os_hbm, buf.at[s], sem_rows.at[s], dest_smem[s, j], j).start()
            return 0

        lax.fori_loop(0, n, issue, 0, unroll=8)

    @pl.when(i == 0)
    def _():
        fetch(0, 0)

    @pl.when(i + 1 < n_tiles)
    def _():
        fetch(i + 1, 1 - slot)

    gate_copy = pltpu.make_async_copy(gate_hbm.at[i], gate_smem, sem_idx)
    gate_copy.start()
    gate_copy.wait()

    def drain(j, _):
        _row_copy(os_hbm, buf.at[slot], sem_rows.at[slot], dest_smem[slot, j], j).wait()
        return 0

    lax.fori_loop(0, n, drain, 0, unroll=8)
    g2 = g2_ref[jnp.minimum(i // tiles_per_seq, n_batch)]
    cur = buf.at[slot]

    def row(rw, _):
        acc = gate_smem[rw * TOP_K] * cur[rw * TOP_K]
        for kk in range(1, TOP_K):
            acc = acc + gate_smem[rw * TOP_K + kk] * cur[rw * TOP_K + kk]
        o_ref[rw] = x_ref[rw] + g2 * acc
        return 0

    lax.fori_loop(0, tc, row, 0, unroll=2)


def _combine(dest_tiles, gate_tiles, out_sorted_3d, x1_3d, g2_3d, *, n_batch, seq):
    n_tok, sl, ln = x1_3d.shape
    tc = COMBINE_TILE
    kern = functools.partial(_combine_kernel, tc=tc, n_tiles=n_tok // tc, tiles_per_seq=seq // tc, n_batch=n_batch)
    return pl.pallas_call(
        kern,
        grid=(n_tok // tc,),
        in_specs=[pl.BlockSpec(memory_space=pl.ANY)] * 3 + [
            pl.BlockSpec((tc, sl, ln), lambda i: (i, 0, 0)), _full(g2_3d.shape)],
        out_specs=pl.BlockSpec((tc, sl, ln), lambda i: (i, 0, 0)),
        out_shape=jax.ShapeDtypeStruct(x1_3d.shape, F32),
        scratch_shapes=[pltpu.SMEM((2, tc * TOP_K), jnp.int32), pltpu.SMEM((tc * TOP_K,), F32),
                        pltpu.VMEM((2, tc * TOP_K, sl, ln), F32), pltpu.SemaphoreType.DMA,
                        pltpu.SemaphoreType.DMA((2,))],
        compiler_params=_cparams(("arbitrary",)),
        name="moe_combine",
    )(dest_tiles, gate_tiles, out_sorted_3d, x1_3d, g2_3d)


def _rope_tables(n_batch, seq, ctx_len):
    rows = seq // GRID_W
    row = jnp.broadcast_to(jnp.arange(rows)[:, None], (rows, GRID_W)).reshape(-1).astype(F32)
    col = jnp.broadcast_to(jnp.arange(GRID_W)[None, :], (rows, GRID_W)).reshape(-1).astype(F32)
    half = QK_ROPE // 2
    inv_freq = ROPE_BASE ** (-jnp.arange(0, half, 2, dtype=F32) / half)
    ang = jnp.concatenate([row[:, None] * inv_freq, col[:, None] * inv_freq], axis=-1)
    cos, sin = jnp.cos(ang), jnp.sin(ang)
    zpad = jnp.zeros((seq, LANES - QK_ROPE), F32)
    cos_l = jnp.concatenate([cos, cos, zpad], axis=-1)
    sin_l = jnp.concatenate([-sin, sin, zpad], axis=-1)
    n_ctx = n_batch * ctx_len
    cos_c = jnp.concatenate([jnp.ones((n_ctx, QK_ROPE), F32), jnp.zeros((n_ctx, LANES - QK_ROPE), F32)], axis=-1)
    cos_t = jnp.concatenate([jnp.tile(cos_l, (n_batch, 1)), cos_c], axis=0)
    sin_t = jnp.concatenate([jnp.tile(sin_l, (n_batch, 1)), jnp.zeros((n_ctx, LANES), F32)], axis=0)
    return cos_t, sin_t


def _pad_head_vec(vec):
    return jnp.concatenate([vec, jnp.zeros((QK_PAD - QK_HEAD,), vec.dtype)])[None, :]


def _routing(idx, rank, counts, n_tok, n_exp):
    rows = EXPERT_ITEM_ROWS
    n_items = -(-(n_tok * TOP_K) // rows) + n_exp
    cnt = counts[0, :n_exp]
    blocks = (cnt + rows - 1) // rows
    blk_end = jnp.cumsum(blocks)
    blk_start = blk_end - blocks
    dest = blk_start[idx] * rows + rank
    total = blk_end[-1]
    it = jnp.arange(n_items, dtype=jnp.int32)
    it_c = jnp.minimum(it, total - 1)
    e_of = jnp.minimum(jnp.searchsorted(blk_end, it_c, side="right"), n_exp - 1).astype(jnp.int32)
    left = cnt[e_of] - (it_c - blk_start[e_of]) * rows
    nsub = jnp.clip((left + EXPERT_SUB_ROWS - 1) // EXPERT_SUB_ROWS, 0, rows // EXPERT_SUB_ROWS)
    nsub = jnp.where(it < total, nsub, 0).astype(jnp.int32)
    return dest.astype(jnp.int32), e_of, it_c.astype(jnp.int32), nsub, n_items


def _layer(layer, xt, mods, cos_t, sin_t, p, expert_params, *, n_batch, seq, ctx_len, ctx_out):
    t, d = xt.shape
    d4 = d // 4
    n_heads = (d - 2 * d4) // V_HEAD
    kv_lora = d // 8
    bf = lambda a: a.astype(BF16)
    row = lambda a: a[None, :]

    win = bf(jnp.pad(p["w_in"], ((0, 0), (0, LANES - QK_ROPE))))
    wuq = p["w_uq"].reshape(-1, n_heads, QK_HEAD)
    wuq = bf(jnp.pad(wuq, ((0, 0), (0, 0), (0, QK_PAD - QK_HEAD))).reshape(-1, n_heads * QK_PAD))
    wukv = p["w_ukv"].reshape(kv_lora, n_heads, QK_NOPE + V_HEAD)
    wk = bf(wukv[:, :, :QK_NOPE].reshape(kv_lora, n_heads * QK_NOPE))
    wv = bf(wukv[:, :, QK_NOPE:].reshape(kv_lora, n_heads * V_HEAD))
    qn = _pad_head_vec(p["q_norm"]) * (QK_HEAD ** -0.5)
    kn = _pad_head_vec(p["k_norm"])

    u_pool, u_ssm, q, k, v = _proj(xt, mods, cos_t, sin_t, row(p["norm_mix"]), win, row(p["q_a_norm"]), wuq,
                                   row(p["kv_a_norm"]), wk, wv, qn, kn, n_batch=n_batch, seq=seq)

    y_pool = _pool(u_pool, bf(p["w_pool"]), row(p["pool_scale"]), n_batch=n_batch, seq=seq, ctx_len=ctx_len)

    n_groups = d4 // SSM_GROUP
    nch = t // S5_CHUNK
    prep = _s5_prep(p["ssm_a_re"], p["ssm_a_im"], p["ssm_log_step"], p["ssm_b_re"], p["ssm_b_im"],
                    p["ssm_c_re"], p["ssm_c_im"])
    u_g = u_ssm.astype(BF16).reshape(nch, S5_CHUNK, n_groups, SSM_GROUP).transpose(2, 0, 1, 3).reshape(
        n_groups, nch, S5_CHUNK * SSM_GROUP)
    y_g = _s5(u_g, prep, n_batch=n_batch, seq=seq, ctx_len=ctx_len)
    y_s5 = y_g.reshape(n_groups, nch, S5_CHUNK, SSM_GROUP).transpose(1, 2, 0, 3).reshape(t, d4)

    y_mla_lat, y_mla_ctx = _attention(q, k, v, n_batch=n_batch, seq=seq, ctx_len=ctx_len, ctx_queries=ctx_out)

    n_rows = t if ctx_out else n_batch * seq
    n_exp = p["w_router"].shape[1]
    wr = bf(jnp.pad(p["w_router"], ((0, 0), (0, LANES - n_exp))))
    br = jnp.concatenate([p["b_router"], jnp.full((LANES - n_exp,), NEG_BIG, F32)])[None, :]
    x1, h2, idx, gates, rank, counts = _merge(
        xt, y_pool, y_s5, u_ssm, y_mla_lat, y_mla_ctx, mods, row(p["ssm_d"]), bf(p["w_glu"]), row(p["b_glu"]),
        row(p["out_norm"]), bf(p["w_out"]), row(p["norm_ffn"]), wr, br, n_rows=n_rows, n_batch=n_batch, seq=seq)

    dest, item_e, item_blk, item_nsub, n_items = _routing(idx[:, :TOP_K], rank[:, :TOP_K], counts, n_rows, n_exp)
    sl = d // LANES
    xs = _dispatch(dest.reshape(n_rows // ROW_TILE, ROW_TILE * TOP_K), h2.reshape(n_rows, sl, LANES),
                   n_items * EXPERT_ITEM_ROWS)
    out_sorted = _experts(layer, item_e, item_blk, item_nsub, xs.reshape(-1, d), *expert_params)
    g2 = mods[:, 5 * d:6 * d].reshape(mods.shape[0], sl, LANES)
    nct = n_rows // COMBINE_TILE
    x2 = _combine(dest.reshape(nct, COMBINE_TILE * TOP_K), gates[:, :TOP_K].reshape(nct, COMBINE_TILE * TOP_K),
                  out_sorted.reshape(-1, sl, LANES), x1.reshape(n_rows, sl, LANES), g2, n_batch=n_batch, seq=seq)
    return x2.reshape(n_rows, d)


_PARAM_NAMES = ("norm_mix", "norm_ffn", "w_in", "w_pool", "pool_scale", "ssm_a_re", "ssm_a_im", "ssm_log_step",
                "ssm_b_re", "ssm_b_im", "ssm_c_re", "ssm_c_im", "ssm_d", "w_glu", "b_glu", "q_a_norm", "w_uq",
                "kv_a_norm", "w_ukv", "q_norm", "k_norm", "out_norm", "w_out", "w_router", "b_router")


def kernel(x, c, ctx, c_ctx, w_ada, b_ada, norm_mix, norm_ffn, w_in, w_pool, pool_scale, ssm_a_re, ssm_a_im, ssm_log_step, ssm_b_re, ssm_b_im, ssm_c_re, ssm_c_im, ssm_d, w_glu, b_glu, q_a_norm, w_uq, kv_a_norm, w_ukv, q_norm, k_norm, out_norm, w_out, w_router, b_router, w_gate, b_gate, w_up, b_up, w_down, b_down):
    n_batch, seq, d = x.shape
    ctx_len = ctx.shape[1]
    depth = w_ada.shape[0]
    assert seq % ROW_TILE == 0 and ctx_len % ROW_TILE == 0 and seq % GRID_W == 0 and n_batch + 1 <= 8
    stacked = dict(zip(_PARAM_NAMES, (norm_mix, norm_ffn, w_in, w_pool, pool_scale, ssm_a_re, ssm_a_im,
                                      ssm_log_step, ssm_b_re, ssm_b_im, ssm_c_re, ssm_c_im, ssm_d, w_glu, b_glu,
                                      q_a_norm, w_uq, kv_a_norm, w_ukv, q_norm, k_norm, out_norm, w_out, w_router,
                                      b_router)))
    expert_params = (w_gate, b_gate, w_up, b_up, w_down, b_down)
    cvecs = jnp.concatenate([c, c_ctx[None, :], jnp.zeros((8 - n_batch - 1, d), F32)], axis=0)
    mods = _ada(cvecs, w_ada, b_ada)
    cos_t, sin_t = _rope_tables(n_batch, seq, ctx_len)
    n_lat = n_batch * seq
    xt = jnp.concatenate([x.reshape(n_lat, d), ctx.reshape(n_batch * ctx_len, d)], axis=0)
    for layer in range(depth):
        p = {name: val[layer] for name, val in stacked.items()}
        xt = _layer(layer, xt, mods[layer], cos_t, sin_t, p, expert_params, n_batch=n_batch, seq=seq,
                    ctx_len=ctx_len, ctx_out=layer < depth - 1)
    return xt[:n_lat].reshape(n_batch, seq, d)
```

```python
import functools
import math

import jax
import jax.numpy as jnp
from jax import lax
from jax.experimental import pallas as pl
from jax.experimental.pallas import tpu as pltpu

F32 = jnp.float32
BF16 = jnp.bfloat16

EPS = 1e-6
GRID_W = 64
POOL_WINDOWS = (2, 4, 8, 16)
POOL_HALO = 8
SSM_GROUP = 16
SSM_STATE = 64
S5_CHUNK = 16
S5_TILE_STATES = (128 // SSM_GROUP) * SSM_STATE
QK_NOPE = 128
QK_ROPE = 64
QK_HEAD = QK_NOPE + QK_ROPE
QK_PAD = 256
V_HEAD = 128
ROPE_BASE = 10000.0
TOP_K = 4
SWIGLU_ALPHA = 1.702
SWIGLU_LIMIT = 7.0

LANES = 128
ROW_TILE = 256
ATTN_Q_TILE = 512
ATTN_KEY_CHUNK = 1024
EXPERT_ITEM_ROWS = 1280
EXPERT_SUB_ROWS = 128
EXPERT_TILE = 512
COMBINE_TILE = 128
VMEM_LIMIT = 52 * 1024 * 1024
EXPERT_VMEM_LIMIT = 57 * 1024 * 1024
NEG_BIG = -1e30


def _cparams(sem, vmem=VMEM_LIMIT):
    return pltpu.CompilerParams(dimension_semantics=sem, vmem_limit_bytes=vmem)


def _rms(x, w):
    return x * lax.rsqrt(jnp.mean(x * x, axis=-1, keepdims=True) + EPS) * w


def _full(shape):
    nd = len(shape)
    return pl.BlockSpec(shape, lambda *_: (0,) * nd)


def _ada_kernel(cv_ref, w_ref, b_ref, o_ref):
    cv = cv_ref[...]
    s = (cv * jax.nn.sigmoid(cv)).astype(BF16)
    o_ref[0] = jnp.dot(s, w_ref[0].astype(BF16), preferred_element_type=F32) + b_ref[0]


def _ada(cvecs, w_ada, b_ada):
    n_layers, d, n6 = w_ada.shape
    tn = 512
    return pl.pallas_call(
        _ada_kernel,
        grid=(n_layers, n6 // tn),
        in_specs=[
            pl.BlockSpec((8, d), lambda l, j: (0, 0)),
            pl.BlockSpec((1, d, tn), lambda l, j: (l, 0, j)),
            pl.BlockSpec((1, 1, tn), lambda l, j: (l, 0, j)),
        ],
        out_specs=pl.BlockSpec((1, 8, tn), lambda l, j: (l, 0, j)),
        out_shape=jax.ShapeDtypeStruct((n_layers, 8, n6), F32),
        compiler_params=_cparams(("arbitrary", "arbitrary")),
        name="ada",
    )(cvecs, w_ada, b_ada.reshape(n_layers, 1, n6))


def _rope128(r, cos, sin):
    lane = lax.broadcasted_iota(jnp.int32, r.shape, 1)
    sw = jnp.where((lane % QK_ROPE) < QK_ROPE // 2, pltpu.roll(r, LANES - 32, 1), pltpu.roll(r, 32, 1))
    return r * cos + sw * sin


def _proj_kernel(x_ref, mod_ref, cos_ref, sin_ref, nmix_ref, win_ref, qan_ref, wuq_ref, kvan_ref,
                 wk_ref, wv_ref, qn_ref, kn_ref,
                 upool_ref, ussm_ref, q_ref, k_ref, v_ref, *, tiles_per_seq, n_batch, n_heads, d_model):
    d = d_model
    r = jnp.minimum(pl.program_id(0) // tiles_per_seq, n_batch)
    sh = mod_ref[pl.ds(r, 1), pl.ds(0, d)]
    sc = mod_ref[pl.ds(r, 1), pl.ds(d, d)]
    h = _rms(x_ref[...], nmix_ref[...]) * (1.0 + sc) + sh
    proj = jnp.dot(h.astype(BF16), win_ref[...], preferred_element_type=F32)
    d4 = d // 4
    upool_ref[...] = proj[:, 0:d4]
    ussm_ref[...] = proj[:, 2 * d4:3 * d4]
    cos = cos_ref[...]
    sin = sin_ref[...]

    qa = _rms(proj[:, d4:2 * d4], qan_ref[...]).astype(BF16)
    qf = jnp.dot(qa, wuq_ref[...], preferred_element_type=F32)
    qn_w = qn_ref[...]
    for hd in range(n_heads):
        blk = qf[:, hd * QK_PAD:(hd + 1) * QK_PAD]
        ss = jnp.sum(blk * blk, axis=-1, keepdims=True) * (1.0 / QK_HEAD)
        qn = blk * lax.rsqrt(ss + EPS) * qn_w
        q_ref[:, hd * QK_PAD:hd * QK_PAD + LANES] = qn[:, :LANES].astype(BF16)
        q_ref[:, hd * QK_PAD + LANES:(hd + 1) * QK_PAD] = _rope128(qn[:, LANES:], cos, sin).astype(BF16)

    kv_lo = 3 * d4
    kv_w = d // 8
    ka = _rms(proj[:, kv_lo:kv_lo + kv_w], kvan_ref[...]).astype(BF16)
    kn_all = jnp.dot(ka, wk_ref[...], preferred_element_type=F32)
    v_ref[...] = jnp.dot(ka, wv_ref[...], preferred_element_type=F32).astype(BF16)
    krp = proj[:, kv_lo + kv_w:kv_lo + kv_w + LANES]
    kr_ss = jnp.sum(krp * krp, axis=-1, keepdims=True)
    kn_w = kn_ref[...]
    for hd in range(n_heads):
        kn = kn_all[:, hd * QK_NOPE:(hd + 1) * QK_NOPE]
        ss = (jnp.sum(kn * kn, axis=-1, keepdims=True) + kr_ss) * (1.0 / QK_HEAD)
        rinv = lax.rsqrt(ss + EPS)
        k_ref[:, hd * QK_PAD:hd * QK_PAD + LANES] = (kn * rinv * kn_w[:, :LANES]).astype(BF16)
        k_ref[:, hd * QK_PAD + LANES:(hd + 1) * QK_PAD] = _rope128(
            krp * rinv * kn_w[:, LANES:], cos, sin).astype(BF16)


def _proj(xt, mods, cos_t, sin_t, nmix, win, qan, wuq, kvan, wk, wv, qn, kn, *, n_batch, seq):
    t, d = xt.shape
    tm = ROW_TILE
    n_heads = wv.shape[1] // V_HEAD
    row = lambda w: pl.BlockSpec((tm, w), lambda i: (i, 0))
    kern = functools.partial(_proj_kernel, tiles_per_seq=seq // tm, n_batch=n_batch,
                             n_heads=n_heads, d_model=d)
    return pl.pallas_call(
        kern,
        grid=(t // tm,),
        in_specs=[row(d), _full(mods.shape), row(LANES), row(LANES), _full(nmix.shape), _full(win.shape),
                  _full(qan.shape), _full(wuq.shape), _full(kvan.shape), _full(wk.shape), _full(wv.shape),
                  _full(qn.shape), _full(kn.shape)],
        out_specs=[row(d // 4), row(d // 4), row(n_heads * QK_PAD), row(n_heads * QK_PAD),
                   row(n_heads * V_HEAD)],
        out_shape=[jax.ShapeDtypeStruct((t, d // 4), F32), jax.ShapeDtypeStruct((t, d // 4), F32),
                   jax.ShapeDtypeStruct((t, n_heads * QK_PAD), BF16),
                   jax.ShapeDtypeStruct((t, n_heads * QK_PAD), BF16),
                   jax.ShapeDtypeStruct((t, n_heads * V_HEAD), BF16)],
        compiler_params=_cparams(("arbitrary",)),
        name="mixer_in",
    )(xt, mods, cos_t, sin_t, nmix, win, qan, wuq, kvan, wk, wv, qn, kn)


def _pool_kernel(prev_ref, cur_ref, next_ref, w_ref, scale_ref, o_ref, pad_ref, *,
                 tm, n_lat_tiles, seq, ctx_len):
    i = pl.program_id(0)
    hw = POOL_HALO
    pad_ref[0:hw, :] = prev_ref[...]
    pad_ref[hw:hw + tm, :] = cur_ref[...]
    pad_ref[hw + tm:2 * hw + tm, :] = next_ref[...]
    is_lat = i < n_lat_tiles
    row0 = i * tm
    s0 = jnp.where(is_lat, (row0 // seq) * seq,
                   n_lat_tiles * tm + ((row0 - n_lat_tiles * tm) // ctx_len) * ctx_len)
    slen = jnp.where(is_lat, seq, ctx_len)
    t = row0 - s0 + lax.broadcasted_iota(jnp.int32, (tm, 1), 0)
    for g, w in enumerate(POOL_WINDOWS):
        lanes = slice(g * LANES, (g + 1) * LANES)
        acc = jnp.zeros((tm, LANES), F32)
        for kk in range(-(w // 2), w - w // 2):
            valid = (t + kk >= 0) & (t + kk < slen)
            acc = acc + jnp.where(valid, pad_ref[hw + kk:hw + kk + tm, lanes], 0.0)
        lo = jnp.maximum(t - w // 2, 0)
        hi = jnp.minimum(t + (w - w // 2), slen)
        mean = acc / (hi - lo).astype(F32)
        dlt = (mean - cur_ref[:, lanes]).astype(BF16)
        o_ref[:, lanes] = jnp.dot(dlt, w_ref[g], preferred_element_type=F32) * scale_ref[:, lanes]


def _pool(u_pool, w_pool, pool_scale, *, n_batch, seq, ctx_len):
    t, dp = u_pool.shape
    tm = ROW_TILE
    hb = tm // POOL_HALO
    n_hblocks = t // POOL_HALO
    kern = functools.partial(_pool_kernel, tm=tm, n_lat_tiles=n_batch * seq // tm, seq=seq, ctx_len=ctx_len)
    return pl.pallas_call(
        kern,
        grid=(t // tm,),
        in_specs=[
            pl.BlockSpec((POOL_HALO, dp), lambda i: (jnp.maximum(i * hb - 1, 0), 0)),
            pl.BlockSpec((tm, dp), lambda i: (i, 0)),
            pl.BlockSpec((POOL_HALO, dp), lambda i: (jnp.minimum((i + 1) * hb, n_hblocks - 1), 0)),
            _full(w_pool.shape), _full(pool_scale.shape),
        ],
        out_specs=pl.BlockSpec((tm, dp), lambda i: (i, 0)),
        out_shape=jax.ShapeDtypeStruct((t, dp), F32),
        scratch_shapes=[pltpu.VMEM((tm + 2 * POOL_HALO, dp), F32)],
        compiler_params=_cparams(("arbitrary",)),
        name="pool",
    )(u_pool, u_pool, u_pool, w_pool, pool_scale)


def _s5_prep_kernel(are_ref, aim_ref, ls_ref, bbr_ref, bbi_ref, cbr_ref, cbi_ref,
                    ks_ref, bb_ref, cb_ref, we_ref, wc_ref, lt_ref):
    tc = S5_CHUNK
    fwd = pl.program_id(0) == 0
    step = jnp.exp(ls_ref[0, 0])
    ar = are_ref[0, 0]
    ai = aim_ref[0, 0]
    la = ar * step
    th = ai * step

    def powtab(m):
        mg = jnp.exp(m * la)
        an = m * th
        return mg * jnp.cos(an), mg * jnp.sin(an)

    lam_r, lam_i = powtab(1.0)
    den = ar * ar + ai * ai
    nr = lam_r - 1.0
    coef_r = (nr * ar + lam_i * ai) / den
    coef_i = (lam_i * ar - nr * ai) / den
    bp_r, bp_i = _cmul(coef_r, coef_i, bbr_ref[0, 0], bbi_ref[0, 0])
    bb_ref[0, 0] = jnp.concatenate([bp_r, bp_i], axis=-1).astype(BF16)
    cbr = cbr_ref[0, 0]
    cbi = cbi_ref[0, 0]
    cb_ref[0, 0] = jnp.concatenate([cbr, -cbi], axis=0).astype(BF16)
    jcol = lax.broadcasted_iota(jnp.int32, (tc, 1), 0).astype(F32)
    lag_r, lag_i = powtab(jcol)
    cbr_b = cbr.astype(BF16)
    cbi_b = cbi.astype(BF16)
    for m in range(tc):
        bm_r, bm_i = _cmul(bp_r, bp_i, lag_r[m:m + 1, :], lag_i[m:m + 1, :])
        k_m = (jnp.dot(bm_r.astype(BF16), cbr_b, preferred_element_type=F32)
               - jnp.dot(bm_i.astype(BF16), cbi_b, preferred_element_type=F32))
        ks_ref[0, 0, m * LANES:(m + 1) * LANES, :] = k_m.astype(BF16)
    e_r, e_i = powtab(jnp.where(fwd, (tc - 1.0) - jcol, jcol))
    c_r, c_i = powtab(jnp.where(fwd, jcol + 1.0, tc - jcol))
    we_ref[0, 0] = jnp.concatenate([e_r, e_i], axis=-1)
    wc_ref[0, 0] = jnp.concatenate([c_r, c_i], axis=-1)
    t_r, t_i = powtab(float(tc))
    lt_ref[0, 0] = jnp.concatenate([t_r, t_i], axis=-1)


def _s5_prep(a_re, a_im, log_step, b_re, b_im, c_re, c_im):
    _, g, p = a_re.shape
    c = b_re.shape[-1]
    gq = LANES // c
    nq = g // gq
    sq = gq * p
    tc = S5_CHUNK
    eye = jnp.eye(gq, dtype=F32)
    lanes_of = lambda a: a.reshape(2, nq, 1, sq)
    ls = jnp.broadcast_to(log_step[:, :, None], (2, g, p))
    bbd = lambda b: jnp.einsum("dqgpc,gh->dqgchp", b.reshape(2, nq, gq, p, c), eye).reshape(2, nq, gq * c, sq)
    cbd = lambda m: jnp.einsum("dqgcp,gh->dqgphc", m.reshape(2, nq, gq, c, p), eye).reshape(2, nq, sq, gq * c)
    blk = lambda *s: pl.BlockSpec((1, 1) + s, lambda d, q: (d, q, 0, 0))
    shp = lambda s, dt: jax.ShapeDtypeStruct((2, nq) + s, dt)
    return pl.pallas_call(
        _s5_prep_kernel,
        grid=(2, nq),
        in_specs=[blk(1, sq), blk(1, sq), blk(1, sq), blk(LANES, sq), blk(LANES, sq), blk(sq, LANES),
                  blk(sq, LANES)],
        out_specs=[blk(tc * LANES, LANES), blk(LANES, 2 * sq), blk(2 * sq, LANES), blk(tc, 2 * sq),
                   blk(tc, 2 * sq), blk(1, 2 * sq)],
        out_shape=[shp((tc * LANES, LANES), BF16), shp((LANES, 2 * sq), BF16), shp((2 * sq, LANES), BF16),
                   shp((tc, 2 * sq), F32), shp((tc, 2 * sq), F32), shp((1, 2 * sq), F32)],
        compiler_params=_cparams(("arbitrary", "arbitrary")),
        name="s5_prep",
    )(lanes_of(a_re), lanes_of(a_im), lanes_of(ls), bbd(b_re), bbd(b_im), cbd(c_re), cbd(c_im))


def _cmul(ar, ai, br, bi):
    return ar * br - ai * bi, ar * bi + ai * br


def _s5_kernel(*refs, tm, rev, has_prev):
    if has_prev:
        u_ref, prev_ref, ks_ref, bb_ref, cb_ref, we_ref, wc_ref, lt_ref, y_ref, pad_ref, st_ref = refs
    else:
        u_ref, ks_ref, bb_ref, cb_ref, we_ref, wc_ref, lt_ref, y_ref, pad_ref, st_ref = refs
    tc = S5_CHUNK
    nck = tm // tc
    nq = u_ref.shape[1] // LANES
    sq = S5_TILE_STATES

    @pl.when(pl.program_id(1) == 0)
    def _():
        st_ref[...] = jnp.zeros_like(st_ref)

    pad_ref[0:tc, :] = jnp.zeros((tc, LANES), F32)
    pad_ref[tc + tm:, :] = jnp.zeros((tc, LANES), F32)
    jpos = lax.broadcasted_iota(jnp.int32, (tm, 1), 0) % tc
    order = range(nck - 1, -1, -1) if rev else range(nck)
    for q in range(nq):
        lanes = slice(q * LANES, (q + 1) * LANES)
        u = u_ref[:, lanes]
        pad_ref[tc:tc + tm, :] = u
        cols = []
        for m in range(tc):
            if rev:
                sh = pad_ref[tc + m:tc + m + tm, :]
                keep = jpos + m < tc
            else:
                sh = pad_ref[tc - m:tc - m + tm, :]
                keep = jpos >= m
            cols.append(jnp.where(keep, sh, 0.0).astype(BF16))
        y = jnp.dot(jnp.concatenate(cols, axis=-1), ks_ref[0, q], preferred_element_type=F32)

        x = jnp.dot(u.astype(BF16), bb_ref[0, q], preferred_element_type=F32)
        we = jnp.tile(we_ref[0, q], (nck, 1))
        er, ei = _cmul(x[:, :sq], x[:, sq:], we[:, :sq], we[:, sq:])
        er = jnp.sum(er.reshape(nck, tc, sq), axis=1)
        ei = jnp.sum(ei.reshape(nck, tc, sq), axis=1)
        lam_r = lt_ref[0, q][:, :sq]
        lam_i = lt_ref[0, q][:, sq:]
        s_r = st_ref[2 * q:2 * q + 1, :]
        s_i = st_ref[2 * q + 1:2 * q + 2, :]
        ent_r, ent_i = [None] * nck, [None] * nck
        for k in order:
            ent_r[k] = jnp.broadcast_to(s_r, (tc, sq))
            ent_i[k] = jnp.broadcast_to(s_i, (tc, sq))
            n_r, n_i = _cmul(lam_r, lam_i, s_r, s_i)
            s_r = n_r + er[k:k + 1, :]
            s_i = n_i + ei[k:k + 1, :]
        st_ref[2 * q:2 * q + 1, :] = s_r
        st_ref[2 * q + 1:2 * q + 2, :] = s_i
        wc = jnp.tile(wc_ref[0, q], (nck, 1))
        z_r, z_i = _cmul(wc[:, :sq], wc[:, sq:], jnp.concatenate(ent_r, axis=0), jnp.concatenate(ent_i, axis=0))
        y = y + jnp.dot(jnp.concatenate([z_r, z_i], axis=-1).astype(BF16), cb_ref[0, q],
                        preferred_element_type=F32)
        y_ref[:, lanes] = (prev_ref[:, lanes] + y) if has_prev else y


def _s5(u_ssm, prep, *, n_batch, seq, ctx_len):
    t, dw = u_ssm.shape
    tm = ROW_TILE
    nq = dw // LANES
    tc = S5_CHUNK
    sq2 = 2 * S5_TILE_STATES
    nct, nlt = ctx_len // tm, seq // tm
    lat_tiles = n_batch * nlt
    ks, bb, cb, we, wc, lt = prep

    def tile_of(rev):
        def f(b, s):
            cpos = (nct - 1 - s) if rev else s
            lpos = (nlt - 1 - (s - nct)) if rev else (s - nct)
            return jnp.where(s < nct, lat_tiles + b * nct + cpos, b * nlt + lpos)
        return f

    y = None
    for d, rev in enumerate((False, True)):
        tile = tile_of(rev)
        row = pl.BlockSpec((tm, dw), lambda b, s: (tile(b, s), 0))
        par = lambda *shape: pl.BlockSpec((1, nq) + shape, lambda b, s: (d, 0, 0, 0))
        has_prev = y is not None
        kern = functools.partial(_s5_kernel, tm=tm, rev=rev, has_prev=has_prev)
        y = pl.pallas_call(
            kern,
            grid=(n_batch, nct + nlt),
            in_specs=[row] * (2 if has_prev else 1) + [par(tc * LANES, LANES), par(LANES, sq2), par(sq2, LANES),
                                                      par(tc, sq2), par(tc, sq2), par(1, sq2)],
            out_specs=row,
            out_shape=jax.ShapeDtypeStruct((t, dw), F32),
            scratch_shapes=[pltpu.VMEM((tm + 2 * tc, LANES), F32), pltpu.VMEM((2 * nq, S5_TILE_STATES), F32)],
            compiler_params=_cparams(("arbitrary", "arbitrary")),
            name="s5_scan",
        )(*([u_ssm] + ([y] if has_prev else []) + [ks, bb, cb, we, wc, lt]))
    return y


_NT = (((1,), (1,)), ((), ()))


def _attn_lat_kernel(q_ref, kl_ref, vl_ref, kc_ref, vc_ref, o_ref):
    q = q_ref[...]
    tq = q.shape[0]
    seq = kl_ref.shape[0]
    kc = min(ATTN_KEY_CHUNK, seq)
    chunks = [(kl_ref, vl_ref, c * kc, kc) for c in range(seq // kc)] + [(kc_ref, vc_ref, 0, kc_ref.shape[0])]
    m = jnp.full((tq, 1), NEG_BIG, F32)
    l = jnp.zeros((tq, 1), F32)
    acc = jnp.zeros((tq, V_HEAD), F32)
    for k_ref, v_ref, off, n in chunks:
        s = lax.dot_general(q, k_ref[off:off + n, :], _NT, preferred_element_type=F32)
        m_new = jnp.maximum(m, jnp.max(s, axis=-1, keepdims=True))
        alpha = jnp.exp(m - m_new)
        p = jnp.exp(s - m_new)
        l = alpha * l + jnp.sum(p, axis=-1, keepdims=True)
        acc = alpha * acc + jnp.dot(p.astype(BF16), v_ref[off:off + n, :], preferred_element_type=F32)
        m = m_new
    o_ref[...] = acc / l


def _attn_ctx_kernel(q_ref, kc_ref, vc_ref, o_ref):
    s = lax.dot_general(q_ref[...], kc_ref[...], _NT, preferred_element_type=F32)
    p = jnp.exp(s - jnp.max(s, axis=-1, keepdims=True))
    l = jnp.sum(p, axis=-1, keepdims=True)
    o_ref[...] = jnp.dot(p.astype(BF16), vc_ref[...], preferred_element_type=F32) / l


def _attention(q, k, v, *, n_batch, seq, ctx_len, ctx_queries):
    n_heads = v.shape[1] // V_HEAD
    tq = ATTN_Q_TILE
    nq = seq // tq
    cblk = n_batch * seq // ctx_len
    y_lat = pl.pallas_call(
        _attn_lat_kernel,
        grid=(n_batch, n_heads, nq),
        in_specs=[
            pl.BlockSpec((tq, QK_PAD), lambda b, h, i: (b * nq + i, h)),
            pl.BlockSpec((seq, QK_PAD), lambda b, h, i: (b, h)),
            pl.BlockSpec((seq, V_HEAD), lambda b, h, i: (b, h)),
            pl.BlockSpec((ctx_len, QK_PAD), lambda b, h, i: (cblk + b, h)),
            pl.BlockSpec((ctx_len, V_HEAD), lambda b, h, i: (cblk + b, h)),
        ],
        out_specs=pl.BlockSpec((tq, V_HEAD), lambda b, h, i: (b * nq + i, h)),
        out_shape=jax.ShapeDtypeStruct((n_batch * seq, n_heads * V_HEAD), F32),
        compiler_params=_cparams(("arbitrary", "arbitrary", "arbitrary")),
        name="attn_latent",
    )(q, k, v, k, v)
    if not ctx_queries:
        return y_lat, None
    y_ctx = pl.pallas_call(
        _attn_ctx_kernel,
        grid=(n_batch, n_heads),
        in_specs=[
            pl.BlockSpec((ctx_len, QK_PAD), lambda b, h: (cblk + b, h)),
            pl.BlockSpec((ctx_len, QK_PAD), lambda b, h: (cblk + b, h)),
            pl.BlockSpec((ctx_len, V_HEAD), lambda b, h: (cblk + b, h)),
        ],
        out_specs=pl.BlockSpec((ctx_len, V_HEAD), lambda b, h: (b, h)),
        out_shape=jax.ShapeDtypeStruct((n_batch * ctx_len, n_heads * V_HEAD), F32),
        compiler_params=_cparams(("arbitrary", "arbitrary")),
        name="attn_context",
    )(q, k, v)
    return y_lat, y_ctx


def _merge_kernel(x_ref, yp_ref, ys_ref, us_ref, yml_ref, *rest, tiles_per_seq, n_batch, d_model, has_ctx):
    if has_ctx:
        ymc_ref, rest = rest[0], rest[1:]
    (mod_ref, sd_ref, wglu_ref, bglu_ref, on_ref, wout_ref, nffn_ref, wr_ref, br_ref,
     x1_ref, h2_ref, idx_ref, gate_ref, rank_ref, cnt_ref, carry_ref) = rest
    d = d_model
    d4 = d // 4
    i = pl.program_id(0)
    tm = x_ref.shape[0]
    r = jnp.minimum(i // tiles_per_seq, n_batch)
    g1 = mod_ref[pl.ds(r, 1), pl.ds(2 * d, d)]
    sh2 = mod_ref[pl.ds(r, 1), pl.ds(3 * d, d)]
    sc2 = mod_ref[pl.ds(r, 1), pl.ds(4 * d, d)]

    z = ys_ref[...] + sd_ref[...] * us_ref[...]
    g = jax.nn.gelu(z, approximate=True)
    glu = jnp.dot(g.astype(BF16), wglu_ref[...], preferred_element_type=F32) + bglu_ref[...]
    y_ssm = g * jax.nn.sigmoid(glu)
    on = on_ref[...]
    y_mla = yml_ref[...]
    if has_ctx:
        y_mla = jnp.where(i < n_batch * tiles_per_seq, y_mla, ymc_ref[...])
    cat = jnp.concatenate([_rms(yp_ref[...], on[:, 0:d4]), _rms(y_ssm, on[:, d4:2 * d4]),
                           _rms(y_mla, on[:, 2 * d4:])], axis=-1).astype(BF16)
    x1 = x_ref[...] + g1 * jnp.dot(cat, wout_ref[...], preferred_element_type=F32)
    x1_ref[...] = x1
    h2 = (_rms(x1, nffn_ref[...]) * (1.0 + sc2) + sh2).astype(BF16)
    h2_ref[...] = h2

    logits = jnp.dot(h2, wr_ref[...], preferred_element_type=F32) + br_ref[...]
    lane = lax.broadcasted_iota(jnp.int32, (tm, LANES), 1)
    vals, hots = [], []
    idx_out = jnp.zeros((tm, LANES), jnp.int32)
    for kk in range(TOP_K):
        mx = jnp.max(logits, axis=-1, keepdims=True)
        sel = jnp.min(jnp.where(logits == mx, lane, LANES), axis=-1, keepdims=True)
        hot = lane == sel
        vals.append(mx)
        hots.append(hot)
        idx_out = jnp.where(lane == kk, sel, idx_out)
        logits = jnp.where(hot, NEG_BIG * 2.0, logits)
    es = [jnp.exp(vv - vals[0]) for vv in vals]
    den = es[0] + es[1] + es[2] + es[3]
    gate_out = jnp.zeros((tm, LANES), F32)
    for kk in range(TOP_K):
        gate_out = jnp.where(lane == kk, es[kk] / den, gate_out)
    idx_ref[...] = idx_out
    gate_ref[...] = gate_out

    @pl.when(i == 0)
    def _():
        carry_ref[...] = jnp.zeros_like(carry_ref)

    onehot = (hots[0] | hots[1] | hots[2] | hots[3]).astype(BF16)
    rr = lax.broadcasted_iota(jnp.int32, (tm, tm), 0)
    cc = lax.broadcasted_iota(jnp.int32, (tm, tm), 1)
    tri = (cc < rr).astype(BF16)
    pos = jnp.dot(tri, onehot, preferred_element_type=F32) + carry_ref[...]
    rank_out = jnp.zeros((tm, LANES), jnp.int32)
    for kk in range(TOP_K):
        rk = jnp.sum(jnp.where(hots[kk], pos, 0.0), axis=-1, keepdims=True).astype(jnp.int32)
        rank_out = jnp.where(lane == kk, rk, rank_out)
    rank_ref[...] = rank_out
    carry = carry_ref[...] + jnp.sum(onehot.astype(F32), axis=0, keepdims=True)
    carry_ref[...] = carry
    cnt_ref[...] = carry.astype(jnp.int32)


def _merge(xt, y_pool, y_s5, u_ssm, y_mla_lat, y_mla_ctx, mods, ssm_d, wglu, bglu, out_norm, wout, nffn, wr, br, *,
           n_rows, n_batch, seq):
    d = xt.shape[1]
    tm = ROW_TILE
    n_lat_tiles = n_batch * seq // tm
    has_ctx = y_mla_ctx is not None
    row = lambda w: pl.BlockSpec((tm, w), lambda i: (i, 0))
    kern = functools.partial(_merge_kernel, tiles_per_seq=seq // tm, n_batch=n_batch, d_model=d, has_ctx=has_ctx)
    small = [mods, ssm_d, wglu, bglu, out_norm, wout, nffn, wr, br]
    y_mla = [y_mla_lat] + ([y_mla_ctx] if has_ctx else [])
    y_specs = [pl.BlockSpec((tm, d // 2), lambda i: (jnp.minimum(i, n_lat_tiles - 1), 0))]
    if has_ctx:
        y_specs.append(pl.BlockSpec((tm, d // 2), lambda i: (jnp.maximum(i - n_lat_tiles, 0), 0)))
    return pl.pallas_call(
        kern,
        grid=(n_rows // tm,),
        in_specs=[row(d), row(d // 4), row(d // 4), row(d // 4)] + y_specs + [_full(a.shape) for a in small],
        out_specs=[row(d), row(d), row(LANES), row(LANES), row(LANES), _full((1, LANES))],
        out_shape=[jax.ShapeDtypeStruct((n_rows, d), F32), jax.ShapeDtypeStruct((n_rows, d), BF16),
                   jax.ShapeDtypeStruct((n_rows, LANES), jnp.int32), jax.ShapeDtypeStruct((n_rows, LANES), F32),
                   jax.ShapeDtypeStruct((n_rows, LANES), jnp.int32), jax.ShapeDtypeStruct((1, LANES), jnp.int32)],
        scratch_shapes=[pltpu.VMEM((1, LANES), F32)],
        compiler_params=_cparams(("arbitrary",)),
        name="merge_router",
    )(xt, y_pool, y_s5, u_ssm, *y_mla, *small)


def _row_copy(src_hbm, dst_hbm, sem, src_row, dst_row):
    return pltpu.make_async_copy(src_hbm.at[src_row], dst_hbm.at[dst_row], sem)


def _dispatch_kernel(dest_hbm, h2_ref, zeros_hbm, xs_hbm, dest_smem, sem_idx, sem_rows, *, tm):
    del zeros_hbm
    i = pl.program_id(0)
    n = tm * TOP_K
    idx_copy = pltpu.make_async_copy(dest_hbm.at[i], dest_smem, sem_idx)
    idx_copy.start()
    idx_copy.wait()

    def issue(j, _):
        _row_copy(h2_ref, xs_hbm, sem_rows, j // TOP_K, dest_smem[j]).start()
        return 0

    lax.fori_loop(0, n, issue, 0, unroll=8)

    def drain(j, _):
        _row_copy(h2_ref, xs_hbm, sem_rows, j // TOP_K, dest_smem[j]).wait()
        return 0

    lax.fori_loop(0, n, drain, 0, unroll=8)


def _dispatch(dest_tiles, h2_3d, n_sorted_rows):
    n_tok, sl, ln = h2_3d.shape
    tm = ROW_TILE
    zeros = jnp.zeros((n_sorted_rows,) + h2_3d.shape[1:], h2_3d.dtype)
    return pl.pallas_call(
        functools.partial(_dispatch_kernel, tm=tm),
        grid=(n_tok // tm,),
        in_specs=[pl.BlockSpec(memory_space=pl.ANY), pl.BlockSpec((tm, sl, ln), lambda i: (i, 0, 0)),
                  pl.BlockSpec(memory_space=pl.ANY)],
        out_specs=pl.BlockSpec(memory_space=pl.ANY),
        out_shape=jax.ShapeDtypeStruct(zeros.shape, zeros.dtype),
        input_output_aliases={2: 0},
        scratch_shapes=[pltpu.SMEM((tm * TOP_K,), jnp.int32), pltpu.SemaphoreType.DMA, pltpu.SemaphoreType.DMA],
        compiler_params=_cparams(("arbitrary",)),
        name="moe_dispatch",
    )(dest_tiles, h2_3d, zeros)


def _expert_kernel(ie_ref, ib_ref, ns_ref, x_ref, wg_ref, bg_ref, wu_ref, bu_ref, wd_ref, bd_ref, o_ref, act_ref,
                   *, nf):
    it = pl.program_id(0)
    j = pl.program_id(1)
    ns = ns_ref[it]
    rows_max = x_ref.shape[0]
    for m in range(1, rows_max // EXPERT_SUB_ROWS + 1):
        mr = m * EXPERT_SUB_ROWS

        @pl.when((ns == m) & (j < nf))
        def _():
            x = x_ref[0:mr, :]
            g = jnp.dot(x, wg_ref[0, 0].astype(BF16), preferred_element_type=F32) + bg_ref[0, 0]
            u = jnp.dot(x, wu_ref[0, 0].astype(BF16), preferred_element_type=F32) + bu_ref[0, 0]
            g = jnp.minimum(g, SWIGLU_LIMIT)
            u = jnp.clip(u, -SWIGLU_LIMIT, SWIGLU_LIMIT)
            act_ref[j, 0:mr, :] = (g * jax.nn.sigmoid(SWIGLU_ALPHA * g) * (u + 1.0)).astype(BF16)

        @pl.when((ns == m) & (j >= nf))
        def _():
            act = jnp.concatenate([act_ref[f, 0:mr, :] for f in range(nf)], axis=-1)
            o_ref[0:mr, :] = jnp.dot(act, wd_ref[0, 0].astype(BF16), preferred_element_type=F32) + bd_ref[0, 0]
            if mr < rows_max:
                o_ref[mr:, :] = jnp.zeros((rows_max - mr, o_ref.shape[1]), F32)

    @pl.when((ns == 0) & (j >= nf))
    def _():
        o_ref[...] = jnp.zeros(o_ref.shape, F32)


def _experts(layer, item_e, item_blk, item_nsub, xs, w_gate, b_gate, w_up, b_up, w_down, b_down):
    n_items = item_e.shape[0]
    n_layers, n_exp, d, f_dim = w_gate.shape
    tf = EXPERT_TILE
    nf = f_dim // tf
    rows = EXPERT_ITEM_ROWS
    assert d // tf == nf

    def fi(j, ns, it):
        return jnp.where(ns[it] > 0, jnp.minimum(j, nf - 1), nf - 1)

    def ni(j, ns, it):
        return jnp.where(ns[it] > 0, jnp.maximum(j - nf, 0), nf - 1)

    return pl.pallas_call(
        functools.partial(_expert_kernel, nf=nf),
        grid_spec=pltpu.PrefetchScalarGridSpec(
            num_scalar_prefetch=3,
            grid=(n_items, 2 * nf),
            in_specs=[
                pl.BlockSpec((rows, d), lambda it, j, ie, ib, ns: (ib[it], 0)),
                pl.BlockSpec((1, 1, d, tf), lambda it, j, ie, ib, ns: (layer, ie[it], 0, fi(j, ns, it))),
                pl.BlockSpec((1, 1, 1, tf), lambda it, j, ie, ib, ns: (layer, ie[it], 0, fi(j, ns, it))),
                pl.BlockSpec((1, 1, d, tf), lambda it, j, ie, ib, ns: (layer, ie[it], 0, fi(j, ns, it))),
                pl.BlockSpec((1, 1, 1, tf), lambda it, j, ie, ib, ns: (layer, ie[it], 0, fi(j, ns, it))),
                pl.BlockSpec((1, 1, f_dim, tf), lambda it, j, ie, ib, ns: (layer, ie[it], 0, ni(j, ns, it))),
                pl.BlockSpec((1, 1, 1, tf), lambda it, j, ie, ib, ns: (layer, ie[it], 0, ni(j, ns, it))),
            ],
            out_specs=pl.BlockSpec((rows, tf), lambda it, j, ie, ib, ns: (it, jnp.maximum(j - nf, 0))),
            scratch_shapes=[pltpu.VMEM((nf, rows, tf), BF16)],
        ),
        out_shape=jax.ShapeDtypeStruct((n_items * rows, d), F32),
        compiler_params=_cparams(("arbitrary", "arbitrary"), EXPERT_VMEM_LIMIT),
        name="moe_experts",
    )(item_e, item_blk, item_nsub, xs, w_gate, b_gate.reshape(n_layers, n_exp, 1, f_dim), w_up,
      b_up.reshape(n_layers, n_exp, 1, f_dim), w_down, b_down.reshape(n_layers, n_exp, 1, d))


def _combine_kernel(dest_hbm, gate_hbm, os_hbm, x_ref, g2_ref, o_ref, dest_smem, gate_smem, buf, sem_idx, sem_rows,
                    *, tc, n_tiles, tiles_per_seq, n_batch):
    i = pl.program_id(0)
    n = tc * TOP_K
    slot = i % 2

    def fetch(tile, s):
        idx_copy = pltpu.make_async_copy(dest_hbm.at[tile], dest_smem.at[s], sem_idx)
        idx_copy.start()
        idx_copy.wait()

        def issue(j, _):
            _row_copy(os_hbm, buf.at[s], sem_rows.at[s], dest_smem[s, j], j).start()
            return 0

        lax.fori_loop(0, n, issue, 0, unroll=8)

    @pl.when(i == 0)
    def _():
        fetch(0, 0)

    @pl.when(i + 1 < n_tiles)
    def _():
        fetch(i + 1, 1 - slot)

    gate_copy = pltpu.make_async_copy(gate_hbm.at[i], gate_smem, sem_idx)
    gate_copy.start()
    gate_copy.wait()

    def drain(j, _):
        _row_copy(os_hbm, buf.at[slot], sem_rows.at[slot], dest_smem[slot, j], j).wait()
        return 0

    lax.fori_loop(0, n, drain, 0, unroll=8)
    g2 = g2_ref[jnp.minimum(i // tiles_per_seq, n_batch)]
    cur = buf.at[slot]

    def row(rw, _):
        acc = gate_smem[rw * TOP_K] * cur[rw * TOP_K]
        for kk in range(1, TOP_K):
            acc = acc + gate_smem[rw * TOP_K + kk] * cur[rw * TOP_K + kk]
        o_ref[rw] = x_ref[rw] + g2 * acc
        return 0

    lax.fori_loop(0, tc, row, 0, unroll=2)


def _combine(dest_tiles, gate_tiles, out_sorted_3d, x1_3d, g2_3d, *, n_batch, seq):
    n_tok, sl, ln = x1_3d.shape
    tc = COMBINE_TILE
    kern = functools.partial(_combine_kernel, tc=tc, n_tiles=n_tok // tc, tiles_per_seq=seq // tc, n_batch=n_batch)
    return pl.pallas_call(
        kern,
        grid=(n_tok // tc,),
        in_specs=[pl.BlockSpec(memory_space=pl.ANY)] * 3 + [
            pl.BlockSpec((tc, sl, ln), lambda i: (i, 0, 0)), _full(g2_3d.shape)],
        out_specs=pl.BlockSpec((tc, sl, ln), lambda i: (i, 0, 0)),
        out_shape=jax.ShapeDtypeStruct(x1_3d.shape, F32),
        scratch_shapes=[pltpu.SMEM((2, tc * TOP_K), jnp.int32), pltpu.SMEM((tc * TOP_K,), F32),
                        pltpu.VMEM((2, tc * TOP_K, sl, ln), F32), pltpu.SemaphoreType.DMA,
                        pltpu.SemaphoreType.DMA((2,))],
        compiler_params=_cparams(("arbitrary",)),
        name="moe_combine",
    )(dest_tiles, gate_tiles, out_sorted_3d, x1_3d, g2_3d)


def _rope_tables(n_batch, seq, ctx_len):
    rows = seq // GRID_W
    row = jnp.broadcast_to(jnp.arange(rows)[:, None], (rows, GRID_W)).reshape(-1).astype(F32)
    col = jnp.broadcast_to(jnp.arange(GRID_W)[None, :], (rows, GRID_W)).reshape(-1).astype(F32)
    half = QK_ROPE // 2
    inv_freq = ROPE_BASE ** (-jnp.arange(0, half, 2, dtype=F32) / half)
    ang = jnp.concatenate([row[:, None] * inv_freq, col[:, None] * inv_freq], axis=-1)
    cos, sin = jnp.cos(ang), jnp.sin(ang)
    zpad = jnp.zeros((seq, LANES - QK_ROPE), F32)
    cos_l = jnp.concatenate([cos, cos, zpad], axis=-1)
    sin_l = jnp.concatenate([-sin, sin, zpad], axis=-1)
    n_ctx = n_batch * ctx_len
    cos_c = jnp.concatenate([jnp.ones((n_ctx, QK_ROPE), F32), jnp.zeros((n_ctx, LANES - QK_ROPE), F32)], axis=-1)
    cos_t = jnp.concatenate([jnp.tile(cos_l, (n_batch, 1)), cos_c], axis=0)
    sin_t = jnp.concatenate([jnp.tile(sin_l, (n_batch, 1)), jnp.zeros((n_ctx, LANES), F32)], axis=0)
    return cos_t, sin_t


def _pad_head_vec(vec):
    return jnp.concatenate([vec, jnp.zeros((QK_PAD - QK_HEAD,), vec.dtype)])[None, :]


def _routing(idx, rank, counts, n_tok, n_exp):
    rows = EXPERT_ITEM_ROWS
    n_items = -(-(n_tok * TOP_K) // rows) + n_exp
    cnt = counts[0, :n_exp]
    blocks = (cnt + rows - 1) // rows
    blk_end = jnp.cumsum(blocks)
    blk_start = blk_end - blocks
    dest = blk_start[idx] * rows + rank
    total = blk_end[-1]
    it = jnp.arange(n_items, dtype=jnp.int32)
    it_c = jnp.minimum(it, total - 1)
    e_of = jnp.minimum(jnp.searchsorted(blk_end, it_c, side="right"), n_exp - 1).astype(jnp.int32)
    left = cnt[e_of] - (it_c - blk_start[e_of]) * rows
    nsub = jnp.clip((left + EXPERT_SUB_ROWS - 1) // EXPERT_SUB_ROWS, 0, rows // EXPERT_SUB_ROWS)
    nsub = jnp.where(it < total, nsub, 0).astype(jnp.int32)
    return dest.astype(jnp.int32), e_of, it_c.astype(jnp.int32), nsub, n_items


def _layer(layer, xt, mods, cos_t, sin_t, p, expert_params, *, n_batch, seq, ctx_len, ctx_out):
    t, d = xt.shape
    d4 = d // 4
    n_heads = (d - 2 * d4) // V_HEAD
    kv_lora = d // 8
    bf = lambda a: a.astype(BF16)
    row = lambda a: a[None, :]

    win = bf(jnp.pad(p["w_in"], ((0, 0), (0, LANES - QK_ROPE))))
    wuq = p["w_uq"].reshape(-1, n_heads, QK_HEAD)
    wuq = bf(jnp.pad(wuq, ((0, 0), (0, 0), (0, QK_PAD - QK_HEAD))).reshape(-1, n_heads * QK_PAD))
    wukv = p["w_ukv"].reshape(kv_lora, n_heads, QK_NOPE + V_HEAD)
    wk = bf(wukv[:, :, :QK_NOPE].reshape(kv_lora, n_heads * QK_NOPE))
    wv = bf(wukv[:, :, QK_NOPE:].reshape(kv_lora, n_heads * V_HEAD))
    qn = _pad_head_vec(p["q_norm"]) * (QK_HEAD ** -0.5)
    kn = _pad_head_vec(p["k_norm"])

    u_pool, u_ssm, q, k, v = _proj(xt, mods, cos_t, sin_t, row(p["norm_mix"]), win, row(p["q_a_norm"]), wuq,
                                   row(p["kv_a_norm"]), wk, wv, qn, kn, n_batch=n_batch, seq=seq)

    y_pool = _pool(u_pool, bf(p["w_pool"]), row(p["pool_scale"]), n_batch=n_batch, seq=seq, ctx_len=ctx_len)

    prep = _s5_prep(p["ssm_a_re"], p["ssm_a_im"], p["ssm_log_step"], p["ssm_b_re"], p["ssm_b_im"],
                    p["ssm_c_re"], p["ssm_c_im"])
    y_s5 = _s5(u_ssm, prep, n_batch=n_batch, seq=seq, ctx_len=ctx_len)

    y_mla_lat, y_mla_ctx = _attention(q, k, v, n_batch=n_batch, seq=seq, ctx_len=ctx_len, ctx_queries=ctx_out)

    n_rows = t if ctx_out else n_batch * seq
    n_exp = p["w_router"].shape[1]
    wr = bf(jnp.pad(p["w_router"], ((0, 0), (0, LANES - n_exp))))
    br = jnp.concatenate([p["b_router"], jnp.full((LANES - n_exp,), NEG_BIG, F32)])[None, :]
    x1, h2, idx, gates, rank, counts = _merge(
        xt, y_pool, y_s5, u_ssm, y_mla_lat, y_mla_ctx, mods, row(p["ssm_d"]), bf(p["w_glu"]), row(p["b_glu"]),
        row(p["out_norm"]), bf(p["w_out"]), row(p["norm_ffn"]), wr, br, n_rows=n_rows, n_batch=n_batch, seq=seq)

    dest, item_e, item_blk, item_nsub, n_items = _routing(idx[:, :TOP_K], rank[:, :TOP_K], counts, n_rows, n_exp)
    sl = d // LANES
    xs = _dispatch(dest.reshape(n_rows // ROW_TILE, ROW_TILE * TOP_K), h2.reshape(n_rows, sl, LANES),
                   n_items * EXPERT_ITEM_ROWS)
    out_sorted = _experts(layer, item_e, item_blk, item_nsub, xs.reshape(-1, d), *expert_params)
    g2 = mods[:, 5 * d:6 * d].reshape(mods.shape[0], sl, LANES)
    nct = n_rows // COMBINE_TILE
    x2 = _combine(dest.reshape(nct, COMBINE_TILE * TOP_K), gates[:, :TOP_K].reshape(nct, COMBINE_TILE * TOP_K),
                  out_sorted.reshape(-1, sl, LANES), x1.reshape(n_rows, sl, LANES), g2, n_batch=n_batch, seq=seq)
    return x2.reshape(n_rows, d)


_PARAM_NAMES = ("norm_mix", "norm_ffn", "w_in", "w_pool", "pool_scale", "ssm_a_re", "ssm_a_im", "ssm_log_step",
                "ssm_b_re", "ssm_b_im", "ssm_c_re", "ssm_c_im", "ssm_d", "w_glu", "b_glu", "q_a_norm", "w_uq",
                "kv_a_norm", "w_ukv", "q_norm", "k_norm", "out_norm", "w_out", "w_router", "b_router")


def kernel(x, c, ctx, c_ctx, w_ada, b_ada, norm_mix, norm_ffn, w_in, w_pool, pool_scale, ssm_a_re, ssm_a_im, ssm_log_step, ssm_b_re, ssm_b_im, ssm_c_re, ssm_c_im, ssm_d, w_glu, b_glu, q_a_norm, w_uq, kv_a_norm, w_ukv, q_norm, k_norm, out_norm, w_out, w_router, b_router, w_gate, b_gate, w_up, b_up, w_down, b_down):
    n_batch, seq, d = x.shape
    ctx_len = ctx.shape[1]
    depth = w_ada.shape[0]
    assert seq % ROW_TILE == 0 and ctx_len % ROW_TILE == 0 and seq % GRID_W == 0 and n_batch + 1 <= 8
    stacked = dict(zip(_PARAM_NAMES, (norm_mix, norm_ffn, w_in, w_pool, pool_scale, ssm_a_re, ssm_a_im,
                                      ssm_log_step, ssm_b_re, ssm_b_im, ssm_c_re, ssm_c_im, ssm_d, w_glu, b_glu,
                                      q_a_norm, w_uq, kv_a_norm, w_ukv, q_norm, k_norm, out_norm, w_out, w_router,
                                      b_router)))
    expert_params = (w_gate, b_gate, w_up, b_up, w_down, b_down)
    cvecs = jnp.concatenate([c, c_ctx[None, :], jnp.zeros((8 - n_batch - 1, d), F32)], axis=0)
    mods = _ada(cvecs, w_ada, b_ada)
    cos_t, sin_t = _rope_tables(n_batch, seq, ctx_len)
    n_lat = n_batch * seq
    xt = jnp.concatenate([x.reshape(n_lat, d), ctx.reshape(n_batch * ctx_len, d)], axis=0)
    for layer in range(depth):
        p = {name: val[layer] for name, val in stacked.items()}
        xt = _layer(layer, xt, mods[layer], cos_t, sin_t, p, expert_params, n_batch=n_batch, seq=seq,
                    ctx_len=ctx_len, ctx_out=layer < depth - 1)
    return xt[:n_lat].reshape(n_batch, seq, d)
```

```python
import functools
import math

import jax
import jax.numpy as jnp
from jax import lax
from jax.experimental import pallas as pl
from jax.experimental.pallas import tpu as pltpu

F32 = jnp.float32
BF16 = jnp.bfloat16

EPS = 1e-6
GRID_W = 64
POOL_WINDOWS = (2, 4, 8, 16)
POOL_HALO = 8
SSM_GROUP = 16
SSM_STATE = 64
S5_CHUNK = 16
S5_TILE_STATES = (128 // SSM_GROUP) * SSM_STATE
QK_NOPE = 128
QK_ROPE = 64
QK_HEAD = QK_NOPE + QK_ROPE
QK_PAD = 256
V_HEAD = 128
ROPE_BASE = 10000.0
TOP_K = 4
SWIGLU_ALPHA = 1.702
SWIGLU_LIMIT = 7.0

LANES = 128
ROW_TILE = 256
ATTN_Q_TILE = 512
ATTN_KEY_CHUNK = 1024
EXPERT_ITEM_ROWS = 1280
EXPERT_SUB_ROWS = 256
EXPERT_TILE = 512
COMBINE_TILE = 128
VMEM_LIMIT = 52 * 1024 * 1024
EXPERT_VMEM_LIMIT = 57 * 1024 * 1024
NEG_BIG = -1e30


def _cparams(sem, vmem=VMEM_LIMIT):
    return pltpu.CompilerParams(dimension_semantics=sem, vmem_limit_bytes=vmem)


def _rms(x, w):
    return x * lax.rsqrt(jnp.mean(x * x, axis=-1, keepdims=True) + EPS) * w


def _full(shape):
    nd = len(shape)
    return pl.BlockSpec(shape, lambda *_: (0,) * nd)


def _ada_kernel(cv_ref, w_ref, b_ref, o_ref):
    cv = cv_ref[...]
    s = (cv * jax.nn.sigmoid(cv)).astype(BF16)
    o_ref[0] = jnp.dot(s, w_ref[0].astype(BF16), preferred_element_type=F32) + b_ref[0]


def _ada(cvecs, w_ada, b_ada):
    n_layers, d, n6 = w_ada.shape
    tn = 512
    return pl.pallas_call(
        _ada_kernel,
        grid=(n_layers, n6 // tn),
        in_specs=[
            pl.BlockSpec((8, d), lambda l, j: (0, 0)),
            pl.BlockSpec((1, d, tn), lambda l, j: (l, 0, j)),
            pl.BlockSpec((1, 1, tn), lambda l, j: (l, 0, j)),
        ],
        out_specs=pl.BlockSpec((1, 8, tn), lambda l, j: (l, 0, j)),
        out_shape=jax.ShapeDtypeStruct((n_layers, 8, n6), F32),
        compiler_params=_cparams(("arbitrary", "arbitrary")),
        name="ada",
    )(cvecs, w_ada, b_ada.reshape(n_layers, 1, n6))


def _rope128(r, cos, sin):
    lane = lax.broadcasted_iota(jnp.int32, r.shape, 1)
    sw = jnp.where((lane % QK_ROPE) < QK_ROPE // 2, pltpu.roll(r, LANES - 32, 1), pltpu.roll(r, 32, 1))
    return r * cos + sw * sin


def _proj_kernel(x_ref, mod_ref, cos_ref, sin_ref, nmix_ref, win_ref, qan_ref, wuq_ref, kvan_ref,
                 wk_ref, wv_ref, qn_ref, kn_ref,
                 upool_ref, ussm_ref, q_ref, k_ref, v_ref, *, tiles_per_seq, n_batch, n_heads, d_model):
    d = d_model
    r = jnp.minimum(pl.program_id(0) // tiles_per_seq, n_batch)
    sh = mod_ref[pl.ds(r, 1), pl.ds(0, d)]
    sc = mod_ref[pl.ds(r, 1), pl.ds(d, d)]
    h = _rms(x_ref[...], nmix_ref[...]) * (1.0 + sc) + sh
    proj = jnp.dot(h.astype(BF16), win_ref[...], preferred_element_type=F32)
    d4 = d // 4
    upool_ref[...] = proj[:, 0:d4]
    ussm_ref[...] = proj[:, 2 * d4:3 * d4]
    cos = cos_ref[...]
    sin = sin_ref[...]

    qa = _rms(proj[:, d4:2 * d4], qan_ref[...]).astype(BF16)
    qf = jnp.dot(qa, wuq_ref[...], preferred_element_type=F32)
    qn_w = qn_ref[...]
    for hd in range(n_heads):
        blk = qf[:, hd * QK_PAD:(hd + 1) * QK_PAD]
        ss = jnp.sum(blk * blk, axis=-1, keepdims=True) * (1.0 / QK_HEAD)
        qn = blk * lax.rsqrt(ss + EPS) * qn_w
        q_ref[:, hd * QK_PAD:hd * QK_PAD + LANES] = qn[:, :LANES].astype(BF16)
        q_ref[:, hd * QK_PAD + LANES:(hd + 1) * QK_PAD] = _rope128(qn[:, LANES:], cos, sin).astype(BF16)

    kv_lo = 3 * d4
    kv_w = d // 8
    ka = _rms(proj[:, kv_lo:kv_lo + kv_w], kvan_ref[...]).astype(BF16)
    kn_all = jnp.dot(ka, wk_ref[...], preferred_element_type=F32)
    v_ref[...] = jnp.dot(ka, wv_ref[...], preferred_element_type=F32).astype(BF16)
    krp = proj[:, kv_lo + kv_w:kv_lo + kv_w + LANES]
    kr_ss = jnp.sum(krp * krp, axis=-1, keepdims=True)
    kn_w = kn_ref[...]
    for hd in range(n_heads):
        kn = kn_all[:, hd * QK_NOPE:(hd + 1) * QK_NOPE]
        ss = (jnp.sum(kn * kn, axis=-1, keepdims=True) + kr_ss) * (1.0 / QK_HEAD)
        rinv = lax.rsqrt(ss + EPS)
        k_ref[:, hd * QK_PAD:hd * QK_PAD + LANES] = (kn * rinv * kn_w[:, :LANES]).astype(BF16)
        k_ref[:, hd * QK_PAD + LANES:(hd + 1) * QK_PAD] = _rope128(
            krp * rinv * kn_w[:, LANES:], cos, sin).astype(BF16)


def _proj(xt, mods, cos_t, sin_t, nmix, win, qan, wuq, kvan, wk, wv, qn, kn, *, n_batch, seq):
    t, d = xt.shape
    tm = ROW_TILE
    n_heads = wv.shape[1] // V_HEAD
    row = lambda w: pl.BlockSpec((tm, w), lambda i: (i, 0))
    kern = functools.partial(_proj_kernel, tiles_per_seq=seq // tm, n_batch=n_batch,
                             n_heads=n_heads, d_model=d)
    return pl.pallas_call(
        kern,
        grid=(t // tm,),
        in_specs=[row(d), _full(mods.shape), row(LANES), row(LANES), _full(nmix.shape), _full(win.shape),
                  _full(qan.shape), _full(wuq.shape), _full(kvan.shape), _full(wk.shape), _full(wv.shape),
                  _full(qn.shape), _full(kn.shape)],
        out_specs=[row(d // 4), row(d // 4), row(n_heads * QK_PAD), row(n_heads * QK_PAD),
                   row(n_heads * V_HEAD)],
        out_shape=[jax.ShapeDtypeStruct((t, d // 4), F32), jax.ShapeDtypeStruct((t, d // 4), F32),
                   jax.ShapeDtypeStruct((t, n_heads * QK_PAD), BF16),
                   jax.ShapeDtypeStruct((t, n_heads * QK_PAD), BF16),
                   jax.ShapeDtypeStruct((t, n_heads * V_HEAD), BF16)],
        compiler_params=_cparams(("arbitrary",)),
        name="mixer_in",
    )(xt, mods, cos_t, sin_t, nmix, win, qan, wuq, kvan, wk, wv, qn, kn)


def _pool_kernel(prev_ref, cur_ref, next_ref, w_ref, scale_ref, o_ref, pad_ref, *,
                 tm, n_lat_tiles, seq, ctx_len):
    i = pl.program_id(0)
    hw = POOL_HALO
    pad_ref[0:hw, :] = prev_ref[...]
    pad_ref[hw:hw + tm, :] = cur_ref[...]
    pad_ref[hw + tm:2 * hw + tm, :] = next_ref[...]
    is_lat = i < n_lat_tiles
    row0 = i * tm
    s0 = jnp.where(is_lat, (row0 // seq) * seq,
                   n_lat_tiles * tm + ((row0 - n_lat_tiles * tm) // ctx_len) * ctx_len)
    slen = jnp.where(is_lat, seq, ctx_len)
    t = row0 - s0 + lax.broadcasted_iota(jnp.int32, (tm, 1), 0)
    for g, w in enumerate(POOL_WINDOWS):
        lanes = slice(g * LANES, (g + 1) * LANES)
        acc = jnp.zeros((tm, LANES), F32)
        for kk in range(-(w // 2), w - w // 2):
            valid = (t + kk >= 0) & (t + kk < slen)
            acc = acc + jnp.where(valid, pad_ref[hw + kk:hw + kk + tm, lanes], 0.0)
        lo = jnp.maximum(t - w // 2, 0)
        hi = jnp.minimum(t + (w - w // 2), slen)
        mean = acc / (hi - lo).astype(F32)
        dlt = (mean - cur_ref[:, lanes]).astype(BF16)
        o_ref[:, lanes] = jnp.dot(dlt, w_ref[g], preferred_element_type=F32) * scale_ref[:, lanes]


def _pool(u_pool, w_pool, pool_scale, *, n_batch, seq, ctx_len):
    t, dp = u_pool.shape
    tm = ROW_TILE
    hb = tm // POOL_HALO
    n_hblocks = t // POOL_HALO
    kern = functools.partial(_pool_kernel, tm=tm, n_lat_tiles=n_batch * seq // tm, seq=seq, ctx_len=ctx_len)
    return pl.pallas_call(
        kern,
        grid=(t // tm,),
        in_specs=[
            pl.BlockSpec((POOL_HALO, dp), lambda i: (jnp.maximum(i * hb - 1, 0), 0)),
            pl.BlockSpec((tm, dp), lambda i: (i, 0)),
            pl.BlockSpec((POOL_HALO, dp), lambda i: (jnp.minimum((i + 1) * hb, n_hblocks - 1), 0)),
            _full(w_pool.shape), _full(pool_scale.shape),
        ],
        out_specs=pl.BlockSpec((tm, dp), lambda i: (i, 0)),
        out_shape=jax.ShapeDtypeStruct((t, dp), F32),
        scratch_shapes=[pltpu.VMEM((tm + 2 * POOL_HALO, dp), F32)],
        compiler_params=_cparams(("arbitrary",)),
        name="pool",
    )(u_pool, u_pool, u_pool, w_pool, pool_scale)


def _s5_prep_kernel(are_ref, aim_ref, ls_ref, bbr_ref, bbi_ref, cbr_ref, cbi_ref,
                    ks_ref, bb_ref, cb_ref, we_ref, wc_ref, lt_ref):
    tc = S5_CHUNK
    fwd = pl.program_id(0) == 0
    step = jnp.exp(ls_ref[0, 0])
    ar = are_ref[0, 0]
    ai = aim_ref[0, 0]
    la = ar * step
    th = ai * step

    def powtab(m):
        mg = jnp.exp(m * la)
        an = m * th
        return mg * jnp.cos(an), mg * jnp.sin(an)

    lam_r, lam_i = powtab(1.0)
    den = ar * ar + ai * ai
    nr = lam_r - 1.0
    coef_r = (nr * ar + lam_i * ai) / den
    coef_i = (lam_i * ar - nr * ai) / den
    bp_r, bp_i = _cmul(coef_r, coef_i, bbr_ref[0, 0], bbi_ref[0, 0])
    bb_ref[0, 0] = jnp.concatenate([bp_r, bp_i], axis=-1).astype(BF16)
    cbr = cbr_ref[0, 0]
    cbi = cbi_ref[0, 0]
    cb_ref[0, 0] = jnp.concatenate([cbr, -cbi], axis=0).astype(BF16)
    jcol = lax.broadcasted_iota(jnp.int32, (tc, 1), 0).astype(F32)
    lag_r, lag_i = powtab(jcol)
    cbr_b = cbr.astype(BF16)
    cbi_b = cbi.astype(BF16)
    for m in range(tc):
        bm_r, bm_i = _cmul(bp_r, bp_i, lag_r[m:m + 1, :], lag_i[m:m + 1, :])
        k_m = (jnp.dot(bm_r.astype(BF16), cbr_b, preferred_element_type=F32)
               - jnp.dot(bm_i.astype(BF16), cbi_b, preferred_element_type=F32))
        ks_ref[0, 0, m * LANES:(m + 1) * LANES, :] = k_m.astype(BF16)
    e_r, e_i = powtab(jnp.where(fwd, (tc - 1.0) - jcol, jcol))
    c_r, c_i = powtab(jnp.where(fwd, jcol + 1.0, tc - jcol))
    we_ref[0, 0] = jnp.concatenate([e_r, e_i], axis=-1)
    wc_ref[0, 0] = jnp.concatenate([c_r, c_i], axis=-1)
    t_r, t_i = powtab(float(tc))
    lt_ref[0, 0] = jnp.concatenate([t_r, t_i], axis=-1)


def _s5_prep(a_re, a_im, log_step, b_re, b_im, c_re, c_im):
    _, g, p = a_re.shape
    c = b_re.shape[-1]
    gq = LANES // c
    nq = g // gq
    sq = gq * p
    tc = S5_CHUNK
    eye = jnp.eye(gq, dtype=F32)
    lanes_of = lambda a: a.reshape(2, nq, 1, sq)
    ls = jnp.broadcast_to(log_step[:, :, None], (2, g, p))
    bbd = lambda b: jnp.einsum("dqgpc,gh->dqgchp", b.reshape(2, nq, gq, p, c), eye).reshape(2, nq, gq * c, sq)
    cbd = lambda m: jnp.einsum("dqgcp,gh->dqgphc", m.reshape(2, nq, gq, c, p), eye).reshape(2, nq, sq, gq * c)
    blk = lambda *s: pl.BlockSpec((1, 1) + s, lambda d, q: (d, q, 0, 0))
    shp = lambda s, dt: jax.ShapeDtypeStruct((2, nq) + s, dt)
    return pl.pallas_call(
        _s5_prep_kernel,
        grid=(2, nq),
        in_specs=[blk(1, sq), blk(1, sq), blk(1, sq), blk(LANES, sq), blk(LANES, sq), blk(sq, LANES),
                  blk(sq, LANES)],
        out_specs=[blk(tc * LANES, LANES), blk(LANES, 2 * sq), blk(2 * sq, LANES), blk(tc, 2 * sq),
                   blk(tc, 2 * sq), blk(1, 2 * sq)],
        out_shape=[shp((tc * LANES, LANES), BF16), shp((LANES, 2 * sq), BF16), shp((2 * sq, LANES), BF16),
                   shp((tc, 2 * sq), F32), shp((tc, 2 * sq), F32), shp((1, 2 * sq), F32)],
        compiler_params=_cparams(("arbitrary", "arbitrary")),
        name="s5_prep",
    )(lanes_of(a_re), lanes_of(a_im), lanes_of(ls), bbd(b_re), bbd(b_im), cbd(c_re), cbd(c_im))


def _cmul(ar, ai, br, bi):
    return ar * br - ai * bi, ar * bi + ai * br


def _s5_kernel(*refs, tm, rev, has_prev):
    if has_prev:
        u_ref, prev_ref, ks_ref, bb_ref, cb_ref, we_ref, wc_ref, lt_ref, y_ref, pad_ref, st_ref = refs
    else:
        u_ref, ks_ref, bb_ref, cb_ref, we_ref, wc_ref, lt_ref, y_ref, pad_ref, st_ref = refs
    tc = S5_CHUNK
    nck = tm // tc
    nq = u_ref.shape[1] // LANES
    sq = S5_TILE_STATES

    @pl.when(pl.program_id(1) == 0)
    def _():
        st_ref[...] = jnp.zeros_like(st_ref)

    pad_ref[0:tc, :] = jnp.zeros((tc, LANES), F32)
    pad_ref[tc + tm:, :] = jnp.zeros((tc, LANES), F32)
    jpos = lax.broadcasted_iota(jnp.int32, (tm, 1), 0) % tc
    order = range(nck - 1, -1, -1) if rev else range(nck)
    for q in range(nq):
        lanes = slice(q * LANES, (q + 1) * LANES)
        u = u_ref[:, lanes]
        pad_ref[tc:tc + tm, :] = u
        cols = []
        for m in range(tc):
            if rev:
                sh = pad_ref[tc + m:tc + m + tm, :]
                keep = jpos + m < tc
            else:
                sh = pad_ref[tc - m:tc - m + tm, :]
                keep = jpos >= m
            cols.append(jnp.where(keep, sh, 0.0).astype(BF16))
        y = jnp.dot(jnp.concatenate(cols, axis=-1), ks_ref[0, q], preferred_element_type=F32)

        x = jnp.dot(u.astype(BF16), bb_ref[0, q], preferred_element_type=F32)
        we = jnp.tile(we_ref[0, q], (nck, 1))
        er, ei = _cmul(x[:, :sq], x[:, sq:], we[:, :sq], we[:, sq:])
        er = jnp.sum(er.reshape(nck, tc, sq), axis=1)
        ei = jnp.sum(ei.reshape(nck, tc, sq), axis=1)
        lam_r = lt_ref[0, q][:, :sq]
        lam_i = lt_ref[0, q][:, sq:]
        s_r = st_ref[2 * q:2 * q + 1, :]
        s_i = st_ref[2 * q + 1:2 * q + 2, :]
        ent_r, ent_i = [None] * nck, [None] * nck
        for k in order:
            ent_r[k] = jnp.broadcast_to(s_r, (tc, sq))
            ent_i[k] = jnp.broadcast_to(s_i, (tc, sq))
            n_r, n_i = _cmul(lam_r, lam_i, s_r, s_i)
            s_r = n_r + er[k:k + 1, :]
            s_i = n_i + ei[k:k + 1, :]
        st_ref[2 * q:2 * q + 1, :] = s_r
        st_ref[2 * q + 1:2 * q + 2, :] = s_i
        wc = jnp.tile(wc_ref[0, q], (nck, 1))
        z_r, z_i = _cmul(wc[:, :sq], wc[:, sq:], jnp.concatenate(ent_r, axis=0), jnp.concatenate(ent_i, axis=0))
        y = y + jnp.dot(jnp.concatenate([z_r, z_i], axis=-1).astype(BF16), cb_ref[0, q],
                        preferred_element_type=F32)
        y_ref[:, lanes] = (prev_ref[:, lanes] + y) if has_prev else y


def _s5(u_ssm, prep, *, n_batch, seq, ctx_len):
    t, dw = u_ssm.shape
    tm = ROW_TILE
    nq = dw // LANES
    tc = S5_CHUNK
    sq2 = 2 * S5_TILE_STATES
    nct, nlt = ctx_len // tm, seq // tm
    lat_tiles = n_batch * nlt
    ks, bb, cb, we, wc, lt = prep

    def tile_of(rev):
        def f(b, s):
            cpos = (nct - 1 - s) if rev else s
            lpos = (nlt - 1 - (s - nct)) if rev else (s - nct)
            return jnp.where(s < nct, lat_tiles + b * nct + cpos, b * nlt + lpos)
        return f

    y = None
    for d, rev in enumerate((False, True)):
        tile = tile_of(rev)
        row = pl.BlockSpec((tm, dw), lambda b, s: (tile(b, s), 0))
        par = lambda *shape: pl.BlockSpec((1, nq) + shape, lambda b, s: (d, 0, 0, 0))
        has_prev = y is not None
        kern = functools.partial(_s5_kernel, tm=tm, rev=rev, has_prev=has_prev)
        y = pl.pallas_call(
            kern,
            grid=(n_batch, nct + nlt),
            in_specs=[row] * (2 if has_prev else 1) + [par(tc * LANES, LANES), par(LANES, sq2), par(sq2, LANES),
                                                      par(tc, sq2), par(tc, sq2), par(1, sq2)],
            out_specs=row,
            out_shape=jax.ShapeDtypeStruct((t, dw), F32),
            scratch_shapes=[pltpu.VMEM((tm + 2 * tc, LANES), F32), pltpu.VMEM((2 * nq, S5_TILE_STATES), F32)],
            compiler_params=_cparams(("arbitrary", "arbitrary")),
            name="s5_scan",
        )(*([u_ssm] + ([y] if has_prev else []) + [ks, bb, cb, we, wc, lt]))
    return y


_NT = (((1,), (1,)), ((), ()))


def _attn_lat_kernel(q_ref, kl_ref, vl_ref, kc_ref, vc_ref, o_ref):
    q = q_ref[...]
    tq = q.shape[0]
    seq = kl_ref.shape[0]
    kc = min(ATTN_KEY_CHUNK, seq)
    chunks = [(kl_ref, vl_ref, c * kc, kc) for c in range(seq // kc)] + [(kc_ref, vc_ref, 0, kc_ref.shape[0])]
    m = jnp.full((tq, 1), NEG_BIG, F32)
    l = jnp.zeros((tq, 1), F32)
    acc = jnp.zeros((tq, V_HEAD), F32)
    for k_ref, v_ref, off, n in chunks:
        s = lax.dot_general(q, k_ref[off:off + n, :], _NT, preferred_element_type=F32)
        m_new = jnp.maximum(m, jnp.max(s, axis=-1, keepdims=True))
        alpha = jnp.exp(m - m_new)
        p = jnp.exp(s - m_new)
        l = alpha * l + jnp.sum(p, axis=-1, keepdims=True)
        acc = alpha * acc + jnp.dot(p.astype(BF16), v_ref[off:off + n, :], preferred_element_type=F32)
        m = m_new
    o_ref[...] = acc / l


def _attn_ctx_kernel(q_ref, kc_ref, vc_ref, o_ref):
    s = lax.dot_general(q_ref[...], kc_ref[...], _NT, preferred_element_type=F32)
    p = jnp.exp(s - jnp.max(s, axis=-1, keepdims=True))
    l = jnp.sum(p, axis=-1, keepdims=True)
    o_ref[...] = jnp.dot(p.astype(BF16), vc_ref[...], preferred_element_type=F32) / l


def _attention(q, k, v, *, n_batch, seq, ctx_len, ctx_queries):
    n_heads = v.shape[1] // V_HEAD
    tq = ATTN_Q_TILE
    nq = seq // tq
    cblk = n_batch * seq // ctx_len
    y_lat = pl.pallas_call(
        _attn_lat_kernel,
        grid=(n_batch, n_heads, nq),
        in_specs=[
            pl.BlockSpec((tq, QK_PAD), lambda b, h, i: (b * nq + i, h)),
            pl.BlockSpec((seq, QK_PAD), lambda b, h, i: (b, h)),
            pl.BlockSpec((seq, V_HEAD), lambda b, h, i: (b, h)),
            pl.BlockSpec((ctx_len, QK_PAD), lambda b, h, i: (cblk + b, h)),
            pl.BlockSpec((ctx_len, V_HEAD), lambda b, h, i: (cblk + b, h)),
        ],
        out_specs=pl.BlockSpec((tq, V_HEAD), lambda b, h, i: (b * nq + i, h)),
        out_shape=jax.ShapeDtypeStruct((n_batch * seq, n_heads * V_HEAD), F32),
        compiler_params=_cparams(("arbitrary", "arbitrary", "arbitrary")),
        name="attn_latent",
    )(q, k, v, k, v)
    if not ctx_queries:
        return y_lat, None
    y_ctx = pl.pallas_call(
        _attn_ctx_kernel,
        grid=(n_batch, n_heads),
        in_specs=[
            pl.BlockSpec((ctx_len, QK_PAD), lambda b, h: (cblk + b, h)),
            pl.BlockSpec((ctx_len, QK_PAD), lambda b, h: (cblk + b, h)),
            pl.BlockSpec((ctx_len, V_HEAD), lambda b, h: (cblk + b, h)),
        ],
        out_specs=pl.BlockSpec((ctx_len, V_HEAD), lambda b, h: (b, h)),
        out_shape=jax.ShapeDtypeStruct((n_batch * ctx_len, n_heads * V_HEAD), F32),
        compiler_params=_cparams(("arbitrary", "arbitrary")),
        name="attn_context",
    )(q, k, v)
    return y_lat, y_ctx


def _merge_kernel(x_ref, yp_ref, ys_ref, us_ref, yml_ref, *rest, tiles_per_seq, n_batch, d_model, has_ctx):
    if has_ctx:
        ymc_ref, rest = rest[0], rest[1:]
    (mod_ref, sd_ref, wglu_ref, bglu_ref, on_ref, wout_ref, nffn_ref, wr_ref, br_ref,
     x1_ref, h2_ref, idx_ref, gate_ref, rank_ref, cnt_ref, carry_ref) = rest
    d = d_model
    d4 = d // 4
    i = pl.program_id(0)
    tm = x_ref.shape[0]
    r = jnp.minimum(i // tiles_per_seq, n_batch)
    g1 = mod_ref[pl.ds(r, 1), pl.ds(2 * d, d)]
    sh2 = mod_ref[pl.ds(r, 1), pl.ds(3 * d, d)]
    sc2 = mod_ref[pl.ds(r, 1), pl.ds(4 * d, d)]

    z = ys_ref[...] + sd_ref[...] * us_ref[...]
    g = jax.nn.gelu(z, approximate=True)
    glu = jnp.dot(g.astype(BF16), wglu_ref[...], preferred_element_type=F32) + bglu_ref[...]
    y_ssm = g * jax.nn.sigmoid(glu)
    on = on_ref[...]
    y_mla = yml_ref[...]
    if has_ctx:
        y_mla = jnp.where(i < n_batch * tiles_per_seq, y_mla, ymc_ref[...])
    cat = jnp.concatenate([_rms(yp_ref[...], on[:, 0:d4]), _rms(y_ssm, on[:, d4:2 * d4]),
                           _rms(y_mla, on[:, 2 * d4:])], axis=-1).astype(BF16)
    x1 = x_ref[...] + g1 * jnp.dot(cat, wout_ref[...], preferred_element_type=F32)
    x1_ref[...] = x1
    h2 = (_rms(x1, nffn_ref[...]) * (1.0 + sc2) + sh2).astype(BF16)
    h2_bits = lax.bitcast_convert_type(h2.astype(F32), jnp.uint32)
    h2_ref[...] = (h2_bits[:, :d // 2] >> 16) | (h2_bits[:, d // 2:] & jnp.uint32(0xFFFF0000))

    logits = jnp.dot(h2, wr_ref[...], preferred_element_type=F32) + br_ref[...]
    lane = lax.broadcasted_iota(jnp.int32, (tm, LANES), 1)
    vals, hots = [], []
    idx_out = jnp.zeros((tm, LANES), jnp.int32)
    for kk in range(TOP_K):
        mx = jnp.max(logits, axis=-1, keepdims=True)
        sel = jnp.min(jnp.where(logits == mx, lane, LANES), axis=-1, keepdims=True)
        hot = lane == sel
        vals.append(mx)
        hots.append(hot)
        idx_out = jnp.where(lane == kk, sel, idx_out)
        logits = jnp.where(hot, NEG_BIG * 2.0, logits)
    es = [jnp.exp(vv - vals[0]) for vv in vals]
    den = es[0] + es[1] + es[2] + es[3]
    gate_out = jnp.zeros((tm, LANES), F32)
    for kk in range(TOP_K):
        gate_out = jnp.where(lane == kk, es[kk] / den, gate_out)
    idx_ref[...] = idx_out
    gate_ref[...] = gate_out

    @pl.when(i == 0)
    def _():
        carry_ref[...] = jnp.zeros_like(carry_ref)

    onehot = (hots[0] | hots[1] | hots[2] | hots[3]).astype(BF16)
    rr = lax.broadcasted_iota(jnp.int32, (tm, tm), 0)
    cc = lax.broadcasted_iota(jnp.int32, (tm, tm), 1)
    tri = (cc < rr).astype(BF16)
    pos = jnp.dot(tri, onehot, preferred_element_type=F32) + carry_ref[...]
    rank_out = jnp.zeros((tm, LANES), jnp.int32)
    for kk in range(TOP_K):
        rk = jnp.sum(jnp.where(hots[kk], pos, 0.0), axis=-1, keepdims=True).astype(jnp.int32)
        rank_out = jnp.where(lane == kk, rk, rank_out)
    rank_ref[...] = rank_out
    carry = carry_ref[...] + jnp.sum(onehot.astype(F32), axis=0, keepdims=True)
    carry_ref[...] = carry
    cnt_ref[...] = carry.astype(jnp.int32)


def _merge(xt, y_pool, y_s5, u_ssm, y_mla_lat, y_mla_ctx, mods, ssm_d, wglu, bglu, out_norm, wout, nffn, wr, br, *,
           n_rows, n_batch, seq):
    d = xt.shape[1]
    tm = ROW_TILE
    n_lat_tiles = n_batch * seq // tm
    has_ctx = y_mla_ctx is not None
    row = lambda w: pl.BlockSpec((tm, w), lambda i: (i, 0))
    kern = functools.partial(_merge_kernel, tiles_per_seq=seq // tm, n_batch=n_batch, d_model=d, has_ctx=has_ctx)
    small = [mods, ssm_d, wglu, bglu, out_norm, wout, nffn, wr, br]
    y_mla = [y_mla_lat] + ([y_mla_ctx] if has_ctx else [])
    y_specs = [pl.BlockSpec((tm, d // 2), lambda i: (jnp.minimum(i, n_lat_tiles - 1), 0))]
    if has_ctx:
        y_specs.append(pl.BlockSpec((tm, d // 2), lambda i: (jnp.maximum(i - n_lat_tiles, 0), 0)))
    return pl.pallas_call(
        kern,
        grid=(n_rows // tm,),
        in_specs=[row(d), row(d // 4), row(d // 4), row(d // 4)] + y_specs + [_full(a.shape) for a in small],
        out_specs=[row(d), row(d // 2), row(LANES), row(LANES), row(LANES), _full((1, LANES))],
        out_shape=[jax.ShapeDtypeStruct((n_rows, d), F32), jax.ShapeDtypeStruct((n_rows, d // 2), jnp.uint32),
                   jax.ShapeDtypeStruct((n_rows, LANES), jnp.int32), jax.ShapeDtypeStruct((n_rows, LANES), F32),
                   jax.ShapeDtypeStruct((n_rows, LANES), jnp.int32), jax.ShapeDtypeStruct((1, LANES), jnp.int32)],
        scratch_shapes=[pltpu.VMEM((1, LANES), F32)],
        compiler_params=_cparams(("arbitrary",)),
        name="merge_router",
    )(xt, y_pool, y_s5, u_ssm, *y_mla, *small)


def _row_copy(src, dst, sem, src_row, dst_row):
    return pltpu.make_async_copy(src.at[pl.ds(src_row, 1), :], dst.at[pl.ds(dst_row, 1), :], sem)


def _dispatch_kernel(dest_hbm, h2_ref, zeros_hbm, xs_hbm, dest_smem, sem_idx, sem_rows, *, tm):
    del zeros_hbm
    i = pl.program_id(0)
    n = tm * TOP_K
    idx_copy = pltpu.make_async_copy(dest_hbm.at[i], dest_smem, sem_idx)
    idx_copy.start()
    idx_copy.wait()

    def issue(j, _):
        _row_copy(h2_ref, xs_hbm, sem_rows, j // TOP_K, dest_smem[j]).start()
        return 0

    lax.fori_loop(0, n, issue, 0, unroll=8)

    def drain(j, _):
        _row_copy(h2_ref, xs_hbm, sem_rows, j // TOP_K, dest_smem[j]).wait()
        return 0

    lax.fori_loop(0, n, drain, 0, unroll=8)


def _dispatch(dest_tiles, h2_packed, n_sorted_rows):
    n_tok, w = h2_packed.shape
    tm = ROW_TILE
    zeros = jnp.zeros((n_sorted_rows, w), h2_packed.dtype)
    return pl.pallas_call(
        functools.partial(_dispatch_kernel, tm=tm),
        grid=(n_tok // tm,),
        in_specs=[pl.BlockSpec(memory_space=pl.ANY), pl.BlockSpec((tm, w), lambda i: (i, 0)),
                  pl.BlockSpec(memory_space=pl.ANY)],
        out_specs=pl.BlockSpec(memory_space=pl.ANY),
        out_shape=jax.ShapeDtypeStruct(zeros.shape, zeros.dtype),
        input_output_aliases={2: 0},
        scratch_shapes=[pltpu.SMEM((tm * TOP_K,), jnp.int32), pltpu.SemaphoreType.DMA, pltpu.SemaphoreType.DMA],
        compiler_params=_cparams(("arbitrary",)),
        name="moe_dispatch",
    )(dest_tiles, h2_packed, zeros)


def _expert_kernel(ie_ref, ib_ref, ns_ref, x_ref, wg_ref, bg_ref, wu_ref, bu_ref, wd_ref, bd_ref, o_ref, act_ref,
                   xb_ref, *, nf):
    it = pl.program_id(0)
    j = pl.program_id(1)
    ns = ns_ref[it]
    rows_max = x_ref.shape[0]
    half = x_ref.shape[1]

    @pl.when((ns > 0) & (j == 0))
    def _():
        words = x_ref[...]
        xb_ref[:, 0:half] = lax.bitcast_convert_type(words << 16, F32).astype(BF16)
        xb_ref[:, half:] = lax.bitcast_convert_type(words & jnp.uint32(0xFFFF0000), F32).astype(BF16)

    for m in range(1, rows_max // EXPERT_SUB_ROWS + 1):
        mr = m * EXPERT_SUB_ROWS

        @pl.when((ns == m) & (j < nf))
        def _():
            x = xb_ref[0:mr, :]
            g = jnp.dot(x, wg_ref[0, 0].astype(BF16), preferred_element_type=F32) + bg_ref[0, 0]
            u = jnp.dot(x, wu_ref[0, 0].astype(BF16), preferred_element_type=F32) + bu_ref[0, 0]
            g = jnp.minimum(g, SWIGLU_LIMIT)
            u = jnp.clip(u, -SWIGLU_LIMIT, SWIGLU_LIMIT)
            act_ref[j, 0:mr, :] = (g * jax.nn.sigmoid(SWIGLU_ALPHA * g) * (u + 1.0)).astype(BF16)

        @pl.when((ns == m) & (j >= nf))
        def _():
            act = jnp.concatenate([act_ref[f, 0:mr, :] for f in range(nf)], axis=-1)
            o_ref[0:mr, :] = jnp.dot(act, wd_ref[0, 0].astype(BF16), preferred_element_type=F32) + bd_ref[0, 0]
            if mr < rows_max:
                o_ref[mr:, :] = jnp.zeros((rows_max - mr, o_ref.shape[1]), F32)

    @pl.when((ns == 0) & (j >= nf))
    def _():
        o_ref[...] = jnp.zeros(o_ref.shape, F32)


def _experts(layer, item_e, item_blk, item_nsub, xs, w_gate, b_gate, w_up, b_up, w_down, b_down):
    n_items = item_e.shape[0]
    n_layers, n_exp, d, f_dim = w_gate.shape
    tf = EXPERT_TILE
    nf = f_dim // tf
    rows = EXPERT_ITEM_ROWS
    assert d // tf == nf

    def fi(j, ns, it):
        return jnp.where(ns[it] > 0, jnp.minimum(j, nf - 1), nf - 1)

    def ni(j, ns, it):
        return jnp.where(ns[it] > 0, jnp.maximum(j - nf, 0), nf - 1)

    return pl.pallas_call(
        functools.partial(_expert_kernel, nf=nf),
        grid_spec=pltpu.PrefetchScalarGridSpec(
            num_scalar_prefetch=3,
            grid=(n_items, 2 * nf),
            in_specs=[
                pl.BlockSpec((rows, d // 2), lambda it, j, ie, ib, ns: (ib[it], 0), pipeline_mode=pl.Buffered(1)),
                pl.BlockSpec((1, 1, d, tf), lambda it, j, ie, ib, ns: (layer, ie[it], 0, fi(j, ns, it))),
                pl.BlockSpec((1, 1, 1, tf), lambda it, j, ie, ib, ns: (layer, ie[it], 0, fi(j, ns, it))),
                pl.BlockSpec((1, 1, d, tf), lambda it, j, ie, ib, ns: (layer, ie[it], 0, fi(j, ns, it))),
                pl.BlockSpec((1, 1, 1, tf), lambda it, j, ie, ib, ns: (layer, ie[it], 0, fi(j, ns, it))),
                pl.BlockSpec((1, 1, f_dim, tf), lambda it, j, ie, ib, ns: (layer, ie[it], 0, ni(j, ns, it))),
                pl.BlockSpec((1, 1, 1, tf), lambda it, j, ie, ib, ns: (layer, ie[it], 0, ni(j, ns, it))),
            ],
            out_specs=pl.BlockSpec((rows, tf), lambda it, j, ie, ib, ns: (it, jnp.maximum(j - nf, 0))),
            scratch_shapes=[pltpu.VMEM((nf, rows, tf), BF16), pltpu.VMEM((rows, d), BF16)],
        ),
        out_shape=jax.ShapeDtypeStruct((n_items * rows, d), F32),
        compiler_params=_cparams(("arbitrary", "arbitrary"), EXPERT_VMEM_LIMIT),
        name="moe_experts",
    )(item_e, item_blk, item_nsub, xs, w_gate, b_gate.reshape(n_layers, n_exp, 1, f_dim), w_up,
      b_up.reshape(n_layers, n_exp, 1, f_dim), w_down, b_down.reshape(n_layers, n_exp, 1, d))


def _combine_kernel(dest_hbm, os_hbm, gate_ref, x_ref, mod_ref, o_ref, dest_smem, buf, sem_idx, sem_rows,
                    *, tc, n_tiles, tiles_per_seq, n_batch):
    i = pl.program_id(0)
    n = tc * TOP_K
    slot = i % 2
    d = x_ref.shape[1]

    def gather(s, j):
        return _row_copy(os_hbm, buf.at[s, j % TOP_K], sem_rows.at[s], dest_smem[s, j], j // TOP_K)

    def fetch(tile, s):
        idx_copy = pltpu.make_async_copy(dest_hbm.at[tile], dest_smem.at[s], sem_idx)
        idx_copy.start()
        idx_copy.wait()

        def issue(j, _):
            gather(s, j).start()
            return 0

        lax.fori_loop(0, n, issue, 0, unroll=8)

    @pl.when(i == 0)
    def _():
        fetch(0, 0)

    @pl.when(i + 1 < n_tiles)
    def _():
        fetch(i + 1, 1 - slot)

    def drain(j, _):
        gather(slot, j).wait()
        return 0

    lax.fori_loop(0, n, drain, 0, unroll=8)
    g2 = mod_ref[pl.ds(jnp.minimum(i // tiles_per_seq, n_batch), 1), pl.ds(5 * d, d)]
    gates = gate_ref[...]
    acc = gates[:, 0:1] * buf[slot, 0]
    for kk in range(1, TOP_K):
        acc = acc + gates[:, kk:kk + 1] * buf[slot, kk]
    o_ref[...] = x_ref[...] + g2 * acc


def _combine(dest_tiles, out_sorted, gates, x1, mods, *, n_batch, seq):
    n_tok, d = x1.shape
    tc = COMBINE_TILE
    kern = functools.partial(_combine_kernel, tc=tc, n_tiles=n_tok // tc, tiles_per_seq=seq // tc, n_batch=n_batch)
    return pl.pallas_call(
        kern,
        grid=(n_tok // tc,),
        in_specs=[pl.BlockSpec(memory_space=pl.ANY)] * 2 + [
            pl.BlockSpec((tc, LANES), lambda i: (i, 0)), pl.BlockSpec((tc, d), lambda i: (i, 0)),
            _full(mods.shape)],
        out_specs=pl.BlockSpec((tc, d), lambda i: (i, 0)),
        out_shape=jax.ShapeDtypeStruct(x1.shape, F32),
        scratch_shapes=[pltpu.SMEM((2, tc * TOP_K), jnp.int32), pltpu.VMEM((2, TOP_K, tc, d), F32),
                        pltpu.SemaphoreType.DMA, pltpu.SemaphoreType.DMA((2,))],
        compiler_params=_cparams(("arbitrary",)),
        name="moe_combine",
    )(dest_tiles, out_sorted, gates, x1, mods)


def _rope_tables(n_batch, seq, ctx_len):
    rows = seq // GRID_W
    row = jnp.broadcast_to(jnp.arange(rows)[:, None], (rows, GRID_W)).reshape(-1).astype(F32)
    col = jnp.broadcast_to(jnp.arange(GRID_W)[None, :], (rows, GRID_W)).reshape(-1).astype(F32)
    half = QK_ROPE // 2
    inv_freq = ROPE_BASE ** (-jnp.arange(0, half, 2, dtype=F32) / half)
    ang = jnp.concatenate([row[:, None] * inv_freq, col[:, None] * inv_freq], axis=-1)
    cos, sin = jnp.cos(ang), jnp.sin(ang)
    zpad = jnp.zeros((seq, LANES - QK_ROPE), F32)
    cos_l = jnp.concatenate([cos, cos, zpad], axis=-1)
    sin_l = jnp.concatenate([-sin, sin, zpad], axis=-1)
    n_ctx = n_batch * ctx_len
    cos_c = jnp.concatenate([jnp.ones((n_ctx, QK_ROPE), F32), jnp.zeros((n_ctx, LANES - QK_ROPE), F32)], axis=-1)
    cos_t = jnp.concatenate([jnp.tile(cos_l, (n_batch, 1)), cos_c], axis=0)
    sin_t = jnp.concatenate([jnp.tile(sin_l, (n_batch, 1)), jnp.zeros((n_ctx, LANES), F32)], axis=0)
    return cos_t, sin_t


def _pad_head_vec(vec):
    return jnp.concatenate([vec, jnp.zeros((QK_PAD - QK_HEAD,), vec.dtype)])[None, :]


def _routing(idx, rank, counts, n_tok, n_exp):
    rows = EXPERT_ITEM_ROWS
    n_items = -(-(n_tok * TOP_K) // rows) + n_exp
    cnt = counts[0, :n_exp]
    blocks = (cnt + rows - 1) // rows
    blk_end = jnp.cumsum(blocks)
    blk_start = blk_end - blocks
    dest = blk_start[idx] * rows + rank
    total = blk_end[-1]
    it = jnp.arange(n_items, dtype=jnp.int32)
    it_c = jnp.minimum(it, total - 1)
    e_of = jnp.minimum(jnp.searchsorted(blk_end, it_c, side="right"), n_exp - 1).astype(jnp.int32)
    left = cnt[e_of] - (it_c - blk_start[e_of]) * rows
    nsub = jnp.clip((left + EXPERT_SUB_ROWS - 1) // EXPERT_SUB_ROWS, 0, rows // EXPERT_SUB_ROWS)
    nsub = jnp.where(it < total, nsub, 0).astype(jnp.int32)
    return dest.astype(jnp.int32), e_of, it_c.astype(jnp.int32), nsub, n_items


def _layer(layer, xt, mods, cos_t, sin_t, p, expert_params, *, n_batch, seq, ctx_len, ctx_out):
    t, d = xt.shape
    d4 = d // 4
    n_heads = (d - 2 * d4) // V_HEAD
    kv_lora = d // 8
    bf = lambda a: a.astype(BF16)
    row = lambda a: a[None, :]

    win = bf(jnp.pad(p["w_in"], ((0, 0), (0, LANES - QK_ROPE))))
    wuq = p["w_uq"].reshape(-1, n_heads, QK_HEAD)
    wuq = bf(jnp.pad(wuq, ((0, 0), (0, 0), (0, QK_PAD - QK_HEAD))).reshape(-1, n_heads * QK_PAD))
    wukv = p["w_ukv"].reshape(kv_lora, n_heads, QK_NOPE + V_HEAD)
    wk = bf(wukv[:, :, :QK_NOPE].reshape(kv_lora, n_heads * QK_NOPE))
    wv = bf(wukv[:, :, QK_NOPE:].reshape(kv_lora, n_heads * V_HEAD))
    qn = _pad_head_vec(p["q_norm"]) * (QK_HEAD ** -0.5)
    kn = _pad_head_vec(p["k_norm"])

    u_pool, u_ssm, q, k, v = _proj(xt, mods, cos_t, sin_t, row(p["norm_mix"]), win, row(p["q_a_norm"]), wuq,
                                   row(p["kv_a_norm"]), wk, wv, qn, kn, n_batch=n_batch, seq=seq)

    y_pool = _pool(u_pool, bf(p["w_pool"]), row(p["pool_scale"]), n_batch=n_batch, seq=seq, ctx_len=ctx_len)

    prep = _s5_prep(p["ssm_a_re"], p["ssm_a_im"], p["ssm_log_step"], p["ssm_b_re"], p["ssm_b_im"],
                    p["ssm_c_re"], p["ssm_c_im"])
    y_s5 = _s5(u_ssm, prep, n_batch=n_batch, seq=seq, ctx_len=ctx_len)

    y_mla_lat, y_mla_ctx = _attention(q, k, v, n_batch=n_batch, seq=seq, ctx_len=ctx_len, ctx_queries=ctx_out)

    n_rows = t if ctx_out else n_batch * seq
    n_exp = p["w_router"].shape[1]
    wr = bf(jnp.pad(p["w_router"], ((0, 0), (0, LANES - n_exp))))
    br = jnp.concatenate([p["b_router"], jnp.full((LANES - n_exp,), NEG_BIG, F32)])[None, :]
    x1, h2, idx, gates, rank, counts = _merge(
        xt, y_pool, y_s5, u_ssm, y_mla_lat, y_mla_ctx, mods, row(p["ssm_d"]), bf(p["w_glu"]), row(p["b_glu"]),
        row(p["out_norm"]), bf(p["w_out"]), row(p["norm_ffn"]), wr, br, n_rows=n_rows, n_batch=n_batch, seq=seq)

    dest, item_e, item_blk, item_nsub, n_items = _routing(idx[:, :TOP_K], rank[:, :TOP_K], counts, n_rows, n_exp)
    xs = _dispatch(dest.reshape(n_rows // ROW_TILE, ROW_TILE * TOP_K), h2, n_items * EXPERT_ITEM_ROWS)
    out_sorted = _experts(layer, item_e, item_blk, item_nsub, xs, *expert_params)
    return _combine(dest.reshape(n_rows // COMBINE_TILE, COMBINE_TILE * TOP_K), out_sorted, gates, x1, mods,
                    n_batch=n_batch, seq=seq)


_PARAM_NAMES = ("norm_mix", "norm_ffn", "w_in", "w_pool", "pool_scale", "ssm_a_re", "ssm_a_im", "ssm_log_step",
                "ssm_b_re", "ssm_b_im", "ssm_c_re", "ssm_c_im", "ssm_d", "w_glu", "b_glu", "q_a_norm", "w_uq",
                "kv_a_norm", "w_ukv", "q_norm", "k_norm", "out_norm", "w_out", "w_router", "b_router")


def kernel(x, c, ctx, c_ctx, w_ada, b_ada, norm_mix, norm_ffn, w_in, w_pool, pool_scale, ssm_a_re, ssm_a_im, ssm_log_step, ssm_b_re, ssm_b_im, ssm_c_re, ssm_c_im, ssm_d, w_glu, b_glu, q_a_norm, w_uq, kv_a_norm, w_ukv, q_norm, k_norm, out_norm, w_out, w_router, b_router, w_gate, b_gate, w_up, b_up, w_down, b_down):
    n_batch, seq, d = x.shape
    ctx_len = ctx.shape[1]
    depth = w_ada.shape[0]
    assert seq % ROW_TILE == 0 and ctx_len % ROW_TILE == 0 and seq % GRID_W == 0 and n_batch + 1 <= 8
    stacked = dict(zip(_PARAM_NAMES, (norm_mix, norm_ffn, w_in, w_pool, pool_scale, ssm_a_re, ssm_a_im,
                                      ssm_log_step, ssm_b_re, ssm_b_im, ssm_c_re, ssm_c_im, ssm_d, w_glu, b_glu,
                                      q_a_norm, w_uq, kv_a_norm, w_ukv, q_norm, k_norm, out_norm, w_out, w_router,
                                      b_router)))
    expert_params = (w_gate, b_gate, w_up, b_up, w_down, b_down)
    cvecs = jnp.concatenate([c, c_ctx[None, :], jnp.zeros((8 - n_batch - 1, d), F32)], axis=0)
    mods = _ada(cvecs, w_ada, b_ada)
    cos_t, sin_t = _rope_tables(n_batch, seq, ctx_len)
    n_lat = n_batch * seq
    xt = jnp.concatenate([x.reshape(n_lat, d), ctx.reshape(n_batch * ctx_len, d)], axis=0)
    for layer in range(depth):
        p = {name: val[layer] for name, val in stacked.items()}
        xt = _layer(layer, xt, mods[layer], cos_t, sin_t, p, expert_params, n_batch=n_batch, seq=seq,
                    ctx_len=ctx_len, ctx_out=layer < depth - 1)
    return xt[:n_lat].reshape(n_batch, seq, d)
```

```python
import functools
import math

import jax
import jax.numpy as jnp
from jax import lax
from jax.experimental import pallas as pl
from jax.experimental.pallas import tpu as pltpu

F32 = jnp.float32
BF16 = jnp.bfloat16

EPS = 1e-6
GRID_W = 64
POOL_WINDOWS = (2, 4, 8, 16)
POOL_HALO = 8
SSM_GROUP = 16
SSM_STATE = 64
S5_CHUNK = 16
S5_TILE_STATES = (128 // SSM_GROUP) * SSM_STATE
QK_NOPE = 128
QK_ROPE = 64
QK_HEAD = QK_NOPE + QK_ROPE
QK_PAD = 256
V_HEAD = 128
ROPE_BASE = 10000.0
TOP_K = 4
SWIGLU_ALPHA = 1.702
SWIGLU_LIMIT = 7.0

LANES = 128
ROW_TILE = 256
ATTN_Q_TILE = 512
ATTN_KEY_CHUNK = 1024
EXPERT_ITEM_ROWS = 1280
EXPERT_SUB_ROWS = 256
EXPERT_TILE = 512
COMBINE_TILE = 128
VMEM_LIMIT = 52 * 1024 * 1024
EXPERT_VMEM_LIMIT = 57 * 1024 * 1024
NEG_BIG = -1e30


def _cparams(sem, vmem=VMEM_LIMIT):
    return pltpu.CompilerParams(dimension_semantics=sem, vmem_limit_bytes=vmem)


def _rms(x, w):
    return x * lax.rsqrt(jnp.mean(x * x, axis=-1, keepdims=True) + EPS) * w


def _full(shape):
    nd = len(shape)
    return pl.BlockSpec(shape, lambda *_: (0,) * nd)


def _ada_kernel(cv_ref, w_ref, b_ref, o_ref):
    cv = cv_ref[...]
    s = (cv * jax.nn.sigmoid(cv)).astype(BF16)
    o_ref[0] = jnp.dot(s, w_ref[0].astype(BF16), preferred_element_type=F32) + b_ref[0]


def _ada(cvecs, w_ada, b_ada):
    n_layers, d, n6 = w_ada.shape
    tn = 512
    return pl.pallas_call(
        _ada_kernel,
        grid=(n_layers, n6 // tn),
        in_specs=[
            pl.BlockSpec((8, d), lambda l, j: (0, 0)),
            pl.BlockSpec((1, d, tn), lambda l, j: (l, 0, j)),
            pl.BlockSpec((1, 1, tn), lambda l, j: (l, 0, j)),
        ],
        out_specs=pl.BlockSpec((1, 8, tn), lambda l, j: (l, 0, j)),
        out_shape=jax.ShapeDtypeStruct((n_layers, 8, n6), F32),
        compiler_params=_cparams(("arbitrary", "arbitrary")),
        name="ada",
    )(cvecs, w_ada, b_ada.reshape(n_layers, 1, n6))


def _rope128(r, cos, sin):
    lane = lax.broadcasted_iota(jnp.int32, r.shape, 1)
    sw = jnp.where((lane % QK_ROPE) < QK_ROPE // 2, pltpu.roll(r, LANES - 32, 1), pltpu.roll(r, 32, 1))
    return r * cos + sw * sin


def _proj_kernel(x_ref, mod_ref, cos_ref, sin_ref, nmix_ref, win_ref, qan_ref, wuq_ref, kvan_ref,
                 wk_ref, wv_ref, qn_ref, kn_ref,
                 upool_ref, ussm_ref, q_ref, k_ref, v_ref, *, tiles_per_seq, n_batch, n_heads, d_model):
    d = d_model
    r = jnp.minimum(pl.program_id(0) // tiles_per_seq, n_batch)
    sh = mod_ref[pl.ds(r, 1), pl.ds(0, d)]
    sc = mod_ref[pl.ds(r, 1), pl.ds(d, d)]
    h = _rms(x_ref[...], nmix_ref[...]) * (1.0 + sc) + sh
    proj = jnp.dot(h.astype(BF16), win_ref[...], preferred_element_type=F32)
    d4 = d // 4
    upool_ref[...] = proj[:, 0:d4]
    ussm_ref[...] = proj[:, 2 * d4:3 * d4]
    cos = cos_ref[...]
    sin = sin_ref[...]

    qa = _rms(proj[:, d4:2 * d4], qan_ref[...]).astype(BF16)
    qf = jnp.dot(qa, wuq_ref[...], preferred_element_type=F32)
    qn_w = qn_ref[...]
    for hd in range(n_heads):
        blk = qf[:, hd * QK_PAD:(hd + 1) * QK_PAD]
        ss = jnp.sum(blk * blk, axis=-1, keepdims=True) * (1.0 / QK_HEAD)
        qn = blk * lax.rsqrt(ss + EPS) * qn_w
        q_ref[:, hd * QK_PAD:hd * QK_PAD + LANES] = qn[:, :LANES].astype(BF16)
        q_ref[:, hd * QK_PAD + LANES:(hd + 1) * QK_PAD] = _rope128(qn[:, LANES:], cos, sin).astype(BF16)

    kv_lo = 3 * d4
    kv_w = d // 8
    ka = _rms(proj[:, kv_lo:kv_lo + kv_w], kvan_ref[...]).astype(BF16)
    kn_all = jnp.dot(ka, wk_ref[...], preferred_element_type=F32)
    v_ref[...] = jnp.dot(ka, wv_ref[...], preferred_element_type=F32).astype(BF16)
    krp = proj[:, kv_lo + kv_w:kv_lo + kv_w + LANES]
    kr_ss = jnp.sum(krp * krp, axis=-1, keepdims=True)
    kn_w = kn_ref[...]
    for hd in range(n_heads):
        kn = kn_all[:, hd * QK_NOPE:(hd + 1) * QK_NOPE]
        ss = (jnp.sum(kn * kn, axis=-1, keepdims=True) + kr_ss) * (1.0 / QK_HEAD)
        rinv = lax.rsqrt(ss + EPS)
        k_ref[:, hd * QK_PAD:hd * QK_PAD + LANES] = (kn * rinv * kn_w[:, :LANES]).astype(BF16)
        k_ref[:, hd * QK_PAD + LANES:(hd + 1) * QK_PAD] = _rope128(
            krp * rinv * kn_w[:, LANES:], cos, sin).astype(BF16)


def _proj(xt, mods, cos_t, sin_t, nmix, win, qan, wuq, kvan, wk, wv, qn, kn, *, n_batch, seq):
    t, d = xt.shape
    tm = ROW_TILE
    n_heads = wv.shape[1] // V_HEAD
    row = lambda w: pl.BlockSpec((tm, w), lambda i: (i, 0))
    kern = functools.partial(_proj_kernel, tiles_per_seq=seq // tm, n_batch=n_batch,
                             n_heads=n_heads, d_model=d)
    return pl.pallas_call(
        kern,
        grid=(t // tm,),
        in_specs=[row(d), _full(mods.shape), row(LANES), row(LANES), _full(nmix.shape), _full(win.shape),
                  _full(qan.shape), _full(wuq.shape), _full(kvan.shape), _full(wk.shape), _full(wv.shape),
                  _full(qn.shape), _full(kn.shape)],
        out_specs=[row(d // 4), row(d // 4), row(n_heads * QK_PAD), row(n_heads * QK_PAD),
                   row(n_heads * V_HEAD)],
        out_shape=[jax.ShapeDtypeStruct((t, d // 4), F32), jax.ShapeDtypeStruct((t, d // 4), F32),
                   jax.ShapeDtypeStruct((t, n_heads * QK_PAD), BF16),
                   jax.ShapeDtypeStruct((t, n_heads * QK_PAD), BF16),
                   jax.ShapeDtypeStruct((t, n_heads * V_HEAD), BF16)],
        compiler_params=_cparams(("arbitrary",)),
        name="mixer_in",
    )(xt, mods, cos_t, sin_t, nmix, win, qan, wuq, kvan, wk, wv, qn, kn)


def _pool_kernel(prev_ref, cur_ref, next_ref, w_ref, scale_ref, o_ref, pad_ref, *,
                 tm, n_lat_tiles, seq, ctx_len):
    i = pl.program_id(0)
    hw = POOL_HALO
    pad_ref[0:hw, :] = prev_ref[...]
    pad_ref[hw:hw + tm, :] = cur_ref[...]
    pad_ref[hw + tm:2 * hw + tm, :] = next_ref[...]
    is_lat = i < n_lat_tiles
    row0 = i * tm
    s0 = jnp.where(is_lat, (row0 // seq) * seq,
                   n_lat_tiles * tm + ((row0 - n_lat_tiles * tm) // ctx_len) * ctx_len)
    slen = jnp.where(is_lat, seq, ctx_len)
    t = row0 - s0 + lax.broadcasted_iota(jnp.int32, (tm, 1), 0)
    for g, w in enumerate(POOL_WINDOWS):
        lanes = slice(g * LANES, (g + 1) * LANES)
        acc = jnp.zeros((tm, LANES), F32)
        for kk in range(-(w // 2), w - w // 2):
            valid = (t + kk >= 0) & (t + kk < slen)
            acc = acc + jnp.where(valid, pad_ref[hw + kk:hw + kk + tm, lanes], 0.0)
        lo = jnp.maximum(t - w // 2, 0)
        hi = jnp.minimum(t + (w - w // 2), slen)
        mean = acc / (hi - lo).astype(F32)
        dlt = (mean - cur_ref[:, lanes]).astype(BF16)
        o_ref[:, lanes] = jnp.dot(dlt, w_ref[g], preferred_element_type=F32) * scale_ref[:, lanes]


def _pool(u_pool, w_pool, pool_scale, *, n_batch, seq, ctx_len):
    t, dp = u_pool.shape
    tm = ROW_TILE
    hb = tm // POOL_HALO
    n_hblocks = t // POOL_HALO
    kern = functools.partial(_pool_kernel, tm=tm, n_lat_tiles=n_batch * seq // tm, seq=seq, ctx_len=ctx_len)
    return pl.pallas_call(
        kern,
        grid=(t // tm,),
        in_specs=[
            pl.BlockSpec((POOL_HALO, dp), lambda i: (jnp.maximum(i * hb - 1, 0), 0)),
            pl.BlockSpec((tm, dp), lambda i: (i, 0)),
            pl.BlockSpec((POOL_HALO, dp), lambda i: (jnp.minimum((i + 1) * hb, n_hblocks - 1), 0)),
            _full(w_pool.shape), _full(pool_scale.shape),
        ],
        out_specs=pl.BlockSpec((tm, dp), lambda i: (i, 0)),
        out_shape=jax.ShapeDtypeStruct((t, dp), F32),
        scratch_shapes=[pltpu.VMEM((tm + 2 * POOL_HALO, dp), F32)],
        compiler_params=_cparams(("arbitrary",)),
        name="pool",
    )(u_pool, u_pool, u_pool, w_pool, pool_scale)


def _s5_prep_kernel(are_ref, aim_ref, ls_ref, bbr_ref, bbi_ref, cbr_ref, cbi_ref,
                    ks_ref, bb_ref, cb_ref, we_ref, wc_ref, lt_ref):
    tc = S5_CHUNK
    fwd = pl.program_id(0) == 0
    step = jnp.exp(ls_ref[0, 0])
    ar = are_ref[0, 0]
    ai = aim_ref[0, 0]
    la = ar * step
    th = ai * step

    def powtab(m):
        mg = jnp.exp(m * la)
        an = m * th
        return mg * jnp.cos(an), mg * jnp.sin(an)

    lam_r, lam_i = powtab(1.0)
    den = ar * ar + ai * ai
    nr = lam_r - 1.0
    coef_r = (nr * ar + lam_i * ai) / den
    coef_i = (lam_i * ar - nr * ai) / den
    bp_r, bp_i = _cmul(coef_r, coef_i, bbr_ref[0, 0], bbi_ref[0, 0])
    bb_ref[0, 0] = jnp.concatenate([bp_r, bp_i], axis=-1).astype(BF16)
    cbr = cbr_ref[0, 0]
    cbi = cbi_ref[0, 0]
    cb_ref[0, 0] = jnp.concatenate([cbr, -cbi], axis=0).astype(BF16)
    jcol = lax.broadcasted_iota(jnp.int32, (tc, 1), 0).astype(F32)
    lag_r, lag_i = powtab(jcol)
    cbr_b = cbr.astype(BF16)
    cbi_b = cbi.astype(BF16)
    for m in range(tc):
        bm_r, bm_i = _cmul(bp_r, bp_i, lag_r[m:m + 1, :], lag_i[m:m + 1, :])
        k_m = (jnp.dot(bm_r.astype(BF16), cbr_b, preferred_element_type=F32)
               - jnp.dot(bm_i.astype(BF16), cbi_b, preferred_element_type=F32))
        ks_ref[0, 0, m * LANES:(m + 1) * LANES, :] = k_m.astype(BF16)
    e_r, e_i = powtab(jnp.where(fwd, (tc - 1.0) - jcol, jcol))
    c_r, c_i = powtab(jnp.where(fwd, jcol + 1.0, tc - jcol))
    we_ref[0, 0] = jnp.concatenate([e_r, e_i], axis=-1)
    wc_ref[0, 0] = jnp.concatenate([c_r, c_i], axis=-1)
    t_r, t_i = powtab(float(tc))
    lt_ref[0, 0] = jnp.concatenate([t_r, t_i], axis=-1)


def _s5_prep(a_re, a_im, log_step, b_re, b_im, c_re, c_im):
    _, g, p = a_re.shape
    c = b_re.shape[-1]
    gq = LANES // c
    nq = g // gq
    sq = gq * p
    tc = S5_CHUNK
    eye = jnp.eye(gq, dtype=F32)
    lanes_of = lambda a: a.reshape(2, nq, 1, sq)
    ls = jnp.broadcast_to(log_step[:, :, None], (2, g, p))
    bbd = lambda b: jnp.einsum("dqgpc,gh->dqgchp", b.reshape(2, nq, gq, p, c), eye).reshape(2, nq, gq * c, sq)
    cbd = lambda m: jnp.einsum("dqgcp,gh->dqgphc", m.reshape(2, nq, gq, c, p), eye).reshape(2, nq, sq, gq * c)
    blk = lambda *s: pl.BlockSpec((1, 1) + s, lambda d, q: (d, q, 0, 0))
    shp = lambda s, dt: jax.ShapeDtypeStruct((2, nq) + s, dt)
    return pl.pallas_call(
        _s5_prep_kernel,
        grid=(2, nq),
        in_specs=[blk(1, sq), blk(1, sq), blk(1, sq), blk(LANES, sq), blk(LANES, sq), blk(sq, LANES),
                  blk(sq, LANES)],
        out_specs=[blk(tc * LANES, LANES), blk(LANES, 2 * sq), blk(2 * sq, LANES), blk(tc, 2 * sq),
                   blk(tc, 2 * sq), blk(1, 2 * sq)],
        out_shape=[shp((tc * LANES, LANES), BF16), shp((LANES, 2 * sq), BF16), shp((2 * sq, LANES), BF16),
                   shp((tc, 2 * sq), F32), shp((tc, 2 * sq), F32), shp((1, 2 * sq), F32)],
        compiler_params=_cparams(("arbitrary", "arbitrary")),
        name="s5_prep",
    )(lanes_of(a_re), lanes_of(a_im), lanes_of(ls), bbd(b_re), bbd(b_im), cbd(c_re), cbd(c_im))


def _cmul(ar, ai, br, bi):
    return ar * br - ai * bi, ar * bi + ai * br


def _s5_kernel(*refs, tm, rev, has_prev):
    if has_prev:
        u_ref, prev_ref, ks_ref, bb_ref, cb_ref, we_ref, wc_ref, lt_ref, y_ref, pad_ref, st_ref = refs
    else:
        u_ref, ks_ref, bb_ref, cb_ref, we_ref, wc_ref, lt_ref, y_ref, pad_ref, st_ref = refs
    tc = S5_CHUNK
    nck = tm // tc
    nq = u_ref.shape[1] // LANES
    sq = S5_TILE_STATES

    @pl.when(pl.program_id(1) == 0)
    def _():
        st_ref[...] = jnp.zeros_like(st_ref)

    pad_ref[0:tc, :] = jnp.zeros((tc, LANES), F32)
    pad_ref[tc + tm:, :] = jnp.zeros((tc, LANES), F32)
    jpos = lax.broadcasted_iota(jnp.int32, (tm, 1), 0) % tc
    order = range(nck - 1, -1, -1) if rev else range(nck)
    for q in range(nq):
        lanes = slice(q * LANES, (q + 1) * LANES)
        u = u_ref[:, lanes]
        pad_ref[tc:tc + tm, :] = u
        cols = []
        for m in range(tc):
            if rev:
                sh = pad_ref[tc + m:tc + m + tm, :]
                keep = jpos + m < tc
            else:
                sh = pad_ref[tc - m:tc - m + tm, :]
                keep = jpos >= m
            cols.append(jnp.where(keep, sh, 0.0).astype(BF16))
        y = jnp.dot(jnp.concatenate(cols, axis=-1), ks_ref[0, q], preferred_element_type=F32)

        x = jnp.dot(u.astype(BF16), bb_ref[0, q], preferred_element_type=F32)
        we = jnp.tile(we_ref[0, q], (nck, 1))
        er, ei = _cmul(x[:, :sq], x[:, sq:], we[:, :sq], we[:, sq:])
        er = jnp.sum(er.reshape(nck, tc, sq), axis=1)
        ei = jnp.sum(ei.reshape(nck, tc, sq), axis=1)
        lam_r = lt_ref[0, q][:, :sq]
        lam_i = lt_ref[0, q][:, sq:]
        s_r = st_ref[2 * q:2 * q + 1, :]
        s_i = st_ref[2 * q + 1:2 * q + 2, :]
        ent_r, ent_i = [None] * nck, [None] * nck
        for k in order:
            ent_r[k] = jnp.broadcast_to(s_r, (tc, sq))
            ent_i[k] = jnp.broadcast_to(s_i, (tc, sq))
            n_r, n_i = _cmul(lam_r, lam_i, s_r, s_i)
            s_r = n_r + er[k:k + 1, :]
            s_i = n_i + ei[k:k + 1, :]
        st_ref[2 * q:2 * q + 1, :] = s_r
        st_ref[2 * q + 1:2 * q + 2, :] = s_i
        wc = jnp.tile(wc_ref[0, q], (nck, 1))
        z_r, z_i = _cmul(wc[:, :sq], wc[:, sq:], jnp.concatenate(ent_r, axis=0), jnp.concatenate(ent_i, axis=0))
        y = y + jnp.dot(jnp.concatenate([z_r, z_i], axis=-1).astype(BF16), cb_ref[0, q],
                        preferred_element_type=F32)
        y_ref[:, lanes] = (prev_ref[:, lanes] + y) if has_prev else y


def _s5(u_ssm, prep, *, n_batch, seq, ctx_len):
    t, dw = u_ssm.shape
    tm = ROW_TILE
    nq = dw // LANES
    tc = S5_CHUNK
    sq2 = 2 * S5_TILE_STATES
    nct, nlt = ctx_len // tm, seq // tm
    lat_tiles = n_batch * nlt
    ks, bb, cb, we, wc, lt = prep

    def tile_of(rev):
        def f(b, s):
            cpos = (nct - 1 - s) if rev else s
            lpos = (nlt - 1 - (s - nct)) if rev else (s - nct)
            return jnp.where(s < nct, lat_tiles + b * nct + cpos, b * nlt + lpos)
        return f

    y = None
    for d, rev in enumerate((False, True)):
        tile = tile_of(rev)
        row = pl.BlockSpec((tm, dw), lambda b, s: (tile(b, s), 0))
        par = lambda *shape: pl.BlockSpec((1, nq) + shape, lambda b, s: (d, 0, 0, 0))
        has_prev = y is not None
        kern = functools.partial(_s5_kernel, tm=tm, rev=rev, has_prev=has_prev)
        y = pl.pallas_call(
            kern,
            grid=(n_batch, nct + nlt),
            in_specs=[row] * (2 if has_prev else 1) + [par(tc * LANES, LANES), par(LANES, sq2), par(sq2, LANES),
                                                      par(tc, sq2), par(tc, sq2), par(1, sq2)],
            out_specs=row,
            out_shape=jax.ShapeDtypeStruct((t, dw), F32),
            scratch_shapes=[pltpu.VMEM((tm + 2 * tc, LANES), F32), pltpu.VMEM((2 * nq, S5_TILE_STATES), F32)],
            compiler_params=_cparams(("arbitrary", "arbitrary")),
            name="s5_scan",
        )(*([u_ssm] + ([y] if has_prev else []) + [ks, bb, cb, we, wc, lt]))
    return y


_NT = (((1,), (1,)), ((), ()))


def _attn_lat_kernel(q_ref, kl_ref, vl_ref, kc_ref, vc_ref, o_ref):
    q = q_ref[...]
    tq = q.shape[0]
    seq = kl_ref.shape[0]
    kc = min(ATTN_KEY_CHUNK, seq)
    chunks = [(kl_ref, vl_ref, c * kc, kc) for c in range(seq // kc)] + [(kc_ref, vc_ref, 0, kc_ref.shape[0])]
    m = jnp.full((tq, 1), NEG_BIG, F32)
    l = jnp.zeros((tq, 1), F32)
    acc = jnp.zeros((tq, V_HEAD), F32)
    for k_ref, v_ref, off, n in chunks:
        s = lax.dot_general(q, k_ref[off:off + n, :], _NT, preferred_element_type=F32)
        m_new = jnp.maximum(m, jnp.max(s, axis=-1, keepdims=True))
        alpha = jnp.exp(m - m_new)
        p = jnp.exp(s - m_new)
        l = alpha * l + jnp.sum(p, axis=-1, keepdims=True)
        acc = alpha * acc + jnp.dot(p.astype(BF16), v_ref[off:off + n, :], preferred_element_type=F32)
        m = m_new
    o_ref[...] = acc / l


def _attn_ctx_kernel(q_ref, kc_ref, vc_ref, o_ref):
    s = lax.dot_general(q_ref[...], kc_ref[...], _NT, preferred_element_type=F32)
    p = jnp.exp(s - jnp.max(s, axis=-1, keepdims=True))
    l = jnp.sum(p, axis=-1, keepdims=True)
    o_ref[...] = jnp.dot(p.astype(BF16), vc_ref[...], preferred_element_type=F32) / l


def _attention(q, k, v, *, n_batch, seq, ctx_len, ctx_queries):
    n_heads = v.shape[1] // V_HEAD
    tq = ATTN_Q_TILE
    nq = seq // tq
    cblk = n_batch * seq // ctx_len
    y_lat = pl.pallas_call(
        _attn_lat_kernel,
        grid=(n_batch, n_heads, nq),
        in_specs=[
            pl.BlockSpec((tq, QK_PAD), lambda b, h, i: (b * nq + i, h)),
            pl.BlockSpec((seq, QK_PAD), lambda b, h, i: (b, h)),
            pl.BlockSpec((seq, V_HEAD), lambda b, h, i: (b, h)),
            pl.BlockSpec((ctx_len, QK_PAD), lambda b, h, i: (cblk + b, h)),
            pl.BlockSpec((ctx_len, V_HEAD), lambda b, h, i: (cblk + b, h)),
        ],
        out_specs=pl.BlockSpec((tq, V_HEAD), lambda b, h, i: (b * nq + i, h)),
        out_shape=jax.ShapeDtypeStruct((n_batch * seq, n_heads * V_HEAD), F32),
        compiler_params=_cparams(("arbitrary", "arbitrary", "arbitrary")),
        name="attn_latent",
    )(q, k, v, k, v)
    if not ctx_queries:
        return y_lat, None
    y_ctx = pl.pallas_call(
        _attn_ctx_kernel,
        grid=(n_batch, n_heads),
        in_specs=[
            pl.BlockSpec((ctx_len, QK_PAD), lambda b, h: (cblk + b, h)),
            pl.BlockSpec((ctx_len, QK_PAD), lambda b, h: (cblk + b, h)),
            pl.BlockSpec((ctx_len, V_HEAD), lambda b, h: (cblk + b, h)),
        ],
        out_specs=pl.BlockSpec((ctx_len, V_HEAD), lambda b, h: (b, h)),
        out_shape=jax.ShapeDtypeStruct((n_batch * ctx_len, n_heads * V_HEAD), F32),
        compiler_params=_cparams(("arbitrary", "arbitrary")),
        name="attn_context",
    )(q, k, v)
    return y_lat, y_ctx


def _merge_kernel(x_ref, yp_ref, ys_ref, us_ref, yml_ref, *rest, tiles_per_seq, n_batch, d_model, has_ctx):
    if has_ctx:
        ymc_ref, rest = rest[0], rest[1:]
    (mod_ref, sd_ref, wglu_ref, bglu_ref, on_ref, wout_ref, nffn_ref, wr_ref, br_ref,
     x1_ref, h2_ref, idx_ref, gate_ref, rank_ref, cnt_ref, carry_ref) = rest
    d = d_model
    d4 = d // 4
    i = pl.program_id(0)
    tm = x_ref.shape[0]
    r = jnp.minimum(i // tiles_per_seq, n_batch)
    g1 = mod_ref[pl.ds(r, 1), pl.ds(2 * d, d)]
    sh2 = mod_ref[pl.ds(r, 1), pl.ds(3 * d, d)]
    sc2 = mod_ref[pl.ds(r, 1), pl.ds(4 * d, d)]

    z = ys_ref[...] + sd_ref[...] * us_ref[...]
    g = jax.nn.gelu(z, approximate=True)
    glu = jnp.dot(g.astype(BF16), wglu_ref[...], preferred_element_type=F32) + bglu_ref[...]
    y_ssm = g * jax.nn.sigmoid(glu)
    on = on_ref[...]
    y_mla = yml_ref[...]
    if has_ctx:
        y_mla = jnp.where(i < n_batch * tiles_per_seq, y_mla, ymc_ref[...])
    cat = jnp.concatenate([_rms(yp_ref[...], on[:, 0:d4]), _rms(y_ssm, on[:, d4:2 * d4]),
                           _rms(y_mla, on[:, 2 * d4:])], axis=-1).astype(BF16)
    x1 = x_ref[...] + g1 * jnp.dot(cat, wout_ref[...], preferred_element_type=F32)
    x1_ref[...] = x1
    h2 = (_rms(x1, nffn_ref[...]) * (1.0 + sc2) + sh2).astype(BF16)
    h2_bits = lax.bitcast_convert_type(h2.astype(F32), jnp.uint32)
    h2_ref[...] = (h2_bits[:, :d // 2] >> 16) | (h2_bits[:, d // 2:] & jnp.uint32(0xFFFF0000))

    logits = jnp.dot(h2, wr_ref[...], preferred_element_type=F32) + br_ref[...]
    lane = lax.broadcasted_iota(jnp.int32, (tm, LANES), 1)
    vals, hots = [], []
    idx_out = jnp.zeros((tm, LANES), jnp.int32)
    for kk in range(TOP_K):
        mx = jnp.max(logits, axis=-1, keepdims=True)
        sel = jnp.min(jnp.where(logits == mx, lane, LANES), axis=-1, keepdims=True)
        hot = lane == sel
        vals.append(mx)
        hots.append(hot)
        idx_out = jnp.where(lane == kk, sel, idx_out)
        logits = jnp.where(hot, NEG_BIG * 2.0, logits)
    es = [jnp.exp(vv - vals[0]) for vv in vals]
    den = es[0] + es[1] + es[2] + es[3]
    gate_out = jnp.zeros((tm, LANES), F32)
    for kk in range(TOP_K):
        gate_out = jnp.where(lane == kk, es[kk] / den, gate_out)
    idx_ref[...] = idx_out
    gate_ref[...] = gate_out

    @pl.when(i == 0)
    def _():
        carry_ref[...] = jnp.zeros_like(carry_ref)

    onehot = (hots[0] | hots[1] | hots[2] | hots[3]).astype(BF16)
    rr = lax.broadcasted_iota(jnp.int32, (tm, tm), 0)
    cc = lax.broadcasted_iota(jnp.int32, (tm, tm), 1)
    tri = (cc < rr).astype(BF16)
    pos = jnp.dot(tri, onehot, preferred_element_type=F32) + carry_ref[...]
    rank_out = jnp.zeros((tm, LANES), jnp.int32)
    for kk in range(TOP_K):
        rk = jnp.sum(jnp.where(hots[kk], pos, 0.0), axis=-1, keepdims=True).astype(jnp.int32)
        rank_out = jnp.where(lane == kk, rk, rank_out)
    rank_ref[...] = rank_out
    carry = carry_ref[...] + jnp.sum(onehot.astype(F32), axis=0, keepdims=True)
    carry_ref[...] = carry
    cnt_ref[...] = carry.astype(jnp.int32)


def _merge(xt, y_pool, y_s5, u_ssm, y_mla_lat, y_mla_ctx, mods, ssm_d, wglu, bglu, out_norm, wout, nffn, wr, br, *,
           n_rows, n_batch, seq):
    d = xt.shape[1]
    tm = ROW_TILE
    n_lat_tiles = n_batch * seq // tm
    has_ctx = y_mla_ctx is not None
    row = lambda w: pl.BlockSpec((tm, w), lambda i: (i, 0))
    kern = functools.partial(_merge_kernel, tiles_per_seq=seq // tm, n_batch=n_batch, d_model=d, has_ctx=has_ctx)
    small = [mods, ssm_d, wglu, bglu, out_norm, wout, nffn, wr, br]
    y_mla = [y_mla_lat] + ([y_mla_ctx] if has_ctx else [])
    y_specs = [pl.BlockSpec((tm, d // 2), lambda i: (jnp.minimum(i, n_lat_tiles - 1), 0))]
    if has_ctx:
        y_specs.append(pl.BlockSpec((tm, d // 2), lambda i: (jnp.maximum(i - n_lat_tiles, 0), 0)))
    return pl.pallas_call(
        kern,
        grid=(n_rows // tm,),
        in_specs=[row(d), row(d // 4), row(d // 4), row(d // 4)] + y_specs + [_full(a.shape) for a in small],
        out_specs=[row(d), row(d // 2), row(LANES), row(LANES), row(LANES), _full((1, LANES))],
        out_shape=[jax.ShapeDtypeStruct((n_rows, d), F32), jax.ShapeDtypeStruct((n_rows, d // 2), jnp.uint32),
                   jax.ShapeDtypeStruct((n_rows, LANES), jnp.int32), jax.ShapeDtypeStruct((n_rows, LANES), F32),
                   jax.ShapeDtypeStruct((n_rows, LANES), jnp.int32), jax.ShapeDtypeStruct((1, LANES), jnp.int32)],
        scratch_shapes=[pltpu.VMEM((1, LANES), F32)],
        compiler_params=_cparams(("arbitrary",)),
        name="merge_router",
    )(xt, y_pool, y_s5, u_ssm, *y_mla, *small)


def _row_copy(src, dst, sem, src_row, dst_row):
    return pltpu.make_async_copy(src.at[pl.ds(src_row, 1), :], dst.at[pl.ds(dst_row, 1), :], sem)


def _dispatch_kernel(dest_hbm, h2_ref, zeros_hbm, xs_hbm, dest_smem, sem_idx, sem_rows, *, tm):
    del zeros_hbm
    i = pl.program_id(0)
    n = tm * TOP_K
    idx_copy = pltpu.make_async_copy(dest_hbm.at[i], dest_smem, sem_idx)
    idx_copy.start()
    idx_copy.wait()

    def issue(r, _):
        for kk in range(TOP_K):
            _row_copy(h2_ref, xs_hbm, sem_rows, r, dest_smem[r * TOP_K + kk]).start(priority=kk % 2)
        return 0

    lax.fori_loop(0, tm, issue, 0, unroll=2)

    def drain(j, _):
        _row_copy(h2_ref, xs_hbm, sem_rows, j // TOP_K, dest_smem[j]).wait()
        return 0

    lax.fori_loop(0, n, drain, 0, unroll=8)


def _dispatch(dest_tiles, h2_packed, n_sorted_rows):
    n_tok, w = h2_packed.shape
    tm = ROW_TILE
    zeros = jnp.zeros((n_sorted_rows, w), h2_packed.dtype)
    return pl.pallas_call(
        functools.partial(_dispatch_kernel, tm=tm),
        grid=(n_tok // tm,),
        in_specs=[pl.BlockSpec(memory_space=pl.ANY), pl.BlockSpec((tm, w), lambda i: (i, 0)),
                  pl.BlockSpec(memory_space=pl.ANY)],
        out_specs=pl.BlockSpec(memory_space=pl.ANY),
        out_shape=jax.ShapeDtypeStruct(zeros.shape, zeros.dtype),
        input_output_aliases={2: 0},
        scratch_shapes=[pltpu.SMEM((tm * TOP_K,), jnp.int32), pltpu.SemaphoreType.DMA, pltpu.SemaphoreType.DMA],
        compiler_params=_cparams(("arbitrary",)),
        name="moe_dispatch",
    )(dest_tiles, h2_packed, zeros)


def _expert_kernel(ie_ref, ix_ref, io_ref, ns_ref, nz_ref, x_ref, wg_ref, bg_ref, wu_ref, bu_ref, wd_ref, bd_ref,
                   o_hbm, act_ref, xb_ref, res_ref, zero_ref, sem_out, sem_zero, *, nf):
    del ie_ref, ix_ref
    it = pl.program_id(0)
    j = pl.program_id(1)
    ns = ns_ref[it]
    rows_max = x_ref.shape[0]
    half = x_ref.shape[1]
    tf = res_ref.shape[2]
    sub = EXPERT_SUB_ROWS
    o_row = pl.multiple_of(io_ref[it] * sub, sub)
    n = j - nf

    def out_copy(step, mr):
        col = pl.multiple_of(step * tf, tf)
        return pltpu.make_async_copy(res_ref.at[step % 2, 0:mr, :], o_hbm.at[pl.ds(o_row, mr), pl.ds(col, tf)],
                                     sem_out.at[step % 2])

    @pl.when((ns > 0) & (j == 0))
    def _():
        words = x_ref[...]
        xb_ref[:, 0:half] = lax.bitcast_convert_type(words << 16, F32).astype(BF16)
        xb_ref[:, half:] = lax.bitcast_convert_type(words & jnp.uint32(0xFFFF0000), F32).astype(BF16)

    for m in range(1, rows_max // EXPERT_SUB_ROWS + 1):
        mr = m * EXPERT_SUB_ROWS

        @pl.when((ns == m) & (j < nf))
        def _():
            x = xb_ref[0:mr, :]
            g = jnp.dot(x, wg_ref[0, 0].astype(BF16), preferred_element_type=F32) + bg_ref[0, 0]
            u = jnp.dot(x, wu_ref[0, 0].astype(BF16), preferred_element_type=F32) + bu_ref[0, 0]
            g = jnp.minimum(g, SWIGLU_LIMIT)
            u = jnp.clip(u, -SWIGLU_LIMIT, SWIGLU_LIMIT)
            act_ref[j, 0:mr, :] = (g * jax.nn.sigmoid(SWIGLU_ALPHA * g) * (u + 1.0)).astype(BF16)

        @pl.when((ns == m) & (j >= nf))
        def _():
            @pl.when(n >= 2)
            def _():
                out_copy(n - 2, mr).wait()

            act = jnp.concatenate([act_ref[f, 0:mr, :] for f in range(nf)], axis=-1)
            res_ref[n % 2, 0:mr, :] = (jnp.dot(act, wd_ref[0, 0].astype(BF16), preferred_element_type=F32)
                                       + bd_ref[0, 0])
            out_copy(n, mr).start()

            @pl.when(n == nf - 1)
            def _():
                out_copy(n - 1, mr).wait()
                out_copy(n, mr).wait()

    nz = nz_ref[it]

    @pl.when((nz > 0) & (j == 2 * nf - 1))
    def _():
        zero_ref[...] = jnp.zeros(zero_ref.shape, F32)

        def zero_copy(k):
            rows0 = pl.multiple_of(o_row + k * sub, sub)
            return pltpu.make_async_copy(zero_ref, o_hbm.at[pl.ds(rows0, sub), :], sem_zero)

        def start(k, _):
            zero_copy(k).start()
            return 0

        def wait(k, _):
            zero_copy(k).wait()
            return 0

        lax.fori_loop(0, nz, start, 0)
        lax.fori_loop(0, nz, wait, 0)


def _experts(layer, items, xs, w_gate, b_gate, w_up, b_up, w_down, b_down):
    item_e, item_x, item_o, item_ns, item_nz = items
    n_items = item_e.shape[0]
    n_layers, n_exp, d, f_dim = w_gate.shape
    tf = EXPERT_TILE
    nf = f_dim // tf
    rows = EXPERT_ITEM_ROWS
    assert d // tf == nf and nf >= 2

    def fi(j, ns, it):
        return jnp.where(ns[it] > 0, jnp.minimum(j, nf - 1), nf - 1)

    def ni(j, ns, it):
        return jnp.where(ns[it] > 0, jnp.maximum(j - nf, 0), nf - 1)

    return pl.pallas_call(
        functools.partial(_expert_kernel, nf=nf),
        grid_spec=pltpu.PrefetchScalarGridSpec(
            num_scalar_prefetch=5,
            grid=(n_items, 2 * nf),
            in_specs=[
                pl.BlockSpec((pl.Element(rows), pl.Element(d // 2)),
                             lambda it, j, ie, ix, io, ns, wk: (ix[it] * EXPERT_SUB_ROWS, 0),
                             pipeline_mode=pl.Buffered(1)),
                pl.BlockSpec((1, 1, d, tf), lambda it, j, ie, ix, io, ns, wk: (layer, ie[it], 0, fi(j, ns, it))),
                pl.BlockSpec((1, 1, 1, tf), lambda it, j, ie, ix, io, ns, wk: (layer, ie[it], 0, fi(j, ns, it))),
                pl.BlockSpec((1, 1, d, tf), lambda it, j, ie, ix, io, ns, wk: (layer, ie[it], 0, fi(j, ns, it))),
                pl.BlockSpec((1, 1, 1, tf), lambda it, j, ie, ix, io, ns, wk: (layer, ie[it], 0, fi(j, ns, it))),
                pl.BlockSpec((1, 1, f_dim, tf), lambda it, j, ie, ix, io, ns, wk: (layer, ie[it], 0, ni(j, ns, it))),
                pl.BlockSpec((1, 1, 1, tf), lambda it, j, ie, ix, io, ns, wk: (layer, ie[it], 0, ni(j, ns, it))),
            ],
            out_specs=pl.BlockSpec(memory_space=pl.ANY),
            scratch_shapes=[pltpu.VMEM((nf, rows, tf), BF16), pltpu.VMEM((rows, d), BF16),
                            pltpu.VMEM((2, rows, tf), F32), pltpu.VMEM((EXPERT_SUB_ROWS, d), F32),
                            pltpu.SemaphoreType.DMA((2,)), pltpu.SemaphoreType.DMA],
        ),
        out_shape=jax.ShapeDtypeStruct((xs.shape[0], d), F32),
        compiler_params=_cparams(("arbitrary", "arbitrary"), EXPERT_VMEM_LIMIT),
        name="moe_experts",
    )(item_e, item_x, item_o, item_ns, item_nz, xs, w_gate, b_gate.reshape(n_layers, n_exp, 1, f_dim), w_up,
      b_up.reshape(n_layers, n_exp, 1, f_dim), w_down, b_down.reshape(n_layers, n_exp, 1, d))


def _combine_kernel(dest_hbm, os_hbm, gate_ref, x_ref, mod_ref, o_ref, dest_smem, buf, sem_idx, sem_rows,
                    *, tc, n_tiles, tiles_per_seq, n_batch):
    i = pl.program_id(0)
    n = tc * TOP_K
    slot = i % 2
    d = x_ref.shape[1]

    def gather(s, j):
        return _row_copy(os_hbm, buf.at[s, j % TOP_K], sem_rows.at[s], dest_smem[s, j], j // TOP_K)

    def fetch(tile, s):
        idx_copy = pltpu.make_async_copy(dest_hbm.at[tile], dest_smem.at[s], sem_idx)
        idx_copy.start()
        idx_copy.wait()

        def issue(r, _):
            for kk in range(TOP_K):
                gather(s, r * TOP_K + kk).start(priority=kk % 2)
            return 0

        lax.fori_loop(0, tc, issue, 0, unroll=2)

    @pl.when(i == 0)
    def _():
        fetch(0, 0)

    @pl.when(i + 1 < n_tiles)
    def _():
        fetch(i + 1, 1 - slot)

    def drain(j, _):
        gather(slot, j).wait()
        return 0

    lax.fori_loop(0, n, drain, 0, unroll=8)
    g2 = mod_ref[pl.ds(jnp.minimum(i // tiles_per_seq, n_batch), 1), pl.ds(5 * d, d)]
    gates = gate_ref[...]
    acc = gates[:, 0:1] * buf[slot, 0]
    for kk in range(1, TOP_K):
        acc = acc + gates[:, kk:kk + 1] * buf[slot, kk]
    o_ref[...] = x_ref[...] + g2 * acc


def _combine(dest_tiles, out_sorted, gates, x1, mods, *, n_batch, seq):
    n_tok, d = x1.shape
    tc = COMBINE_TILE
    kern = functools.partial(_combine_kernel, tc=tc, n_tiles=n_tok // tc, tiles_per_seq=seq // tc, n_batch=n_batch)
    return pl.pallas_call(
        kern,
        grid=(n_tok // tc,),
        in_specs=[pl.BlockSpec(memory_space=pl.ANY)] * 2 + [
            pl.BlockSpec((tc, LANES), lambda i: (i, 0)), pl.BlockSpec((tc, d), lambda i: (i, 0)),
            _full(mods.shape)],
        out_specs=pl.BlockSpec((tc, d), lambda i: (i, 0)),
        out_shape=jax.ShapeDtypeStruct(x1.shape, F32),
        scratch_shapes=[pltpu.SMEM((2, tc * TOP_K), jnp.int32), pltpu.VMEM((2, TOP_K, tc, d), F32),
                        pltpu.SemaphoreType.DMA, pltpu.SemaphoreType.DMA((2,))],
        compiler_params=_cparams(("arbitrary",)),
        name="moe_combine",
    )(dest_tiles, out_sorted, gates, x1, mods)


def _rope_tables(n_batch, seq, ctx_len):
    rows = seq // GRID_W
    row = jnp.broadcast_to(jnp.arange(rows)[:, None], (rows, GRID_W)).reshape(-1).astype(F32)
    col = jnp.broadcast_to(jnp.arange(GRID_W)[None, :], (rows, GRID_W)).reshape(-1).astype(F32)
    half = QK_ROPE // 2
    inv_freq = ROPE_BASE ** (-jnp.arange(0, half, 2, dtype=F32) / half)
    ang = jnp.concatenate([row[:, None] * inv_freq, col[:, None] * inv_freq], axis=-1)
    cos, sin = jnp.cos(ang), jnp.sin(ang)
    zpad = jnp.zeros((seq, LANES - QK_ROPE), F32)
    cos_l = jnp.concatenate([cos, cos, zpad], axis=-1)
    sin_l = jnp.concatenate([-sin, sin, zpad], axis=-1)
    n_ctx = n_batch * ctx_len
    cos_c = jnp.concatenate([jnp.ones((n_ctx, QK_ROPE), F32), jnp.zeros((n_ctx, LANES - QK_ROPE), F32)], axis=-1)
    cos_t = jnp.concatenate([jnp.tile(cos_l, (n_batch, 1)), cos_c], axis=0)
    sin_t = jnp.concatenate([jnp.tile(sin_l, (n_batch, 1)), jnp.zeros((n_ctx, LANES), F32)], axis=0)
    return cos_t, sin_t


def _pad_head_vec(vec):
    return jnp.concatenate([vec, jnp.zeros((QK_PAD - QK_HEAD,), vec.dtype)])[None, :]


def _routing(idx, rank, counts, n_tok, n_exp):
    rows, sub = EXPERT_ITEM_ROWS, EXPERT_SUB_ROWS
    n = n_tok * TOP_K
    cap = n + n_exp * sub + rows
    n_items = -(-n // rows) + n_exp + 1
    i32 = lambda a: a.astype(jnp.int32)
    cnt = counts[0, :n_exp]
    seg = ((cnt + sub - 1) // sub) * sub
    seg_end = jnp.cumsum(seg)
    seg_start = seg_end - seg
    dest = seg_start[idx] + rank
    used_end = seg_end[-1]
    per = (cnt + rows - 1) // rows
    it_end = jnp.cumsum(per)
    it_start = it_end - per
    total = it_end[-1]
    it = jnp.arange(n_items, dtype=jnp.int32)
    it_c = jnp.minimum(it, total - 1)
    e_of = i32(jnp.minimum(jnp.searchsorted(it_end, it_c, side="right"), n_exp - 1))
    k_in = it_c - it_start[e_of]
    x_row = seg_start[e_of] + k_in * rows
    nsub = jnp.clip((cnt[e_of] - k_in * rows + sub - 1) // sub, 0, rows // sub)
    active = it < total
    nsub = jnp.where(active, nsub, 0)
    first_idle = it == total
    o_row = jnp.where(active, x_row, jnp.where(first_idle, used_end, 0))
    n_zero = jnp.where(first_idle, (cap - used_end) // sub, 0)
    return i32(dest), (e_of, i32(x_row // sub), i32(o_row // sub), i32(nsub), i32(n_zero)), cap


def _layer(layer, xt, mods, cos_t, sin_t, p, expert_params, *, n_batch, seq, ctx_len, ctx_out):
    t, d = xt.shape
    d4 = d // 4
    n_heads = (d - 2 * d4) // V_HEAD
    kv_lora = d // 8
    bf = lambda a: a.astype(BF16)
    row = lambda a: a[None, :]

    win = bf(jnp.pad(p["w_in"], ((0, 0), (0, LANES - QK_ROPE))))
    wuq = p["w_uq"].reshape(-1, n_heads, QK_HEAD)
    wuq = bf(jnp.pad(wuq, ((0, 0), (0, 0), (0, QK_PAD - QK_HEAD))).reshape(-1, n_heads * QK_PAD))
    wukv = p["w_ukv"].reshape(kv_lora, n_heads, QK_NOPE + V_HEAD)
    wk = bf(wukv[:, :, :QK_NOPE].reshape(kv_lora, n_heads * QK_NOPE))
    wv = bf(wukv[:, :, QK_NOPE:].reshape(kv_lora, n_heads * V_HEAD))
    qn = _pad_head_vec(p["q_norm"]) * (QK_HEAD ** -0.5)
    kn = _pad_head_vec(p["k_norm"])

    u_pool, u_ssm, q, k, v = _proj(xt, mods, cos_t, sin_t, row(p["norm_mix"]), win, row(p["q_a_norm"]), wuq,
                                   row(p["kv_a_norm"]), wk, wv, qn, kn, n_batch=n_batch, seq=seq)

    y_pool = _pool(u_pool, bf(p["w_pool"]), row(p["pool_scale"]), n_batch=n_batch, seq=seq, ctx_len=ctx_len)

    prep = _s5_prep(p["ssm_a_re"], p["ssm_a_im"], p["ssm_log_step"], p["ssm_b_re"], p["ssm_b_im"],
                    p["ssm_c_re"], p["ssm_c_im"])
    y_s5 = _s5(u_ssm, prep, n_batch=n_batch, seq=seq, ctx_len=ctx_len)

    y_mla_lat, y_mla_ctx = _attention(q, k, v, n_batch=n_batch, seq=seq, ctx_len=ctx_len, ctx_queries=ctx_out)

    n_rows = t if ctx_out else n_batch * seq
    n_exp = p["w_router"].shape[1]
    wr = bf(jnp.pad(p["w_router"], ((0, 0), (0, LANES - n_exp))))
    br = jnp.concatenate([p["b_router"], jnp.full((LANES - n_exp,), NEG_BIG, F32)])[None, :]
    x1, h2, idx, gates, rank, counts = _merge(
        xt, y_pool, y_s5, u_ssm, y_mla_lat, y_mla_ctx, mods, row(p["ssm_d"]), bf(p["w_glu"]), row(p["b_glu"]),
        row(p["out_norm"]), bf(p["w_out"]), row(p["norm_ffn"]), wr, br, n_rows=n_rows, n_batch=n_batch, seq=seq)

    dest, items, cap = _routing(idx[:, :TOP_K], rank[:, :TOP_K], counts, n_rows, n_exp)
    xs = _dispatch(dest.reshape(n_rows // ROW_TILE, ROW_TILE * TOP_K), h2, cap)
    out_sorted = _experts(layer, items, xs, *expert_params)
    return _combine(dest.reshape(n_rows // COMBINE_TILE, COMBINE_TILE * TOP_K), out_sorted, gates, x1, mods,
                    n_batch=n_batch, seq=seq)


_PARAM_NAMES = ("norm_mix", "norm_ffn", "w_in", "w_pool", "pool_scale", "ssm_a_re", "ssm_a_im", "ssm_log_step",
                "ssm_b_re", "ssm_b_im", "ssm_c_re", "ssm_c_im", "ssm_d", "w_glu", "b_glu", "q_a_norm", "w_uq",
                "kv_a_norm", "w_ukv", "q_norm", "k_norm", "out_norm", "w_out", "w_router", "b_router")


def kernel(x, c, ctx, c_ctx, w_ada, b_ada, norm_mix, norm_ffn, w_in, w_pool, pool_scale, ssm_a_re, ssm_a_im, ssm_log_step, ssm_b_re, ssm_b_im, ssm_c_re, ssm_c_im, ssm_d, w_glu, b_glu, q_a_norm, w_uq, kv_a_norm, w_ukv, q_norm, k_norm, out_norm, w_out, w_router, b_router, w_gate, b_gate, w_up, b_up, w_down, b_down):
    n_batch, seq, d = x.shape
    ctx_len = ctx.shape[1]
    depth = w_ada.shape[0]
    assert seq % ROW_TILE == 0 and ctx_len % ROW_TILE == 0 and seq % GRID_W == 0 and n_batch + 1 <= 8
    stacked = dict(zip(_PARAM_NAMES, (norm_mix, norm_ffn, w_in, w_pool, pool_scale, ssm_a_re, ssm_a_im,
                                      ssm_log_step, ssm_b_re, ssm_b_im, ssm_c_re, ssm_c_im, ssm_d, w_glu, b_glu,
                                      q_a_norm, w_uq, kv_a_norm, w_ukv, q_norm, k_norm, out_norm, w_out, w_router,
                                      b_router)))
    expert_params = (w_gate, b_gate, w_up, b_up, w_down, b_down)
    cvecs = jnp.concatenate([c, c_ctx[None, :], jnp.zeros((8 - n_batch - 1, d), F32)], axis=0)
    mods = _ada(cvecs, w_ada, b_ada)
    cos_t, sin_t = _rope_tables(n_batch, seq, ctx_len)
    n_lat = n_batch * seq
    xt = jnp.concatenate([x.reshape(n_lat, d), ctx.reshape(n_batch * ctx_len, d)], axis=0)
    for layer in range(depth):
        p = {name: val[layer] for name, val in stacked.items()}
        xt = _layer(layer, xt, mods[layer], cos_t, sin_t, p, expert_params, n_batch=n_batch, seq=seq,
                    ctx_len=ctx_len, ctx_out=layer < depth - 1)
    return xt[:n_lat].reshape(n_batch, seq, d)
```

```python
import functools
import math

import jax
import jax.numpy as jnp
from jax import lax
from jax.experimental import pallas as pl
from jax.experimental.pallas import tpu as pltpu

F32 = jnp.float32
BF16 = jnp.bfloat16

EPS = 1e-6
GRID_W = 64
POOL_WINDOWS = (2, 4, 8, 16)
POOL_HALO = 8
SSM_GROUP = 16
SSM_STATE = 64
S5_CHUNK = 16
S5_TILE_STATES = (128 // SSM_GROUP) * SSM_STATE
QK_NOPE = 128
QK_ROPE = 64
QK_HEAD = QK_NOPE + QK_ROPE
QK_PAD = 256
V_HEAD = 128
ROPE_BASE = 10000.0
TOP_K = 4
SWIGLU_ALPHA = 1.702
SWIGLU_LIMIT = 7.0

LANES = 128
ROW_TILE = 256
ATTN_Q_TILE = 512
ATTN_KEY_CHUNK = 1024
EXPERT_ITEM_ROWS = 1280
EXPERT_SUB_ROWS = 256
EXPERT_TILE = 512
COMBINE_TILE = 256
VMEM_LIMIT = 52 * 1024 * 1024
EXPERT_VMEM_LIMIT = 57 * 1024 * 1024
NEG_BIG = -1e30


def _cparams(sem, vmem=VMEM_LIMIT):
    return pltpu.CompilerParams(dimension_semantics=sem, vmem_limit_bytes=vmem)


def _rms(x, w):
    return x * lax.rsqrt(jnp.mean(x * x, axis=-1, keepdims=True) + EPS) * w


def _full(shape):
    nd = len(shape)
    return pl.BlockSpec(shape, lambda *_: (0,) * nd)


def _ada_kernel(cv_ref, w_ref, b_ref, o_ref):
    cv = cv_ref[...]
    s = (cv * jax.nn.sigmoid(cv)).astype(BF16)
    o_ref[0] = jnp.dot(s, w_ref[0].astype(BF16), preferred_element_type=F32) + b_ref[0]


def _ada(cvecs, w_ada, b_ada):
    n_layers, d, n6 = w_ada.shape
    tn = 512
    return pl.pallas_call(
        _ada_kernel,
        grid=(n_layers, n6 // tn),
        in_specs=[
            pl.BlockSpec((8, d), lambda l, j: (0, 0)),
            pl.BlockSpec((1, d, tn), lambda l, j: (l, 0, j)),
            pl.BlockSpec((1, 1, tn), lambda l, j: (l, 0, j)),
        ],
        out_specs=pl.BlockSpec((1, 8, tn), lambda l, j: (l, 0, j)),
        out_shape=jax.ShapeDtypeStruct((n_layers, 8, n6), F32),
        compiler_params=_cparams(("arbitrary", "arbitrary")),
        name="ada",
    )(cvecs, w_ada, b_ada.reshape(n_layers, 1, n6))


def _rope128(r, cos, sin):
    lane = lax.broadcasted_iota(jnp.int32, r.shape, 1)
    sw = jnp.where((lane % QK_ROPE) < QK_ROPE // 2, pltpu.roll(r, LANES - 32, 1), pltpu.roll(r, 32, 1))
    return r * cos + sw * sin


def _proj_kernel(x_ref, mod_ref, cos_ref, sin_ref, nmix_ref, win_ref, qan_ref, wuq_ref, kvan_ref,
                 wk_ref, wv_ref, qn_ref, kn_ref,
                 upool_ref, ussm_ref, q_ref, k_ref, v_ref, *, tiles_per_seq, n_batch, n_heads, d_model):
    d = d_model
    r = jnp.minimum(pl.program_id(0) // tiles_per_seq, n_batch)
    sh = mod_ref[pl.ds(r, 1), pl.ds(0, d)]
    sc = mod_ref[pl.ds(r, 1), pl.ds(d, d)]
    h = _rms(x_ref[...], nmix_ref[...]) * (1.0 + sc) + sh
    proj = jnp.dot(h.astype(BF16), win_ref[...], preferred_element_type=F32)
    d4 = d // 4
    upool_ref[...] = proj[:, 0:d4]
    ussm_ref[...] = proj[:, 2 * d4:3 * d4]
    cos = cos_ref[...]
    sin = sin_ref[...]

    qa = _rms(proj[:, d4:2 * d4], qan_ref[...]).astype(BF16)
    qf = jnp.dot(qa, wuq_ref[...], preferred_element_type=F32)
    qn_w = qn_ref[...]
    for hd in range(n_heads):
        blk = qf[:, hd * QK_PAD:(hd + 1) * QK_PAD]
        ss = jnp.sum(blk * blk, axis=-1, keepdims=True) * (1.0 / QK_HEAD)
        qn = blk * lax.rsqrt(ss + EPS) * qn_w
        q_ref[:, hd * QK_PAD:hd * QK_PAD + LANES] = qn[:, :LANES].astype(BF16)
        q_ref[:, hd * QK_PAD + LANES:(hd + 1) * QK_PAD] = _rope128(qn[:, LANES:], cos, sin).astype(BF16)

    kv_lo = 3 * d4
    kv_w = d // 8
    ka = _rms(proj[:, kv_lo:kv_lo + kv_w], kvan_ref[...]).astype(BF16)
    kn_all = jnp.dot(ka, wk_ref[...], preferred_element_type=F32)
    v_ref[...] = jnp.dot(ka, wv_ref[...], preferred_element_type=F32).astype(BF16)
    krp = proj[:, kv_lo + kv_w:kv_lo + kv_w + LANES]
    kr_ss = jnp.sum(krp * krp, axis=-1, keepdims=True)
    kn_w = kn_ref[...]
    for hd in range(n_heads):
        kn = kn_all[:, hd * QK_NOPE:(hd + 1) * QK_NOPE]
        ss = (jnp.sum(kn * kn, axis=-1, keepdims=True) + kr_ss) * (1.0 / QK_HEAD)
        rinv = lax.rsqrt(ss + EPS)
        k_ref[:, hd * QK_PAD:hd * QK_PAD + LANES] = (kn * rinv * kn_w[:, :LANES]).astype(BF16)
        k_ref[:, hd * QK_PAD + LANES:(hd + 1) * QK_PAD] = _rope128(
            krp * rinv * kn_w[:, LANES:], cos, sin).astype(BF16)


def _proj(xt, mods, cos_t, sin_t, nmix, win, qan, wuq, kvan, wk, wv, qn, kn, *, n_batch, seq):
    t, d = xt.shape
    tm = ROW_TILE
    n_heads = wv.shape[1] // V_HEAD
    row = lambda w: pl.BlockSpec((tm, w), lambda i: (i, 0))
    kern = functools.partial(_proj_kernel, tiles_per_seq=seq // tm, n_batch=n_batch,
                             n_heads=n_heads, d_model=d)
    return pl.pallas_call(
        kern,
        grid=(t // tm,),
        in_specs=[row(d), _full(mods.shape), row(LANES), row(LANES), _full(nmix.shape), _full(win.shape),
                  _full(qan.shape), _full(wuq.shape), _full(kvan.shape), _full(wk.shape), _full(wv.shape),
                  _full(qn.shape), _full(kn.shape)],
        out_specs=[row(d // 4), row(d // 4), row(n_heads * QK_PAD), row(n_heads * QK_PAD),
                   row(n_heads * V_HEAD)],
        out_shape=[jax.ShapeDtypeStruct((t, d // 4), F32), jax.ShapeDtypeStruct((t, d // 4), F32),
                   jax.ShapeDtypeStruct((t, n_heads * QK_PAD), BF16),
                   jax.ShapeDtypeStruct((t, n_heads * QK_PAD), BF16),
                   jax.ShapeDtypeStruct((t, n_heads * V_HEAD), BF16)],
        compiler_params=_cparams(("arbitrary",)),
        name="mixer_in",
    )(xt, mods, cos_t, sin_t, nmix, win, qan, wuq, kvan, wk, wv, qn, kn)


def _pool_kernel(prev_ref, cur_ref, next_ref, w_ref, scale_ref, o_ref, pad_ref, *,
                 tm, n_lat_tiles, seq, ctx_len):
    i = pl.program_id(0)
    hw = POOL_HALO
    pad_ref[0:hw, :] = prev_ref[...]
    pad_ref[hw:hw + tm, :] = cur_ref[...]
    pad_ref[hw + tm:2 * hw + tm, :] = next_ref[...]
    is_lat = i < n_lat_tiles
    row0 = i * tm
    s0 = jnp.where(is_lat, (row0 // seq) * seq,
                   n_lat_tiles * tm + ((row0 - n_lat_tiles * tm) // ctx_len) * ctx_len)
    slen = jnp.where(is_lat, seq, ctx_len)
    t = row0 - s0 + lax.broadcasted_iota(jnp.int32, (tm, 1), 0)
    for g, w in enumerate(POOL_WINDOWS):
        lanes = slice(g * LANES, (g + 1) * LANES)
        acc = jnp.zeros((tm, LANES), F32)
        for kk in range(-(w // 2), w - w // 2):
            valid = (t + kk >= 0) & (t + kk < slen)
            acc = acc + jnp.where(valid, pad_ref[hw + kk:hw + kk + tm, lanes], 0.0)
        lo = jnp.maximum(t - w // 2, 0)
        hi = jnp.minimum(t + (w - w // 2), slen)
        mean = acc / (hi - lo).astype(F32)
        dlt = (mean - cur_ref[:, lanes]).astype(BF16)
        o_ref[:, lanes] = jnp.dot(dlt, w_ref[g], preferred_element_type=F32) * scale_ref[:, lanes]


def _pool(u_pool, w_pool, pool_scale, *, n_batch, seq, ctx_len):
    t, dp = u_pool.shape
    tm = ROW_TILE
    hb = tm // POOL_HALO
    n_hblocks = t // POOL_HALO
    kern = functools.partial(_pool_kernel, tm=tm, n_lat_tiles=n_batch * seq // tm, seq=seq, ctx_len=ctx_len)
    return pl.pallas_call(
        kern,
        grid=(t // tm,),
        in_specs=[
            pl.BlockSpec((POOL_HALO, dp), lambda i: (jnp.maximum(i * hb - 1, 0), 0)),
            pl.BlockSpec((tm, dp), lambda i: (i, 0)),
            pl.BlockSpec((POOL_HALO, dp), lambda i: (jnp.minimum((i + 1) * hb, n_hblocks - 1), 0)),
            _full(w_pool.shape), _full(pool_scale.shape),
        ],
        out_specs=pl.BlockSpec((tm, dp), lambda i: (i, 0)),
        out_shape=jax.ShapeDtypeStruct((t, dp), F32),
        scratch_shapes=[pltpu.VMEM((tm + 2 * POOL_HALO, dp), F32)],
        compiler_params=_cparams(("arbitrary",)),
        name="pool",
    )(u_pool, u_pool, u_pool, w_pool, pool_scale)


def _s5_prep_kernel(are_ref, aim_ref, ls_ref, bbr_ref, bbi_ref, cbr_ref, cbi_ref,
                    ks_ref, bb_ref, cb_ref, we_ref, wc_ref, lt_ref):
    tc = S5_CHUNK
    fwd = pl.program_id(0) == 0
    step = jnp.exp(ls_ref[0, 0])
    ar = are_ref[0, 0]
    ai = aim_ref[0, 0]
    la = ar * step
    th = ai * step

    def powtab(m):
        mg = jnp.exp(m * la)
        an = m * th
        return mg * jnp.cos(an), mg * jnp.sin(an)

    lam_r, lam_i = powtab(1.0)
    den = ar * ar + ai * ai
    nr = lam_r - 1.0
    coef_r = (nr * ar + lam_i * ai) / den
    coef_i = (lam_i * ar - nr * ai) / den
    bp_r, bp_i = _cmul(coef_r, coef_i, bbr_ref[0, 0], bbi_ref[0, 0])
    bb_ref[0, 0] = jnp.concatenate([bp_r, bp_i], axis=-1).astype(BF16)
    cbr = cbr_ref[0, 0]
    cbi = cbi_ref[0, 0]
    cb_ref[0, 0] = jnp.concatenate([cbr, -cbi], axis=0).astype(BF16)
    jcol = lax.broadcasted_iota(jnp.int32, (tc, 1), 0).astype(F32)
    lag_r, lag_i = powtab(jcol)
    cbr_b = cbr.astype(BF16)
    cbi_b = cbi.astype(BF16)
    for m in range(tc):
        bm_r, bm_i = _cmul(bp_r, bp_i, lag_r[m:m + 1, :], lag_i[m:m + 1, :])
        k_m = (jnp.dot(bm_r.astype(BF16), cbr_b, preferred_element_type=F32)
               - jnp.dot(bm_i.astype(BF16), cbi_b, preferred_element_type=F32))
        ks_ref[0, 0, m * LANES:(m + 1) * LANES, :] = k_m.astype(BF16)
    e_r, e_i = powtab(jnp.where(fwd, (tc - 1.0) - jcol, jcol))
    c_r, c_i = powtab(jnp.where(fwd, jcol + 1.0, tc - jcol))
    we_ref[0, 0] = jnp.concatenate([e_r, e_i], axis=-1)
    wc_ref[0, 0] = jnp.concatenate([c_r, c_i], axis=-1)
    t_r, t_i = powtab(float(tc))
    lt_ref[0, 0] = jnp.concatenate([t_r, t_i], axis=-1)


def _s5_prep(a_re, a_im, log_step, b_re, b_im, c_re, c_im):
    _, g, p = a_re.shape
    c = b_re.shape[-1]
    gq = LANES // c
    nq = g // gq
    sq = gq * p
    tc = S5_CHUNK
    eye = jnp.eye(gq, dtype=F32)
    lanes_of = lambda a: a.reshape(2, nq, 1, sq)
    ls = jnp.broadcast_to(log_step[:, :, None], (2, g, p))
    bbd = lambda b: jnp.einsum("dqgpc,gh->dqgchp", b.reshape(2, nq, gq, p, c), eye).reshape(2, nq, gq * c, sq)
    cbd = lambda m: jnp.einsum("dqgcp,gh->dqgphc", m.reshape(2, nq, gq, c, p), eye).reshape(2, nq, sq, gq * c)
    blk = lambda *s: pl.BlockSpec((1, 1) + s, lambda d, q: (d, q, 0, 0))
    shp = lambda s, dt: jax.ShapeDtypeStruct((2, nq) + s, dt)
    return pl.pallas_call(
        _s5_prep_kernel,
        grid=(2, nq),
        in_specs=[blk(1, sq), blk(1, sq), blk(1, sq), blk(LANES, sq), blk(LANES, sq), blk(sq, LANES),
                  blk(sq, LANES)],
        out_specs=[blk(tc * LANES, LANES), blk(LANES, 2 * sq), blk(2 * sq, LANES), blk(tc, 2 * sq),
                   blk(tc, 2 * sq), blk(1, 2 * sq)],
        out_shape=[shp((tc * LANES, LANES), BF16), shp((LANES, 2 * sq), BF16), shp((2 * sq, LANES), BF16),
                   shp((tc, 2 * sq), F32), shp((tc, 2 * sq), F32), shp((1, 2 * sq), F32)],
        compiler_params=_cparams(("arbitrary", "arbitrary")),
        name="s5_prep",
    )(lanes_of(a_re), lanes_of(a_im), lanes_of(ls), bbd(b_re), bbd(b_im), cbd(c_re), cbd(c_im))


def _cmul(ar, ai, br, bi):
    return ar * br - ai * bi, ar * bi + ai * br


def _s5_kernel(*refs, tm, rev, has_prev):
    if has_prev:
        u_ref, prev_ref, ks_ref, bb_ref, cb_ref, we_ref, wc_ref, lt_ref, y_ref, pad_ref, st_ref = refs
    else:
        u_ref, ks_ref, bb_ref, cb_ref, we_ref, wc_ref, lt_ref, y_ref, pad_ref, st_ref = refs
    tc = S5_CHUNK
    nck = tm // tc
    nq = u_ref.shape[1] // LANES
    sq = S5_TILE_STATES

    @pl.when(pl.program_id(1) == 0)
    def _():
        st_ref[...] = jnp.zeros_like(st_ref)

    pad_ref[0:tc, :] = jnp.zeros((tc, LANES), F32)
    pad_ref[tc + tm:, :] = jnp.zeros((tc, LANES), F32)
    jpos = lax.broadcasted_iota(jnp.int32, (tm, 1), 0) % tc
    order = range(nck - 1, -1, -1) if rev else range(nck)
    for q in range(nq):
        lanes = slice(q * LANES, (q + 1) * LANES)
        u = u_ref[:, lanes]
        pad_ref[tc:tc + tm, :] = u
        cols = []
        for m in range(tc):
            if rev:
                sh = pad_ref[tc + m:tc + m + tm, :]
                keep = jpos + m < tc
            else:
                sh = pad_ref[tc - m:tc - m + tm, :]
                keep = jpos >= m
            cols.append(jnp.where(keep, sh, 0.0).astype(BF16))
        y = jnp.dot(jnp.concatenate(cols, axis=-1), ks_ref[0, q], preferred_element_type=F32)

        x = jnp.dot(u.astype(BF16), bb_ref[0, q], preferred_element_type=F32)
        we = jnp.tile(we_ref[0, q], (nck, 1))
        er, ei = _cmul(x[:, :sq], x[:, sq:], we[:, :sq], we[:, sq:])
        er = jnp.sum(er.reshape(nck, tc, sq), axis=1)
        ei = jnp.sum(ei.reshape(nck, tc, sq), axis=1)
        lam_r = lt_ref[0, q][:, :sq]
        lam_i = lt_ref[0, q][:, sq:]
        s_r = st_ref[2 * q:2 * q + 1, :]
        s_i = st_ref[2 * q + 1:2 * q + 2, :]
        ent_r, ent_i = [None] * nck, [None] * nck
        for k in order:
            ent_r[k] = jnp.broadcast_to(s_r, (tc, sq))
            ent_i[k] = jnp.broadcast_to(s_i, (tc, sq))
            n_r, n_i = _cmul(lam_r, lam_i, s_r, s_i)
            s_r = n_r + er[k:k + 1, :]
            s_i = n_i + ei[k:k + 1, :]
        st_ref[2 * q:2 * q + 1, :] = s_r
        st_ref[2 * q + 1:2 * q + 2, :] = s_i
        wc = jnp.tile(wc_ref[0, q], (nck, 1))
        z_r, z_i = _cmul(wc[:, :sq], wc[:, sq:], jnp.concatenate(ent_r, axis=0), jnp.concatenate(ent_i, axis=0))
        y = y + jnp.dot(jnp.concatenate([z_r, z_i], axis=-1).astype(BF16), cb_ref[0, q],
                        preferred_element_type=F32)
        y_ref[:, lanes] = (prev_ref[:, lanes] + y) if has_prev else y


def _s5(u_ssm, prep, *, n_batch, seq, ctx_len):
    t, dw = u_ssm.shape
    tm = ROW_TILE
    nq = dw // LANES
    tc = S5_CHUNK
    sq2 = 2 * S5_TILE_STATES
    nct, nlt = ctx_len // tm, seq // tm
    lat_tiles = n_batch * nlt
    ks, bb, cb, we, wc, lt = prep

    def tile_of(rev):
        def f(b, s):
            cpos = (nct - 1 - s) if rev else s
            lpos = (nlt - 1 - (s - nct)) if rev else (s - nct)
            return jnp.where(s < nct, lat_tiles + b * nct + cpos, b * nlt + lpos)
        return f

    y = None
    for d, rev in enumerate((False, True)):
        tile = tile_of(rev)
        row = pl.BlockSpec((tm, dw), lambda b, s: (tile(b, s), 0))
        par = lambda *shape: pl.BlockSpec((1, nq) + shape, lambda b, s: (d, 0, 0, 0))
        has_prev = y is not None
        kern = functools.partial(_s5_kernel, tm=tm, rev=rev, has_prev=has_prev)
        y = pl.pallas_call(
            kern,
            grid=(n_batch, nct + nlt),
            in_specs=[row] * (2 if has_prev else 1) + [par(tc * LANES, LANES), par(LANES, sq2), par(sq2, LANES),
                                                      par(tc, sq2), par(tc, sq2), par(1, sq2)],
            out_specs=row,
            out_shape=jax.ShapeDtypeStruct((t, dw), F32),
            scratch_shapes=[pltpu.VMEM((tm + 2 * tc, LANES), F32), pltpu.VMEM((2 * nq, S5_TILE_STATES), F32)],
            compiler_params=_cparams(("arbitrary", "arbitrary")),
            name="s5_scan",
        )(*([u_ssm] + ([y] if has_prev else []) + [ks, bb, cb, we, wc, lt]))
    return y


_NT = (((1,), (1,)), ((), ()))


def _attn_lat_kernel(q_ref, kl_ref, vl_ref, kc_ref, vc_ref, o_ref):
    q = q_ref[...]
    tq = q.shape[0]
    seq = kl_ref.shape[0]
    kc = min(ATTN_KEY_CHUNK, seq)
    chunks = [(kl_ref, vl_ref, c * kc, kc) for c in range(seq // kc)] + [(kc_ref, vc_ref, 0, kc_ref.shape[0])]
    m = jnp.full((tq, 1), NEG_BIG, F32)
    l = jnp.zeros((tq, 1), F32)
    acc = jnp.zeros((tq, V_HEAD), F32)
    for k_ref, v_ref, off, n in chunks:
        s = lax.dot_general(q, k_ref[off:off + n, :], _NT, preferred_element_type=F32)
        m_new = jnp.maximum(m, jnp.max(s, axis=-1, keepdims=True))
        alpha = jnp.exp(m - m_new)
        p = jnp.exp(s - m_new)
        l = alpha * l + jnp.sum(p, axis=-1, keepdims=True)
        acc = alpha * acc + jnp.dot(p.astype(BF16), v_ref[off:off + n, :], preferred_element_type=F32)
        m = m_new
    o_ref[...] = acc / l


def _attn_ctx_kernel(q_ref, kc_ref, vc_ref, o_ref):
    s = lax.dot_general(q_ref[...], kc_ref[...], _NT, preferred_element_type=F32)
    p = jnp.exp(s - jnp.max(s, axis=-1, keepdims=True))
    l = jnp.sum(p, axis=-1, keepdims=True)
    o_ref[...] = jnp.dot(p.astype(BF16), vc_ref[...], preferred_element_type=F32) / l


def _attention(q, k, v, *, n_batch, seq, ctx_len, ctx_queries):
    n_heads = v.shape[1] // V_HEAD
    tq = ATTN_Q_TILE
    nq = seq // tq
    cblk = n_batch * seq // ctx_len
    y_lat = pl.pallas_call(
        _attn_lat_kernel,
        grid=(n_batch, n_heads, nq),
        in_specs=[
            pl.BlockSpec((tq, QK_PAD), lambda b, h, i: (b * nq + i, h)),
            pl.BlockSpec((seq, QK_PAD), lambda b, h, i: (b, h)),
            pl.BlockSpec((seq, V_HEAD), lambda b, h, i: (b, h)),
            pl.BlockSpec((ctx_len, QK_PAD), lambda b, h, i: (cblk + b, h)),
            pl.BlockSpec((ctx_len, V_HEAD), lambda b, h, i: (cblk + b, h)),
        ],
        out_specs=pl.BlockSpec((tq, V_HEAD), lambda b, h, i: (b * nq + i, h)),
        out_shape=jax.ShapeDtypeStruct((n_batch * seq, n_heads * V_HEAD), F32),
        compiler_params=_cparams(("arbitrary", "arbitrary", "arbitrary")),
        name="attn_latent",
    )(q, k, v, k, v)
    if not ctx_queries:
        return y_lat, None
    y_ctx = pl.pallas_call(
        _attn_ctx_kernel,
        grid=(n_batch, n_heads),
        in_specs=[
            pl.BlockSpec((ctx_len, QK_PAD), lambda b, h: (cblk + b, h)),
            pl.BlockSpec((ctx_len, QK_PAD), lambda b, h: (cblk + b, h)),
            pl.BlockSpec((ctx_len, V_HEAD), lambda b, h: (cblk + b, h)),
        ],
        out_specs=pl.BlockSpec((ctx_len, V_HEAD), lambda b, h: (b, h)),
        out_shape=jax.ShapeDtypeStruct((n_batch * ctx_len, n_heads * V_HEAD), F32),
        compiler_params=_cparams(("arbitrary", "arbitrary")),
        name="attn_context",
    )(q, k, v)
    return y_lat, y_ctx


def _merge_kernel(x_ref, yp_ref, ys_ref, us_ref, yml_ref, *rest, tiles_per_seq, n_batch, d_model, has_ctx):
    if has_ctx:
        ymc_ref, rest = rest[0], rest[1:]
    (mod_ref, sd_ref, wglu_ref, bglu_ref, on_ref, wout_ref, nffn_ref, wr_ref, br_ref,
     x1_ref, h2_ref, idx_ref, gate_ref, rank_ref, cnt_ref, carry_ref) = rest
    d = d_model
    d4 = d // 4
    i = pl.program_id(0)
    tm = x_ref.shape[0]
    r = jnp.minimum(i // tiles_per_seq, n_batch)
    g1 = mod_ref[pl.ds(r, 1), pl.ds(2 * d, d)]
    sh2 = mod_ref[pl.ds(r, 1), pl.ds(3 * d, d)]
    sc2 = mod_ref[pl.ds(r, 1), pl.ds(4 * d, d)]

    z = ys_ref[...] + sd_ref[...] * us_ref[...]
    g = jax.nn.gelu(z, approximate=True)
    glu = jnp.dot(g.astype(BF16), wglu_ref[...], preferred_element_type=F32) + bglu_ref[...]
    y_ssm = g * jax.nn.sigmoid(glu)
    on = on_ref[...]
    y_mla = yml_ref[...]
    if has_ctx:
        y_mla = jnp.where(i < n_batch * tiles_per_seq, y_mla, ymc_ref[...])
    cat = jnp.concatenate([_rms(yp_ref[...], on[:, 0:d4]), _rms(y_ssm, on[:, d4:2 * d4]),
                           _rms(y_mla, on[:, 2 * d4:])], axis=-1).astype(BF16)
    x1 = x_ref[...] + g1 * jnp.dot(cat, wout_ref[...], preferred_element_type=F32)
    x1_ref[...] = x1
    h2 = (_rms(x1, nffn_ref[...]) * (1.0 + sc2) + sh2).astype(BF16)
    h2_bits = lax.bitcast_convert_type(h2.astype(F32), jnp.uint32)
    h2_ref[...] = (h2_bits[:, :d // 2] >> 16) | (h2_bits[:, d // 2:] & jnp.uint32(0xFFFF0000))

    logits = jnp.dot(h2, wr_ref[...], preferred_element_type=F32) + br_ref[...]
    lane = lax.broadcasted_iota(jnp.int32, (tm, LANES), 1)
    vals, hots = [], []
    idx_out = jnp.zeros((tm, LANES), jnp.int32)
    for kk in range(TOP_K):
        mx = jnp.max(logits, axis=-1, keepdims=True)
        sel = jnp.min(jnp.where(logits == mx, lane, LANES), axis=-1, keepdims=True)
        hot = lane == sel
        vals.append(mx)
        hots.append(hot)
        idx_out = jnp.where(lane == kk, sel, idx_out)
        logits = jnp.where(hot, NEG_BIG * 2.0, logits)
    es = [jnp.exp(vv - vals[0]) for vv in vals]
    den = es[0] + es[1] + es[2] + es[3]
    gate_out = jnp.zeros((tm, LANES), F32)
    for kk in range(TOP_K):
        gate_out = jnp.where(lane == kk, es[kk] / den, gate_out)
    idx_ref[...] = idx_out
    gate_ref[...] = gate_out

    @pl.when(i == 0)
    def _():
        carry_ref[...] = jnp.zeros_like(carry_ref)

    onehot = (hots[0] | hots[1] | hots[2] | hots[3]).astype(BF16)
    rr = lax.broadcasted_iota(jnp.int32, (tm, tm), 0)
    cc = lax.broadcasted_iota(jnp.int32, (tm, tm), 1)
    tri = (cc < rr).astype(BF16)
    pos = jnp.dot(tri, onehot, preferred_element_type=F32) + carry_ref[...]
    rank_out = jnp.zeros((tm, LANES), jnp.int32)
    for kk in range(TOP_K):
        rk = jnp.sum(jnp.where(hots[kk], pos, 0.0), axis=-1, keepdims=True).astype(jnp.int32)
        rank_out = jnp.where(lane == kk, rk, rank_out)
    rank_ref[...] = rank_out
    carry = carry_ref[...] + jnp.sum(onehot.astype(F32), axis=0, keepdims=True)
    carry_ref[...] = carry
    cnt_ref[...] = carry.astype(jnp.int32)


def _merge(xt, y_pool, y_s5, u_ssm, y_mla_lat, y_mla_ctx, mods, ssm_d, wglu, bglu, out_norm, wout, nffn, wr, br, *,
           n_rows, n_batch, seq):
    d = xt.shape[1]
    tm = ROW_TILE
    n_lat_tiles = n_batch * seq // tm
    has_ctx = y_mla_ctx is not None
    row = lambda w: pl.BlockSpec((tm, w), lambda i: (i, 0))
    kern = functools.partial(_merge_kernel, tiles_per_seq=seq // tm, n_batch=n_batch, d_model=d, has_ctx=has_ctx)
    small = [mods, ssm_d, wglu, bglu, out_norm, wout, nffn, wr, br]
    y_mla = [y_mla_lat] + ([y_mla_ctx] if has_ctx else [])
    y_specs = [pl.BlockSpec((tm, d // 2), lambda i: (jnp.minimum(i, n_lat_tiles - 1), 0))]
    if has_ctx:
        y_specs.append(pl.BlockSpec((tm, d // 2), lambda i: (jnp.maximum(i - n_lat_tiles, 0), 0)))
    return pl.pallas_call(
        kern,
        grid=(n_rows // tm,),
        in_specs=[row(d), row(d // 4), row(d // 4), row(d // 4)] + y_specs + [_full(a.shape) for a in small],
        out_specs=[row(d), row(d // 2), row(LANES), row(LANES), row(LANES), _full((1, LANES))],
        out_shape=[jax.ShapeDtypeStruct((n_rows, d), F32), jax.ShapeDtypeStruct((n_rows, d // 2), jnp.uint32),
                   jax.ShapeDtypeStruct((n_rows, LANES), jnp.int32), jax.ShapeDtypeStruct((n_rows, LANES), F32),
                   jax.ShapeDtypeStruct((n_rows, LANES), jnp.int32), jax.ShapeDtypeStruct((1, LANES), jnp.int32)],
        scratch_shapes=[pltpu.VMEM((1, LANES), F32)],
        compiler_params=_cparams(("arbitrary",)),
        name="merge_router",
    )(xt, y_pool, y_s5, u_ssm, *y_mla, *small)


def _row_copy(src, dst, sem, src_row, dst_row):
    return pltpu.make_async_copy(src.at[pl.ds(src_row, 1), :], dst.at[pl.ds(dst_row, 1), :], sem)


def _dispatch_kernel(dest_hbm, h2_ref, zeros_hbm, xs_hbm, dest_smem, sem_idx, sem_rows, *, tm):
    del zeros_hbm
    i = pl.program_id(0)
    n = tm * TOP_K
    idx_copy = pltpu.make_async_copy(dest_hbm.at[i], dest_smem, sem_idx)
    idx_copy.start()
    idx_copy.wait()

    def issue(r, _):
        for kk in range(TOP_K):
            _row_copy(h2_ref, xs_hbm, sem_rows, r, dest_smem[r * TOP_K + kk]).start(priority=kk % 2)
        return 0

    lax.fori_loop(0, tm, issue, 0, unroll=2)

    def drain(j, _):
        _row_copy(h2_ref, xs_hbm, sem_rows, j // TOP_K, dest_smem[j]).wait()
        return 0

    lax.fori_loop(0, n, drain, 0, unroll=8)


def _dispatch(dest_tiles, h2_packed, n_sorted_rows):
    n_tok, w = h2_packed.shape
    tm = ROW_TILE
    zeros = jnp.zeros((n_sorted_rows, w), h2_packed.dtype)
    return pl.pallas_call(
        functools.partial(_dispatch_kernel, tm=tm),
        grid=(n_tok // tm,),
        in_specs=[pl.BlockSpec(memory_space=pl.ANY), pl.BlockSpec((tm, w), lambda i: (i, 0)),
                  pl.BlockSpec(memory_space=pl.ANY)],
        out_specs=pl.BlockSpec(memory_space=pl.ANY),
        out_shape=jax.ShapeDtypeStruct(zeros.shape, zeros.dtype),
        input_output_aliases={2: 0},
        scratch_shapes=[pltpu.SMEM((tm * TOP_K,), jnp.int32), pltpu.SemaphoreType.DMA, pltpu.SemaphoreType.DMA],
        compiler_params=_cparams(("arbitrary",)),
        name="moe_dispatch",
    )(dest_tiles, h2_packed, zeros)


def _expert_kernel(ie_ref, ix_ref, io_ref, ns_ref, nz_ref, x_ref, wg_ref, bg_ref, wu_ref, bu_ref, wd_ref, bd_ref,
                   o_hbm, act_ref, xb_ref, res_ref, zero_ref, sem_out, sem_zero, *, nf):
    del ie_ref, ix_ref
    it = pl.program_id(0)
    j = pl.program_id(1)
    ns = ns_ref[it]
    rows_max = x_ref.shape[0]
    half = x_ref.shape[1]
    tw = res_ref.shape[2]
    sub = EXPERT_SUB_ROWS
    o_row = pl.multiple_of(io_ref[it] * sub, sub)
    n = j - nf

    def out_copy(step, mr):
        col = pl.multiple_of(step * tw, tw)
        return pltpu.make_async_copy(res_ref.at[step % 2, 0:mr, :], o_hbm.at[pl.ds(o_row, mr), pl.ds(col, tw)],
                                     sem_out.at[step % 2])

    @pl.when((ns > 0) & (j == 0))
    def _():
        words = x_ref[...]
        xb_ref[:, 0:half] = lax.bitcast_convert_type(words << 16, F32).astype(BF16)
        xb_ref[:, half:] = lax.bitcast_convert_type(words & jnp.uint32(0xFFFF0000), F32).astype(BF16)

    for m in range(1, rows_max // EXPERT_SUB_ROWS + 1):
        mr = m * EXPERT_SUB_ROWS

        @pl.when((ns == m) & (j < nf))
        def _():
            x = xb_ref[0:mr, :]
            g = jnp.dot(x, wg_ref[0, 0].astype(BF16), preferred_element_type=F32) + bg_ref[0, 0]
            u = jnp.dot(x, wu_ref[0, 0].astype(BF16), preferred_element_type=F32) + bu_ref[0, 0]
            g = jnp.minimum(g, SWIGLU_LIMIT)
            u = jnp.clip(u, -SWIGLU_LIMIT, SWIGLU_LIMIT)
            act_ref[j, 0:mr, :] = (g * jax.nn.sigmoid(SWIGLU_ALPHA * g) * (u + 1.0)).astype(BF16)

        @pl.when((ns == m) & (j >= nf))
        def _():
            @pl.when(n >= 2)
            def _():
                out_copy(n - 2, mr).wait()

            act = jnp.concatenate([act_ref[f, 0:mr, :] for f in range(nf)], axis=-1)
            res = jnp.dot(act, wd_ref[0, 0].astype(BF16), preferred_element_type=F32) + bd_ref[0, 0]
            bits = lax.bitcast_convert_type(res.astype(BF16).astype(F32), jnp.uint32)
            res_ref[n % 2, 0:mr, :] = (bits[:, :tw] >> 16) | (bits[:, tw:] & jnp.uint32(0xFFFF0000))
            out_copy(n, mr).start()

            @pl.when(n == nf - 1)
            def _():
                out_copy(n - 1, mr).wait()
                out_copy(n, mr).wait()

    nz = nz_ref[it]

    @pl.when((nz > 0) & (j == 2 * nf - 1))
    def _():
        zero_ref[...] = jnp.zeros(zero_ref.shape, jnp.uint32)

        def zero_copy(k):
            rows0 = pl.multiple_of(o_row + k * sub, sub)
            return pltpu.make_async_copy(zero_ref, o_hbm.at[pl.ds(rows0, sub), :], sem_zero)

        def start(k, _):
            zero_copy(k).start()
            return 0

        def wait(k, _):
            zero_copy(k).wait()
            return 0

        lax.fori_loop(0, nz, start, 0)
        lax.fori_loop(0, nz, wait, 0)


def _experts(layer, items, xs, w_gate, b_gate, w_up, b_up, w_down, b_down):
    item_e, item_x, item_o, item_ns, item_nz = items
    n_items = item_e.shape[0]
    n_layers, n_exp, d, f_dim = w_gate.shape
    tf = EXPERT_TILE
    nf = f_dim // tf
    rows = EXPERT_ITEM_ROWS
    assert d // tf == nf and nf >= 2

    def fi(j, ns, it):
        return jnp.where(ns[it] > 0, jnp.minimum(j, nf - 1), nf - 1)

    def ni(j, ns, it):
        return jnp.where(ns[it] > 0, jnp.maximum(j - nf, 0), nf - 1)

    return pl.pallas_call(
        functools.partial(_expert_kernel, nf=nf),
        grid_spec=pltpu.PrefetchScalarGridSpec(
            num_scalar_prefetch=5,
            grid=(n_items, 2 * nf),
            in_specs=[
                pl.BlockSpec((pl.Element(rows), pl.Element(d // 2)),
                             lambda it, j, ie, ix, io, ns, wk: (ix[it] * EXPERT_SUB_ROWS, 0),
                             pipeline_mode=pl.Buffered(1)),
                pl.BlockSpec((1, 1, d, tf), lambda it, j, ie, ix, io, ns, wk: (layer, ie[it], 0, fi(j, ns, it))),
                pl.BlockSpec((1, 1, 1, tf), lambda it, j, ie, ix, io, ns, wk: (layer, ie[it], 0, fi(j, ns, it))),
                pl.BlockSpec((1, 1, d, tf), lambda it, j, ie, ix, io, ns, wk: (layer, ie[it], 0, fi(j, ns, it))),
                pl.BlockSpec((1, 1, 1, tf), lambda it, j, ie, ix, io, ns, wk: (layer, ie[it], 0, fi(j, ns, it))),
                pl.BlockSpec((1, 1, f_dim, tf), lambda it, j, ie, ix, io, ns, wk: (layer, ie[it], 0, ni(j, ns, it))),
                pl.BlockSpec((1, 1, 1, tf), lambda it, j, ie, ix, io, ns, wk: (layer, ie[it], 0, ni(j, ns, it))),
            ],
            out_specs=pl.BlockSpec(memory_space=pl.ANY),
            scratch_shapes=[pltpu.VMEM((nf, rows, tf), BF16), pltpu.VMEM((rows, d), BF16),
                            pltpu.VMEM((2, rows, tf // 2), jnp.uint32),
                            pltpu.VMEM((EXPERT_SUB_ROWS, d // 2), jnp.uint32),
                            pltpu.SemaphoreType.DMA((2,)), pltpu.SemaphoreType.DMA],
        ),
        out_shape=jax.ShapeDtypeStruct((xs.shape[0], d // 2), jnp.uint32),
        compiler_params=_cparams(("arbitrary", "arbitrary"), EXPERT_VMEM_LIMIT),
        name="moe_experts",
    )(item_e, item_x, item_o, item_ns, item_nz, xs, w_gate, b_gate.reshape(n_layers, n_exp, 1, f_dim), w_up,
      b_up.reshape(n_layers, n_exp, 1, f_dim), w_down, b_down.reshape(n_layers, n_exp, 1, d))


def _combine_kernel(dest_hbm, os_hbm, gate_ref, x_ref, mod_ref, o_ref, dest_smem, buf, sem_idx, sem_rows,
                    *, tc, n_tiles, tiles_per_seq, n_batch):
    i = pl.program_id(0)
    n = tc * TOP_K
    slot = i % 2
    d = x_ref.shape[1]

    def gather(s, j):
        return _row_copy(os_hbm, buf.at[s, j % TOP_K], sem_rows.at[s], dest_smem[s, j], j // TOP_K)

    def fetch(tile, s):
        idx_copy = pltpu.make_async_copy(dest_hbm.at[tile], dest_smem.at[s], sem_idx)
        idx_copy.start()
        idx_copy.wait()

        def issue(r, _):
            for kk in range(TOP_K):
                gather(s, r * TOP_K + kk).start(priority=kk % 2)
            return 0

        lax.fori_loop(0, tc, issue, 0, unroll=2)

    @pl.when(i == 0)
    def _():
        fetch(0, 0)

    @pl.when(i + 1 < n_tiles)
    def _():
        fetch(i + 1, 1 - slot)

    def drain(j, _):
        gather(slot, j).wait()
        return 0

    lax.fori_loop(0, n, drain, 0, unroll=8)
    g2 = mod_ref[pl.ds(jnp.minimum(i // tiles_per_seq, n_batch), 1), pl.ds(5 * d, d)]
    gates = gate_ref[...]
    lo = hi = None
    for kk in range(TOP_K):
        words = buf[slot, kk]
        g = gates[:, kk:kk + 1]
        lo_k = g * lax.bitcast_convert_type(words << 16, F32)
        hi_k = g * lax.bitcast_convert_type(words & jnp.uint32(0xFFFF0000), F32)
        lo = lo_k if lo is None else lo + lo_k
        hi = hi_k if hi is None else hi + hi_k
    tw = EXPERT_TILE // 2
    pieces = []
    for c in range(0, d // 2, tw):
        pieces += [lo[:, c:c + tw], hi[:, c:c + tw]]
    o_ref[...] = x_ref[...] + g2 * jnp.concatenate(pieces, axis=-1)


def _combine(dest_tiles, out_sorted, gates, x1, mods, *, n_batch, seq):
    n_tok, d = x1.shape
    tc = COMBINE_TILE
    kern = functools.partial(_combine_kernel, tc=tc, n_tiles=n_tok // tc, tiles_per_seq=seq // tc, n_batch=n_batch)
    return pl.pallas_call(
        kern,
        grid=(n_tok // tc,),
        in_specs=[pl.BlockSpec(memory_space=pl.ANY)] * 2 + [
            pl.BlockSpec((tc, LANES), lambda i: (i, 0)), pl.BlockSpec((tc, d), lambda i: (i, 0)),
            _full(mods.shape)],
        out_specs=pl.BlockSpec((tc, d), lambda i: (i, 0)),
        out_shape=jax.ShapeDtypeStruct(x1.shape, F32),
        scratch_shapes=[pltpu.SMEM((2, tc * TOP_K), jnp.int32), pltpu.VMEM((2, TOP_K, tc, d // 2), jnp.uint32),
                        pltpu.SemaphoreType.DMA, pltpu.SemaphoreType.DMA((2,))],
        compiler_params=_cparams(("arbitrary",)),
        name="moe_combine",
    )(dest_tiles, out_sorted, gates, x1, mods)


def _rope_tables(n_batch, seq, ctx_len):
    rows = seq // GRID_W
    row = jnp.broadcast_to(jnp.arange(rows)[:, None], (rows, GRID_W)).reshape(-1).astype(F32)
    col = jnp.broadcast_to(jnp.arange(GRID_W)[None, :], (rows, GRID_W)).reshape(-1).astype(F32)
    half = QK_ROPE // 2
    inv_freq = ROPE_BASE ** (-jnp.arange(0, half, 2, dtype=F32) / half)
    ang = jnp.concatenate([row[:, None] * inv_freq, col[:, None] * inv_freq], axis=-1)
    cos, sin = jnp.cos(ang), jnp.sin(ang)
    zpad = jnp.zeros((seq, LANES - QK_ROPE), F32)
    cos_l = jnp.concatenate([cos, cos, zpad], axis=-1)
    sin_l = jnp.concatenate([-sin, sin, zpad], axis=-1)
    n_ctx = n_batch * ctx_len
    cos_c = jnp.concatenate([jnp.ones((n_ctx, QK_ROPE), F32), jnp.zeros((n_ctx, LANES - QK_ROPE), F32)], axis=-1)
    cos_t = jnp.concatenate([jnp.tile(cos_l, (n_batch, 1)), cos_c], axis=0)
    sin_t = jnp.concatenate([jnp.tile(sin_l, (n_batch, 1)), jnp.zeros((n_ctx, LANES), F32)], axis=0)
    return cos_t, sin_t


def _pad_head_vec(vec):
    return jnp.concatenate([vec, jnp.zeros((QK_PAD - QK_HEAD,), vec.dtype)])[None, :]


def _routing(idx, rank, counts, n_tok, n_exp):
    rows, sub = EXPERT_ITEM_ROWS, EXPERT_SUB_ROWS
    n = n_tok * TOP_K
    cap = n + n_exp * sub + rows
    n_items = -(-n // rows) + n_exp + 1
    i32 = lambda a: a.astype(jnp.int32)
    cnt = counts[0, :n_exp]
    seg = ((cnt + sub - 1) // sub) * sub
    seg_end = jnp.cumsum(seg)
    seg_start = seg_end - seg
    dest = seg_start[idx] + rank
    used_end = seg_end[-1]
    per = (cnt + rows - 1) // rows
    it_end = jnp.cumsum(per)
    it_start = it_end - per
    total = it_end[-1]
    it = jnp.arange(n_items, dtype=jnp.int32)
    it_c = jnp.minimum(it, total - 1)
    e_of = i32(jnp.minimum(jnp.searchsorted(it_end, it_c, side="right"), n_exp - 1))
    k_in = it_c - it_start[e_of]
    x_row = seg_start[e_of] + k_in * rows
    nsub = jnp.clip((cnt[e_of] - k_in * rows + sub - 1) // sub, 0, rows // sub)
    active = it < total
    nsub = jnp.where(active, nsub, 0)
    first_idle = it == total
    o_row = jnp.where(active, x_row, jnp.where(first_idle, used_end, 0))
    n_zero = jnp.where(first_idle, (cap - used_end) // sub, 0)
    return i32(dest), (e_of, i32(x_row // sub), i32(o_row // sub), i32(nsub), i32(n_zero)), cap


def _layer(layer, xt, mods, cos_t, sin_t, p, expert_params, *, n_batch, seq, ctx_len, ctx_out):
    t, d = xt.shape
    d4 = d // 4
    n_heads = (d - 2 * d4) // V_HEAD
    kv_lora = d // 8
    bf = lambda a: a.astype(BF16)
    row = lambda a: a[None, :]

    win = bf(jnp.pad(p["w_in"], ((0, 0), (0, LANES - QK_ROPE))))
    wuq = p["w_uq"].reshape(-1, n_heads, QK_HEAD)
    wuq = bf(jnp.pad(wuq, ((0, 0), (0, 0), (0, QK_PAD - QK_HEAD))).reshape(-1, n_heads * QK_PAD))
    wukv = p["w_ukv"].reshape(kv_lora, n_heads, QK_NOPE + V_HEAD)
    wk = bf(wukv[:, :, :QK_NOPE].reshape(kv_lora, n_heads * QK_NOPE))
    wv = bf(wukv[:, :, QK_NOPE:].reshape(kv_lora, n_heads * V_HEAD))
    qn = _pad_head_vec(p["q_norm"]) * (QK_HEAD ** -0.5)
    kn = _pad_head_vec(p["k_norm"])

    u_pool, u_ssm, q, k, v = _proj(xt, mods, cos_t, sin_t, row(p["norm_mix"]), win, row(p["q_a_norm"]), wuq,
                                   row(p["kv_a_norm"]), wk, wv, qn, kn, n_batch=n_batch, seq=seq)

    y_pool = _pool(u_pool, bf(p["w_pool"]), row(p["pool_scale"]), n_batch=n_batch, seq=seq, ctx_len=ctx_len)

    prep = _s5_prep(p["ssm_a_re"], p["ssm_a_im"], p["ssm_log_step"], p["ssm_b_re"], p["ssm_b_im"],
                    p["ssm_c_re"], p["ssm_c_im"])
    y_s5 = _s5(u_ssm, prep, n_batch=n_batch, seq=seq, ctx_len=ctx_len)

    y_mla_lat, y_mla_ctx = _attention(q, k, v, n_batch=n_batch, seq=seq, ctx_len=ctx_len, ctx_queries=ctx_out)

    n_rows = t if ctx_out else n_batch * seq
    n_exp = p["w_router"].shape[1]
    wr = bf(jnp.pad(p["w_router"], ((0, 0), (0, LANES - n_exp))))
    br = jnp.concatenate([p["b_router"], jnp.full((LANES - n_exp,), NEG_BIG, F32)])[None, :]
    x1, h2, idx, gates, rank, counts = _merge(
        xt, y_pool, y_s5, u_ssm, y_mla_lat, y_mla_ctx, mods, row(p["ssm_d"]), bf(p["w_glu"]), row(p["b_glu"]),
        row(p["out_norm"]), bf(p["w_out"]), row(p["norm_ffn"]), wr, br, n_rows=n_rows, n_batch=n_batch, seq=seq)

    dest, items, cap = _routing(idx[:, :TOP_K], rank[:, :TOP_K], counts, n_rows, n_exp)
    xs = _dispatch(dest.reshape(n_rows // ROW_TILE, ROW_TILE * TOP_K), h2, cap)
    out_sorted = _experts(layer, items, xs, *expert_params)
    return _combine(dest.reshape(n_rows // COMBINE_TILE, COMBINE_TILE * TOP_K), out_sorted, gates, x1, mods,
                    n_batch=n_batch, seq=seq)


_PARAM_NAMES = ("norm_mix", "norm_ffn", "w_in", "w_pool", "pool_scale", "ssm_a_re", "ssm_a_im", "ssm_log_step",
                "ssm_b_re", "ssm_b_im", "ssm_c_re", "ssm_c_im", "ssm_d", "w_glu", "b_glu", "q_a_norm", "w_uq",
                "kv_a_norm", "w_ukv", "q_norm", "k_norm", "out_norm", "w_out", "w_router", "b_router")


def kernel(x, c, ctx, c_ctx, w_ada, b_ada, norm_mix, norm_ffn, w_in, w_pool, pool_scale, ssm_a_re, ssm_a_im, ssm_log_step, ssm_b_re, ssm_b_im, ssm_c_re, ssm_c_im, ssm_d, w_glu, b_glu, q_a_norm, w_uq, kv_a_norm, w_ukv, q_norm, k_norm, out_norm, w_out, w_router, b_router, w_gate, b_gate, w_up, b_up, w_down, b_down):
    n_batch, seq, d = x.shape
    ctx_len = ctx.shape[1]
    depth = w_ada.shape[0]
    assert seq % ROW_TILE == 0 and ctx_len % ROW_TILE == 0 and seq % GRID_W == 0 and n_batch + 1 <= 8
    stacked = dict(zip(_PARAM_NAMES, (norm_mix, norm_ffn, w_in, w_pool, pool_scale, ssm_a_re, ssm_a_im,
                                      ssm_log_step, ssm_b_re, ssm_b_im, ssm_c_re, ssm_c_im, ssm_d, w_glu, b_glu,
                                      q_a_norm, w_uq, kv_a_norm, w_ukv, q_norm, k_norm, out_norm, w_out, w_router,
                                      b_router)))
    expert_params = (w_gate, b_gate, w_up, b_up, w_down, b_down)
    cvecs = jnp.concatenate([c, c_ctx[None, :], jnp.zeros((8 - n_batch - 1, d), F32)], axis=0)
    mods = _ada(cvecs, w_ada, b_ada)
    cos_t, sin_t = _rope_tables(n_batch, seq, ctx_len)
    n_lat = n_batch * seq
    xt = jnp.concatenate([x.reshape(n_lat, d), ctx.reshape(n_batch * ctx_len, d)], axis=0)
    for layer in range(depth):
        p = {name: val[layer] for name, val in stacked.items()}
        xt = _layer(layer, xt, mods[layer], cos_t, sin_t, p, expert_params, n_batch=n_batch, seq=seq,
                    ctx_len=ctx_len, ctx_out=layer < depth - 1)
    return xt[:n_lat].reshape(n_batch, seq, d)
```

```python
import functools
import math

import jax
import jax.numpy as jnp
from jax import lax
from jax.experimental import pallas as pl
from jax.experimental.pallas import tpu as pltpu

F32 = jnp.float32
BF16 = jnp.bfloat16

EPS = 1e-6
GRID_W = 64
POOL_WINDOWS = (2, 4, 8, 16)
POOL_HALO = 8
SSM_GROUP = 16
SSM_STATE = 64
S5_CHUNK = 16
S5_TILE_STATES = (128 // SSM_GROUP) * SSM_STATE
QK_NOPE = 128
QK_ROPE = 64
QK_HEAD = QK_NOPE + QK_ROPE
QK_PAD = 256
V_HEAD = 128
ROPE_BASE = 10000.0
TOP_K = 4
SWIGLU_ALPHA = 1.702
SWIGLU_LIMIT = 7.0

LANES = 128
ROW_TILE = 256
ATTN_Q_TILE = 512
ATTN_KEY_CHUNK = 1024
EXPERT_ITEM_ROWS = 1280
EXPERT_SUB_ROWS = 256
EXPERT_TILE = 512
COMBINE_TILE = 256
VMEM_LIMIT = 52 * 1024 * 1024
EXPERT_VMEM_LIMIT = 57 * 1024 * 1024
NEG_BIG = -1e30


def _cparams(sem, vmem=VMEM_LIMIT):
    return pltpu.CompilerParams(dimension_semantics=sem, vmem_limit_bytes=vmem)


def _rms(x, w):
    return x * lax.rsqrt(jnp.mean(x * x, axis=-1, keepdims=True) + EPS) * w


def _full(shape):
    nd = len(shape)
    return pl.BlockSpec(shape, lambda *_: (0,) * nd)


def _ada_kernel(cv_ref, w_ref, b_ref, o_ref):
    cv = cv_ref[...]
    s = (cv * jax.nn.sigmoid(cv)).astype(BF16)
    o_ref[0] = jnp.dot(s, w_ref[0].astype(BF16), preferred_element_type=F32) + b_ref[0]


def _ada(cvecs, w_ada, b_ada):
    n_layers, d, n6 = w_ada.shape
    tn = 512
    return pl.pallas_call(
        _ada_kernel,
        grid=(n_layers, n6 // tn),
        in_specs=[
            pl.BlockSpec((8, d), lambda l, j: (0, 0)),
            pl.BlockSpec((1, d, tn), lambda l, j: (l, 0, j)),
            pl.BlockSpec((1, 1, tn), lambda l, j: (l, 0, j)),
        ],
        out_specs=pl.BlockSpec((1, 8, tn), lambda l, j: (l, 0, j)),
        out_shape=jax.ShapeDtypeStruct((n_layers, 8, n6), F32),
        compiler_params=_cparams(("arbitrary", "arbitrary")),
        name="ada",
    )(cvecs, w_ada, b_ada.reshape(n_layers, 1, n6))


def _rope128(r, cos, sin):
    lane = lax.broadcasted_iota(jnp.int32, r.shape, 1)
    sw = jnp.where((lane % QK_ROPE) < QK_ROPE // 2, pltpu.roll(r, LANES - 32, 1), pltpu.roll(r, 32, 1))
    return r * cos + sw * sin


def _proj_kernel(x_ref, mod_ref, cos_ref, sin_ref, nmix_ref, win_ref, qan_ref, wuq_ref, kvan_ref,
                 wk_ref, wv_ref, qn_ref, kn_ref,
                 upool_ref, ussm_ref, q_ref, k_ref, v_ref, *, tiles_per_seq, n_batch, n_heads, d_model):
    d = d_model
    r = jnp.minimum(pl.program_id(0) // tiles_per_seq, n_batch)
    sh = mod_ref[pl.ds(r, 1), pl.ds(0, d)]
    sc = mod_ref[pl.ds(r, 1), pl.ds(d, d)]
    h = _rms(x_ref[...], nmix_ref[...]) * (1.0 + sc) + sh
    proj = jnp.dot(h.astype(BF16), win_ref[...], preferred_element_type=F32)
    d4 = d // 4
    upool_ref[...] = proj[:, 0:d4]
    ussm_ref[...] = proj[:, 2 * d4:3 * d4]
    cos = cos_ref[...]
    sin = sin_ref[...]

    qa = _rms(proj[:, d4:2 * d4], qan_ref[...]).astype(BF16)
    qf = jnp.dot(qa, wuq_ref[...], preferred_element_type=F32)
    qn_w = qn_ref[...]
    for hd in range(n_heads):
        blk = qf[:, hd * QK_PAD:(hd + 1) * QK_PAD]
        ss = jnp.sum(blk * blk, axis=-1, keepdims=True) * (1.0 / QK_HEAD)
        qn = blk * lax.rsqrt(ss + EPS) * qn_w
        q_ref[:, hd * QK_PAD:hd * QK_PAD + LANES] = qn[:, :LANES].astype(BF16)
        q_ref[:, hd * QK_PAD + LANES:(hd + 1) * QK_PAD] = _rope128(qn[:, LANES:], cos, sin).astype(BF16)

    kv_lo = 3 * d4
    kv_w = d // 8
    ka = _rms(proj[:, kv_lo:kv_lo + kv_w], kvan_ref[...]).astype(BF16)
    kn_all = jnp.dot(ka, wk_ref[...], preferred_element_type=F32)
    v_ref[...] = jnp.dot(ka, wv_ref[...], preferred_element_type=F32).astype(BF16)
    krp = proj[:, kv_lo + kv_w:kv_lo + kv_w + LANES]
    kr_ss = jnp.sum(krp * krp, axis=-1, keepdims=True)
    kn_w = kn_ref[...]
    for hd in range(n_heads):
        kn = kn_all[:, hd * QK_NOPE:(hd + 1) * QK_NOPE]
        ss = (jnp.sum(kn * kn, axis=-1, keepdims=True) + kr_ss) * (1.0 / QK_HEAD)
        rinv = lax.rsqrt(ss + EPS)
        k_ref[:, hd * QK_PAD:hd * QK_PAD + LANES] = (kn * rinv * kn_w[:, :LANES]).astype(BF16)
        k_ref[:, hd * QK_PAD + LANES:(hd + 1) * QK_PAD] = _rope128(
            krp * rinv * kn_w[:, LANES:], cos, sin).astype(BF16)


def _proj(xt, mods, cos_t, sin_t, nmix, win, qan, wuq, kvan, wk, wv, qn, kn, *, n_batch, seq):
    t, d = xt.shape
    tm = ROW_TILE
    n_heads = wv.shape[1] // V_HEAD
    row = lambda w: pl.BlockSpec((tm, w), lambda i: (i, 0))
    kern = functools.partial(_proj_kernel, tiles_per_seq=seq // tm, n_batch=n_batch,
                             n_heads=n_heads, d_model=d)
    return pl.pallas_call(
        kern,
        grid=(t // tm,),
        in_specs=[row(d), _full(mods.shape), row(LANES), row(LANES), _full(nmix.shape), _full(win.shape),
                  _full(qan.shape), _full(wuq.shape), _full(kvan.shape), _full(wk.shape), _full(wv.shape),
                  _full(qn.shape), _full(kn.shape)],
        out_specs=[row(d // 4), row(d // 4), row(n_heads * QK_PAD), row(n_heads * QK_PAD),
                   row(n_heads * V_HEAD)],
        out_shape=[jax.ShapeDtypeStruct((t, d // 4), F32), jax.ShapeDtypeStruct((t, d // 4), F32),
                   jax.ShapeDtypeStruct((t, n_heads * QK_PAD), BF16),
                   jax.ShapeDtypeStruct((t, n_heads * QK_PAD), BF16),
                   jax.ShapeDtypeStruct((t, n_heads * V_HEAD), BF16)],
        compiler_params=_cparams(("arbitrary",)),
        name="mixer_in",
    )(xt, mods, cos_t, sin_t, nmix, win, qan, wuq, kvan, wk, wv, qn, kn)


def _pool_kernel(prev_ref, cur_ref, next_ref, w_ref, scale_ref, o_ref, pad_ref, *,
                 tm, n_lat_tiles, seq, ctx_len):
    i = pl.program_id(0)
    hw = POOL_HALO
    pad_ref[0:hw, :] = prev_ref[...]
    pad_ref[hw:hw + tm, :] = cur_ref[...]
    pad_ref[hw + tm:2 * hw + tm, :] = next_ref[...]
    is_lat = i < n_lat_tiles
    row0 = i * tm
    s0 = jnp.where(is_lat, (row0 // seq) * seq,
                   n_lat_tiles * tm + ((row0 - n_lat_tiles * tm) // ctx_len) * ctx_len)
    slen = jnp.where(is_lat, seq, ctx_len)
    t = row0 - s0 + lax.broadcasted_iota(jnp.int32, (tm, 1), 0)
    for g, w in enumerate(POOL_WINDOWS):
        lanes = slice(g * LANES, (g + 1) * LANES)
        acc = jnp.zeros((tm, LANES), F32)
        for kk in range(-(w // 2), w - w // 2):
            valid = (t + kk >= 0) & (t + kk < slen)
            acc = acc + jnp.where(valid, pad_ref[hw + kk:hw + kk + tm, lanes], 0.0)
        lo = jnp.maximum(t - w // 2, 0)
        hi = jnp.minimum(t + (w - w // 2), slen)
        mean = acc / (hi - lo).astype(F32)
        dlt = (mean - cur_ref[:, lanes]).astype(BF16)
        o_ref[:, lanes] = jnp.dot(dlt, w_ref[g], preferred_element_type=F32) * scale_ref[:, lanes]


def _pool(u_pool, w_pool, pool_scale, *, n_batch, seq, ctx_len):
    t, dp = u_pool.shape
    tm = ROW_TILE
    hb = tm // POOL_HALO
    n_hblocks = t // POOL_HALO
    kern = functools.partial(_pool_kernel, tm=tm, n_lat_tiles=n_batch * seq // tm, seq=seq, ctx_len=ctx_len)
    return pl.pallas_call(
        kern,
        grid=(t // tm,),
        in_specs=[
            pl.BlockSpec((POOL_HALO, dp), lambda i: (jnp.maximum(i * hb - 1, 0), 0)),
            pl.BlockSpec((tm, dp), lambda i: (i, 0)),
            pl.BlockSpec((POOL_HALO, dp), lambda i: (jnp.minimum((i + 1) * hb, n_hblocks - 1), 0)),
            _full(w_pool.shape), _full(pool_scale.shape),
        ],
        out_specs=pl.BlockSpec((tm, dp), lambda i: (i, 0)),
        out_shape=jax.ShapeDtypeStruct((t, dp), F32),
        scratch_shapes=[pltpu.VMEM((tm + 2 * POOL_HALO, dp), F32)],
        compiler_params=_cparams(("arbitrary",)),
        name="pool",
    )(u_pool, u_pool, u_pool, w_pool, pool_scale)


def _s5_prep_kernel(are_ref, aim_ref, ls_ref, bbr_ref, bbi_ref, cbr_ref, cbi_ref,
                    ks_ref, bb_ref, cb_ref, we_ref, wc_ref, lt_ref):
    tc = S5_CHUNK
    fwd = pl.program_id(0) == 0
    step = jnp.exp(ls_ref[0, 0])
    ar = are_ref[0, 0]
    ai = aim_ref[0, 0]
    la = ar * step
    th = ai * step

    def powtab(m):
        mg = jnp.exp(m * la)
        an = m * th
        return mg * jnp.cos(an), mg * jnp.sin(an)

    lam_r, lam_i = powtab(1.0)
    den = ar * ar + ai * ai
    nr = lam_r - 1.0
    coef_r = (nr * ar + lam_i * ai) / den
    coef_i = (lam_i * ar - nr * ai) / den
    bp_r, bp_i = _cmul(coef_r, coef_i, bbr_ref[0, 0], bbi_ref[0, 0])
    bb_ref[0, 0] = jnp.concatenate([bp_r, bp_i], axis=-1).astype(BF16)
    cbr = cbr_ref[0, 0]
    cbi = cbi_ref[0, 0]
    cb_ref[0, 0] = jnp.concatenate([cbr, -cbi], axis=0).astype(BF16)
    jcol = lax.broadcasted_iota(jnp.int32, (tc, 1), 0).astype(F32)
    lag_r, lag_i = powtab(jcol)
    cbr_b = cbr.astype(BF16)
    cbi_b = cbi.astype(BF16)
    for m in range(tc):
        bm_r, bm_i = _cmul(bp_r, bp_i, lag_r[m:m + 1, :], lag_i[m:m + 1, :])
        k_m = (jnp.dot(bm_r.astype(BF16), cbr_b, preferred_element_type=F32)
               - jnp.dot(bm_i.astype(BF16), cbi_b, preferred_element_type=F32))
        ks_ref[0, 0, m * LANES:(m + 1) * LANES, :] = k_m.astype(BF16)
    e_r, e_i = powtab(jnp.where(fwd, (tc - 1.0) - jcol, jcol))
    c_r, c_i = powtab(jnp.where(fwd, jcol + 1.0, tc - jcol))
    we_ref[0, 0] = jnp.concatenate([e_r, e_i], axis=-1)
    wc_ref[0, 0] = jnp.concatenate([c_r, c_i], axis=-1)
    t_r, t_i = powtab(float(tc))
    lt_ref[0, 0] = jnp.concatenate([t_r, t_i], axis=-1)


def _s5_prep(a_re, a_im, log_step, b_re, b_im, c_re, c_im):
    _, g, p = a_re.shape
    c = b_re.shape[-1]
    gq = LANES // c
    nq = g // gq
    sq = gq * p
    tc = S5_CHUNK
    eye = jnp.eye(gq, dtype=F32)
    lanes_of = lambda a: a.reshape(2, nq, 1, sq)
    ls = jnp.broadcast_to(log_step[:, :, None], (2, g, p))
    bbd = lambda b: jnp.einsum("dqgpc,gh->dqgchp", b.reshape(2, nq, gq, p, c), eye).reshape(2, nq, gq * c, sq)
    cbd = lambda m: jnp.einsum("dqgcp,gh->dqgphc", m.reshape(2, nq, gq, c, p), eye).reshape(2, nq, sq, gq * c)
    blk = lambda *s: pl.BlockSpec((1, 1) + s, lambda d, q: (d, q, 0, 0))
    shp = lambda s, dt: jax.ShapeDtypeStruct((2, nq) + s, dt)
    return pl.pallas_call(
        _s5_prep_kernel,
        grid=(2, nq),
        in_specs=[blk(1, sq), blk(1, sq), blk(1, sq), blk(LANES, sq), blk(LANES, sq), blk(sq, LANES),
                  blk(sq, LANES)],
        out_specs=[blk(tc * LANES, LANES), blk(LANES, 2 * sq), blk(2 * sq, LANES), blk(tc, 2 * sq),
                   blk(tc, 2 * sq), blk(1, 2 * sq)],
        out_shape=[shp((tc * LANES, LANES), BF16), shp((LANES, 2 * sq), BF16), shp((2 * sq, LANES), BF16),
                   shp((tc, 2 * sq), F32), shp((tc, 2 * sq), F32), shp((1, 2 * sq), F32)],
        compiler_params=_cparams(("arbitrary", "arbitrary")),
        name="s5_prep",
    )(lanes_of(a_re), lanes_of(a_im), lanes_of(ls), bbd(b_re), bbd(b_im), cbd(c_re), cbd(c_im))


def _cmul(ar, ai, br, bi):
    return ar * br - ai * bi, ar * bi + ai * br


def _s5_kernel(*refs, tm, rev, has_prev):
    if has_prev:
        u_ref, prev_ref, ks_ref, bb_ref, cb_ref, we_ref, wc_ref, lt_ref, y_ref, pad_ref, st_ref = refs
    else:
        u_ref, ks_ref, bb_ref, cb_ref, we_ref, wc_ref, lt_ref, y_ref, pad_ref, st_ref = refs
    tc = S5_CHUNK
    nck = tm // tc
    nq = u_ref.shape[1] // LANES
    sq = S5_TILE_STATES

    @pl.when(pl.program_id(1) == 0)
    def _():
        st_ref[...] = jnp.zeros_like(st_ref)

    pad_ref[0:tc, :] = jnp.zeros((tc, LANES), F32)
    pad_ref[tc + tm:, :] = jnp.zeros((tc, LANES), F32)
    jpos = lax.broadcasted_iota(jnp.int32, (tm, 1), 0) % tc
    order = range(nck - 1, -1, -1) if rev else range(nck)
    for q in range(nq):
        lanes = slice(q * LANES, (q + 1) * LANES)
        u = u_ref[:, lanes]
        pad_ref[tc:tc + tm, :] = u
        cols = []
        for m in range(tc):
            if rev:
                sh = pad_ref[tc + m:tc + m + tm, :]
                keep = jpos + m < tc
            else:
                sh = pad_ref[tc - m:tc - m + tm, :]
                keep = jpos >= m
            cols.append(jnp.where(keep, sh, 0.0).astype(BF16))
        y = jnp.dot(jnp.concatenate(cols, axis=-1), ks_ref[0, q], preferred_element_type=F32)

        x = jnp.dot(u.astype(BF16), bb_ref[0, q], preferred_element_type=F32)
        we = jnp.tile(we_ref[0, q], (nck, 1))
        er, ei = _cmul(x[:, :sq], x[:, sq:], we[:, :sq], we[:, sq:])
        er = jnp.sum(er.reshape(nck, tc, sq), axis=1)
        ei = jnp.sum(ei.reshape(nck, tc, sq), axis=1)
        lam_r = lt_ref[0, q][:, :sq]
        lam_i = lt_ref[0, q][:, sq:]
        s_r = st_ref[2 * q:2 * q + 1, :]
        s_i = st_ref[2 * q + 1:2 * q + 2, :]
        ent_r, ent_i = [None] * nck, [None] * nck
        for k in order:
            ent_r[k] = jnp.broadcast_to(s_r, (tc, sq))
            ent_i[k] = jnp.broadcast_to(s_i, (tc, sq))
            n_r, n_i = _cmul(lam_r, lam_i, s_r, s_i)
            s_r = n_r + er[k:k + 1, :]
            s_i = n_i + ei[k:k + 1, :]
        st_ref[2 * q:2 * q + 1, :] = s_r
        st_ref[2 * q + 1:2 * q + 2, :] = s_i
        wc = jnp.tile(wc_ref[0, q], (nck, 1))
        z_r, z_i = _cmul(wc[:, :sq], wc[:, sq:], jnp.concatenate(ent_r, axis=0), jnp.concatenate(ent_i, axis=0))
        y = y + jnp.dot(jnp.concatenate([z_r, z_i], axis=-1).astype(BF16), cb_ref[0, q],
                        preferred_element_type=F32)
        y_ref[:, lanes] = (prev_ref[:, lanes] + y) if has_prev else y


def _s5(u_ssm, prep, *, n_batch, seq, ctx_len):
    t, dw = u_ssm.shape
    tm = ROW_TILE
    nq = dw // LANES
    tc = S5_CHUNK
    sq2 = 2 * S5_TILE_STATES
    nct, nlt = ctx_len // tm, seq // tm
    lat_tiles = n_batch * nlt
    ks, bb, cb, we, wc, lt = prep

    def tile_of(rev):
        def f(b, s):
            cpos = (nct - 1 - s) if rev else s
            lpos = (nlt - 1 - (s - nct)) if rev else (s - nct)
            return jnp.where(s < nct, lat_tiles + b * nct + cpos, b * nlt + lpos)
        return f

    y = None
    for d, rev in enumerate((False, True)):
        tile = tile_of(rev)
        row = pl.BlockSpec((tm, dw), lambda b, s: (tile(b, s), 0))
        par = lambda *shape: pl.BlockSpec((1, nq) + shape, lambda b, s: (d, 0, 0, 0))
        has_prev = y is not None
        kern = functools.partial(_s5_kernel, tm=tm, rev=rev, has_prev=has_prev)
        y = pl.pallas_call(
            kern,
            grid=(n_batch, nct + nlt),
            in_specs=[row] * (2 if has_prev else 1) + [par(tc * LANES, LANES), par(LANES, sq2), par(sq2, LANES),
                                                      par(tc, sq2), par(tc, sq2), par(1, sq2)],
            out_specs=row,
            out_shape=jax.ShapeDtypeStruct((t, dw), F32),
            scratch_shapes=[pltpu.VMEM((tm + 2 * tc, LANES), F32), pltpu.VMEM((2 * nq, S5_TILE_STATES), F32)],
            compiler_params=_cparams(("arbitrary", "arbitrary")),
            name="s5_scan",
        )(*([u_ssm] + ([y] if has_prev else []) + [ks, bb, cb, we, wc, lt]))
    return y


_NT = (((1,), (1,)), ((), ()))


def _attn_lat_kernel(q_ref, kl_ref, vl_ref, kc_ref, vc_ref, o_ref):
    q = q_ref[...]
    tq = q.shape[0]
    seq = kl_ref.shape[0]
    kc = min(ATTN_KEY_CHUNK, seq)
    chunks = [(kl_ref, vl_ref, c * kc, kc) for c in range(seq // kc)] + [(kc_ref, vc_ref, 0, kc_ref.shape[0])]
    m = jnp.full((tq, 1), NEG_BIG, F32)
    l = jnp.zeros((tq, 1), F32)
    acc = jnp.zeros((tq, V_HEAD), F32)
    for k_ref, v_ref, off, n in chunks:
        s = lax.dot_general(q, k_ref[off:off + n, :], _NT, preferred_element_type=F32)
        m_new = jnp.maximum(m, jnp.max(s, axis=-1, keepdims=True))
        alpha = jnp.exp(m - m_new)
        p = jnp.exp(s - m_new)
        l = alpha * l + jnp.sum(p, axis=-1, keepdims=True)
        acc = alpha * acc + jnp.dot(p.astype(BF16), v_ref[off:off + n, :], preferred_element_type=F32)
        m = m_new
    o_ref[...] = acc / l


def _attn_ctx_kernel(q_ref, kc_ref, vc_ref, o_ref):
    s = lax.dot_general(q_ref[...], kc_ref[...], _NT, preferred_element_type=F32)
    p = jnp.exp(s - jnp.max(s, axis=-1, keepdims=True))
    l = jnp.sum(p, axis=-1, keepdims=True)
    o_ref[...] = jnp.dot(p.astype(BF16), vc_ref[...], preferred_element_type=F32) / l


def _attention(q, k, v, *, n_batch, seq, ctx_len, ctx_queries):
    n_heads = v.shape[1] // V_HEAD
    tq = ATTN_Q_TILE
    nq = seq // tq
    cblk = n_batch * seq // ctx_len
    y_lat = pl.pallas_call(
        _attn_lat_kernel,
        grid=(n_batch, n_heads, nq),
        in_specs=[
            pl.BlockSpec((tq, QK_PAD), lambda b, h, i: (b * nq + i, h)),
            pl.BlockSpec((seq, QK_PAD), lambda b, h, i: (b, h)),
            pl.BlockSpec((seq, V_HEAD), lambda b, h, i: (b, h)),
            pl.BlockSpec((ctx_len, QK_PAD), lambda b, h, i: (cblk + b, h)),
            pl.BlockSpec((ctx_len, V_HEAD), lambda b, h, i: (cblk + b, h)),
        ],
        out_specs=pl.BlockSpec((tq, V_HEAD), lambda b, h, i: (b * nq + i, h)),
        out_shape=jax.ShapeDtypeStruct((n_batch * seq, n_heads * V_HEAD), F32),
        compiler_params=_cparams(("arbitrary", "arbitrary", "arbitrary")),
        name="attn_latent",
    )(q, k, v, k, v)
    if not ctx_queries:
        return y_lat, None
    y_ctx = pl.pallas_call(
        _attn_ctx_kernel,
        grid=(n_batch, n_heads),
        in_specs=[
            pl.BlockSpec((ctx_len, QK_PAD), lambda b, h: (cblk + b, h)),
            pl.BlockSpec((ctx_len, QK_PAD), lambda b, h: (cblk + b, h)),
            pl.BlockSpec((ctx_len, V_HEAD), lambda b, h: (cblk + b, h)),
        ],
        out_specs=pl.BlockSpec((ctx_len, V_HEAD), lambda b, h: (b, h)),
        out_shape=jax.ShapeDtypeStruct((n_batch * ctx_len, n_heads * V_HEAD), F32),
        compiler_params=_cparams(("arbitrary", "arbitrary")),
        name="attn_context",
    )(q, k, v)
    return y_lat, y_ctx


def _merge_kernel(x_ref, yp_ref, ys_ref, us_ref, yml_ref, *rest, tiles_per_seq, n_batch, d_model, has_ctx):
    if has_ctx:
        ymc_ref, rest = rest[0], rest[1:]
    (mod_ref, sd_ref, wglu_ref, bglu_ref, on_ref, wout_ref, nffn_ref, wr_ref, br_ref,
     x1_ref, h2_ref, idx_ref, gate_ref, rank_ref, cnt_ref, carry_ref) = rest
    d = d_model
    d4 = d // 4
    i = pl.program_id(0)
    tm = x_ref.shape[0]
    r = jnp.minimum(i // tiles_per_seq, n_batch)
    g1 = mod_ref[pl.ds(r, 1), pl.ds(2 * d, d)]
    sh2 = mod_ref[pl.ds(r, 1), pl.ds(3 * d, d)]
    sc2 = mod_ref[pl.ds(r, 1), pl.ds(4 * d, d)]

    z = ys_ref[...] + sd_ref[...] * us_ref[...]
    g = jax.nn.gelu(z, approximate=True)
    glu = jnp.dot(g.astype(BF16), wglu_ref[...], preferred_element_type=F32) + bglu_ref[...]
    y_ssm = g * jax.nn.sigmoid(glu)
    on = on_ref[...]
    y_mla = yml_ref[...]
    if has_ctx:
        y_mla = jnp.where(i < n_batch * tiles_per_seq, y_mla, ymc_ref[...])
    cat = jnp.concatenate([_rms(yp_ref[...], on[:, 0:d4]), _rms(y_ssm, on[:, d4:2 * d4]),
                           _rms(y_mla, on[:, 2 * d4:])], axis=-1).astype(BF16)
    x1 = x_ref[...] + g1 * jnp.dot(cat, wout_ref[...], preferred_element_type=F32)
    x1_ref[...] = x1
    h2 = (_rms(x1, nffn_ref[...]) * (1.0 + sc2) + sh2).astype(BF16)
    h2_bits = lax.bitcast_convert_type(h2.astype(F32), jnp.uint32)
    h2_ref[...] = (h2_bits[:, :d // 2] >> 16) | (h2_bits[:, d // 2:] & jnp.uint32(0xFFFF0000))

    logits = jnp.dot(h2, wr_ref[...], preferred_element_type=F32) + br_ref[...]
    lane = lax.broadcasted_iota(jnp.int32, (tm, LANES), 1)
    vals, hots = [], []
    idx_out = jnp.zeros((tm, LANES), jnp.int32)
    for kk in range(TOP_K):
        mx = jnp.max(logits, axis=-1, keepdims=True)
        sel = jnp.min(jnp.where(logits == mx, lane, LANES), axis=-1, keepdims=True)
        hot = lane == sel
        vals.append(mx)
        hots.append(hot)
        idx_out = jnp.where(lane == kk, sel, idx_out)
        logits = jnp.where(hot, NEG_BIG * 2.0, logits)
    es = [jnp.exp(vv - vals[0]) for vv in vals]
    den = es[0] + es[1] + es[2] + es[3]
    gate_out = jnp.zeros((tm, LANES), F32)
    for kk in range(TOP_K):
        gate_out = jnp.where(lane == kk, es[kk] / den, gate_out)
    idx_ref[...] = idx_out
    gate_ref[...] = gate_out

    @pl.when(i == 0)
    def _():
        carry_ref[...] = jnp.zeros_like(carry_ref)

    onehot = (hots[0] | hots[1] | hots[2] | hots[3]).astype(BF16)
    rr = lax.broadcasted_iota(jnp.int32, (tm, tm), 0)
    cc = lax.broadcasted_iota(jnp.int32, (tm, tm), 1)
    tri = (cc < rr).astype(BF16)
    pos = jnp.dot(tri, onehot, preferred_element_type=F32) + carry_ref[...]
    rank_out = jnp.zeros((tm, LANES), jnp.int32)
    for kk in range(TOP_K):
        rk = jnp.sum(jnp.where(hots[kk], pos, 0.0), axis=-1, keepdims=True).astype(jnp.int32)
        rank_out = jnp.where(lane == kk, rk, rank_out)
    rank_ref[...] = rank_out
    carry = carry_ref[...] + jnp.sum(onehot.astype(F32), axis=0, keepdims=True)
    carry_ref[...] = carry
    cnt_ref[...] = carry.astype(jnp.int32)


def _merge(xt, y_pool, y_s5, u_ssm, y_mla_lat, y_mla_ctx, mods, ssm_d, wglu, bglu, out_norm, wout, nffn, wr, br, *,
           n_rows, n_batch, seq):
    d = xt.shape[1]
    tm = ROW_TILE
    n_lat_tiles = n_batch * seq // tm
    has_ctx = y_mla_ctx is not None
    row = lambda w: pl.BlockSpec((tm, w), lambda i: (i, 0))
    kern = functools.partial(_merge_kernel, tiles_per_seq=seq // tm, n_batch=n_batch, d_model=d, has_ctx=has_ctx)
    small = [mods, ssm_d, wglu, bglu, out_norm, wout, nffn, wr, br]
    y_mla = [y_mla_lat] + ([y_mla_ctx] if has_ctx else [])
    y_specs = [pl.BlockSpec((tm, d // 2), lambda i: (jnp.minimum(i, n_lat_tiles - 1), 0))]
    if has_ctx:
        y_specs.append(pl.BlockSpec((tm, d // 2), lambda i: (jnp.maximum(i - n_lat_tiles, 0), 0)))
    return pl.pallas_call(
        kern,
        grid=(n_rows // tm,),
        in_specs=[row(d), row(d // 4), row(d // 4), row(d // 4)] + y_specs + [_full(a.shape) for a in small],
        out_specs=[row(d), row(d // 2), row(LANES), row(LANES), row(LANES), _full((1, LANES))],
        out_shape=[jax.ShapeDtypeStruct((n_rows, d), F32), jax.ShapeDtypeStruct((n_rows, d // 2), jnp.uint32),
                   jax.ShapeDtypeStruct((n_rows, LANES), jnp.int32), jax.ShapeDtypeStruct((n_rows, LANES), F32),
                   jax.ShapeDtypeStruct((n_rows, LANES), jnp.int32), jax.ShapeDtypeStruct((1, LANES), jnp.int32)],
        scratch_shapes=[pltpu.VMEM((1, LANES), F32)],
        compiler_params=_cparams(("arbitrary",)),
        name="merge_router",
    )(xt, y_pool, y_s5, u_ssm, *y_mla, *small)


def _row_copy(src, dst, sem, src_row, dst_row):
    return pltpu.make_async_copy(src.at[pl.ds(src_row, 1), :], dst.at[pl.ds(dst_row, 1), :], sem)


def _dispatch_kernel(dest_hbm, h2_ref, zeros_hbm, xs_hbm, dest_smem, sem_idx, sem_rows, *, tm):
    del zeros_hbm
    i = pl.program_id(0)
    n = tm * TOP_K
    idx_copy = pltpu.make_async_copy(dest_hbm.at[i], dest_smem, sem_idx)
    idx_copy.start()
    idx_copy.wait()

    def issue(r, _):
        for kk in range(TOP_K):
            _row_copy(h2_ref, xs_hbm, sem_rows, r, dest_smem[r * TOP_K + kk]).start(priority=kk % 2)
        return 0

    lax.fori_loop(0, tm, issue, 0, unroll=2)

    def drain(r, _):
        for kk in range(TOP_K):
            _row_copy(h2_ref, xs_hbm, sem_rows, r, dest_smem[r * TOP_K + kk]).wait()
        return 0

    lax.fori_loop(0, tm, drain, 0, unroll=2)


def _dispatch(dest_tiles, h2_packed, n_sorted_rows):
    n_tok, w = h2_packed.shape
    tm = ROW_TILE
    zeros = jnp.zeros((n_sorted_rows, w), h2_packed.dtype)
    return pl.pallas_call(
        functools.partial(_dispatch_kernel, tm=tm),
        grid=(n_tok // tm,),
        in_specs=[pl.BlockSpec(memory_space=pl.ANY), pl.BlockSpec((tm, w), lambda i: (i, 0)),
                  pl.BlockSpec(memory_space=pl.ANY)],
        out_specs=pl.BlockSpec(memory_space=pl.ANY),
        out_shape=jax.ShapeDtypeStruct(zeros.shape, zeros.dtype),
        input_output_aliases={2: 0},
        scratch_shapes=[pltpu.SMEM((tm * TOP_K,), jnp.int32), pltpu.SemaphoreType.DMA, pltpu.SemaphoreType.DMA],
        compiler_params=_cparams(("arbitrary",)),
        name="moe_dispatch",
    )(dest_tiles, h2_packed, zeros)


def _expert_kernel(ie_ref, ix_ref, io_ref, ns_ref, nz_ref, x_ref, wg_ref, bg_ref, wu_ref, bu_ref, wd_ref, bd_ref,
                   o_hbm, act_ref, xb_ref, res_ref, zero_ref, sem_out, sem_zero, *, nf):
    del ie_ref, ix_ref
    it = pl.program_id(0)
    j = pl.program_id(1)
    ns = ns_ref[it]
    rows_max = x_ref.shape[0]
    half = x_ref.shape[1]
    tw = res_ref.shape[2]
    sub = EXPERT_SUB_ROWS
    o_row = pl.multiple_of(io_ref[it] * sub, sub)
    n = j - nf

    def out_copy(step, mr):
        col = pl.multiple_of(step * tw, tw)
        return pltpu.make_async_copy(res_ref.at[step % 2, 0:mr, :], o_hbm.at[pl.ds(o_row, mr), pl.ds(col, tw)],
                                     sem_out.at[step % 2])

    @pl.when((ns > 0) & (j == 0))
    def _():
        words = x_ref[...]
        xb_ref[:, 0:half] = lax.bitcast_convert_type(words << 16, F32).astype(BF16)
        xb_ref[:, half:] = lax.bitcast_convert_type(words & jnp.uint32(0xFFFF0000), F32).astype(BF16)

    for m in range(1, rows_max // EXPERT_SUB_ROWS + 1):
        mr = m * EXPERT_SUB_ROWS

        @pl.when((ns == m) & (j < nf))
        def _():
            x = xb_ref[0:mr, :]
            g = jnp.dot(x, wg_ref[0, 0].astype(BF16), preferred_element_type=F32) + bg_ref[0, 0]
            u = jnp.dot(x, wu_ref[0, 0].astype(BF16), preferred_element_type=F32) + bu_ref[0, 0]
            g = jnp.minimum(g, SWIGLU_LIMIT)
            u = jnp.clip(u, -SWIGLU_LIMIT, SWIGLU_LIMIT)
            act_ref[j, 0:mr, :] = (g * jax.nn.sigmoid(SWIGLU_ALPHA * g) * (u + 1.0)).astype(BF16)

        @pl.when((ns == m) & (j >= nf))
        def _():
            @pl.when(n >= 2)
            def _():
                out_copy(n - 2, mr).wait()

            act = jnp.concatenate([act_ref[f, 0:mr, :] for f in range(nf)], axis=-1)
            res = jnp.dot(act, wd_ref[0, 0].astype(BF16), preferred_element_type=F32) + bd_ref[0, 0]
            bits = lax.bitcast_convert_type(res.astype(BF16).astype(F32), jnp.uint32)
            res_ref[n % 2, 0:mr, :] = (bits[:, :tw] >> 16) | (bits[:, tw:] & jnp.uint32(0xFFFF0000))
            out_copy(n, mr).start()

            @pl.when(n == nf - 1)
            def _():
                out_copy(n - 1, mr).wait()
                out_copy(n, mr).wait()

    nz = nz_ref[it]

    @pl.when((nz > 0) & (j == 2 * nf - 1))
    def _():
        zero_ref[...] = jnp.zeros(zero_ref.shape, jnp.uint32)

        def zero_copy(k):
            rows0 = pl.multiple_of(o_row + k * sub, sub)
            return pltpu.make_async_copy(zero_ref, o_hbm.at[pl.ds(rows0, sub), :], sem_zero)

        def start(k, _):
            zero_copy(k).start()
            return 0

        def wait(k, _):
            zero_copy(k).wait()
            return 0

        lax.fori_loop(0, nz, start, 0)
        lax.fori_loop(0, nz, wait, 0)


def _experts(layer, items, xs, w_gate, b_gate, w_up, b_up, w_down, b_down):
    item_e, item_x, item_o, item_ns, item_nz = items
    n_items = item_e.shape[0]
    n_layers, n_exp, d, f_dim = w_gate.shape
    tf = EXPERT_TILE
    nf = f_dim // tf
    rows = EXPERT_ITEM_ROWS
    assert d // tf == nf and nf >= 2

    def fi(j, ns, it):
        return jnp.where(ns[it] > 0, jnp.minimum(j, nf - 1), nf - 1)

    def ni(j, ns, it):
        return jnp.where(ns[it] > 0, jnp.maximum(j - nf, 0), nf - 1)

    return pl.pallas_call(
        functools.partial(_expert_kernel, nf=nf),
        grid_spec=pltpu.PrefetchScalarGridSpec(
            num_scalar_prefetch=5,
            grid=(n_items, 2 * nf),
            in_specs=[
                pl.BlockSpec((pl.Element(rows), pl.Element(d // 2)),
                             lambda it, j, ie, ix, io, ns, wk: (ix[it] * EXPERT_SUB_ROWS, 0),
                             pipeline_mode=pl.Buffered(1)),
                pl.BlockSpec((1, 1, d, tf), lambda it, j, ie, ix, io, ns, wk: (layer, ie[it], 0, fi(j, ns, it))),
                pl.BlockSpec((1, 1, 1, tf), lambda it, j, ie, ix, io, ns, wk: (layer, ie[it], 0, fi(j, ns, it))),
                pl.BlockSpec((1, 1, d, tf), lambda it, j, ie, ix, io, ns, wk: (layer, ie[it], 0, fi(j, ns, it))),
                pl.BlockSpec((1, 1, 1, tf), lambda it, j, ie, ix, io, ns, wk: (layer, ie[it], 0, fi(j, ns, it))),
                pl.BlockSpec((1, 1, f_dim, tf), lambda it, j, ie, ix, io, ns, wk: (layer, ie[it], 0, ni(j, ns, it))),
                pl.BlockSpec((1, 1, 1, tf), lambda it, j, ie, ix, io, ns, wk: (layer, ie[it], 0, ni(j, ns, it))),
            ],
            out_specs=pl.BlockSpec(memory_space=pl.ANY),
            scratch_shapes=[pltpu.VMEM((nf, rows, tf), BF16), pltpu.VMEM((rows, d), BF16),
                            pltpu.VMEM((2, rows, tf // 2), jnp.uint32),
                            pltpu.VMEM((EXPERT_SUB_ROWS, d // 2), jnp.uint32),
                            pltpu.SemaphoreType.DMA((2,)), pltpu.SemaphoreType.DMA],
        ),
        out_shape=jax.ShapeDtypeStruct((xs.shape[0], d // 2), jnp.uint32),
        compiler_params=_cparams(("arbitrary", "arbitrary"), EXPERT_VMEM_LIMIT),
        name="moe_experts",
    )(item_e, item_x, item_o, item_ns, item_nz, xs, w_gate, b_gate.reshape(n_layers, n_exp, 1, f_dim), w_up,
      b_up.reshape(n_layers, n_exp, 1, f_dim), w_down, b_down.reshape(n_layers, n_exp, 1, d))


def _combine_kernel(dest_hbm, os_hbm, gate_ref, x_ref, mod_ref, o_ref, dest_smem, buf, sem_idx, sem_rows,
                    *, tc, n_tiles, tiles_per_seq, n_batch):
    i = pl.program_id(0)
    n = tc * TOP_K
    slot = i % 2
    d = x_ref.shape[1]

    def gather(s, r, kk):
        return _row_copy(os_hbm, buf.at[s, kk], sem_rows.at[s], dest_smem[s, r * TOP_K + kk], r)

    def fetch(tile, s):
        idx_copy = pltpu.make_async_copy(dest_hbm.at[tile], dest_smem.at[s], sem_idx)
        idx_copy.start()
        idx_copy.wait()

        def issue(r, _):
            for kk in range(TOP_K):
                gather(s, r, kk).start(priority=kk % 2)
            return 0

        lax.fori_loop(0, tc, issue, 0, unroll=2)

    @pl.when(i == 0)
    def _():
        fetch(0, 0)

    @pl.when(i + 1 < n_tiles)
    def _():
        fetch(i + 1, 1 - slot)

    def drain(r, _):
        for kk in range(TOP_K):
            gather(slot, r, kk).wait()
        return 0

    lax.fori_loop(0, tc, drain, 0, unroll=2)
    g2 = mod_ref[pl.ds(jnp.minimum(i // tiles_per_seq, n_batch), 1), pl.ds(5 * d, d)]
    gates = gate_ref[...]
    lo = hi = None
    for kk in range(TOP_K):
        words = buf[slot, kk]
        g = gates[:, kk:kk + 1]
        lo_k = g * lax.bitcast_convert_type(words << 16, F32)
        hi_k = g * lax.bitcast_convert_type(words & jnp.uint32(0xFFFF0000), F32)
        lo = lo_k if lo is None else lo + lo_k
        hi = hi_k if hi is None else hi + hi_k
    tw = EXPERT_TILE // 2
    pieces = []
    for c in range(0, d // 2, tw):
        pieces += [lo[:, c:c + tw], hi[:, c:c + tw]]
    o_ref[...] = x_ref[...] + g2 * jnp.concatenate(pieces, axis=-1)


def _combine(dest_tiles, out_sorted, gates, x1, mods, *, n_batch, seq):
    n_tok, d = x1.shape
    tc = COMBINE_TILE
    kern = functools.partial(_combine_kernel, tc=tc, n_tiles=n_tok // tc, tiles_per_seq=seq // tc, n_batch=n_batch)
    return pl.pallas_call(
        kern,
        grid=(n_tok // tc,),
        in_specs=[pl.BlockSpec(memory_space=pl.ANY)] * 2 + [
            pl.BlockSpec((tc, LANES), lambda i: (i, 0)), pl.BlockSpec((tc, d), lambda i: (i, 0)),
            _full(mods.shape)],
        out_specs=pl.BlockSpec((tc, d), lambda i: (i, 0)),
        out_shape=jax.ShapeDtypeStruct(x1.shape, F32),
        scratch_shapes=[pltpu.SMEM((2, tc * TOP_K), jnp.int32), pltpu.VMEM((2, TOP_K, tc, d // 2), jnp.uint32),
                        pltpu.SemaphoreType.DMA, pltpu.SemaphoreType.DMA((2,))],
        compiler_params=_cparams(("arbitrary",)),
        name="moe_combine",
    )(dest_tiles, out_sorted, gates, x1, mods)


def _rope_tables(n_batch, seq, ctx_len):
    rows = seq // GRID_W
    row = jnp.broadcast_to(jnp.arange(rows)[:, None], (rows, GRID_W)).reshape(-1).astype(F32)
    col = jnp.broadcast_to(jnp.arange(GRID_W)[None, :], (rows, GRID_W)).reshape(-1).astype(F32)
    half = QK_ROPE // 2
    inv_freq = ROPE_BASE ** (-jnp.arange(0, half, 2, dtype=F32) / half)
    ang = jnp.concatenate([row[:, None] * inv_freq, col[:, None] * inv_freq], axis=-1)
    cos, sin = jnp.cos(ang), jnp.sin(ang)
    zpad = jnp.zeros((seq, LANES - QK_ROPE), F32)
    cos_l = jnp.concatenate([cos, cos, zpad], axis=-1)
    sin_l = jnp.concatenate([-sin, sin, zpad], axis=-1)
    n_ctx = n_batch * ctx_len
    cos_c = jnp.concatenate([jnp.ones((n_ctx, QK_ROPE), F32), jnp.zeros((n_ctx, LANES - QK_ROPE), F32)], axis=-1)
    cos_t = jnp.concatenate([jnp.tile(cos_l, (n_batch, 1)), cos_c], axis=0)
    sin_t = jnp.concatenate([jnp.tile(sin_l, (n_batch, 1)), jnp.zeros((n_ctx, LANES), F32)], axis=0)
    return cos_t, sin_t


def _pad_head_vec(vec):
    return jnp.concatenate([vec, jnp.zeros((QK_PAD - QK_HEAD,), vec.dtype)])[None, :]


def _routing(idx, rank, counts, n_tok, n_exp):
    rows, sub = EXPERT_ITEM_ROWS, EXPERT_SUB_ROWS
    n = n_tok * TOP_K
    cap = n + n_exp * sub + rows
    n_items = -(-n // rows) + n_exp + 1
    i32 = lambda a: a.astype(jnp.int32)
    cnt = counts[0, :n_exp]
    seg = ((cnt + sub - 1) // sub) * sub
    seg_end = jnp.cumsum(seg)
    seg_start = seg_end - seg
    dest = seg_start[idx] + rank
    used_end = seg_end[-1]
    per = (cnt + rows - 1) // rows
    it_end = jnp.cumsum(per)
    it_start = it_end - per
    total = it_end[-1]
    it = jnp.arange(n_items, dtype=jnp.int32)
    it_c = jnp.minimum(it, total - 1)
    e_of = i32(jnp.minimum(jnp.searchsorted(it_end, it_c, side="right"), n_exp - 1))
    k_in = it_c - it_start[e_of]
    x_row = seg_start[e_of] + k_in * rows
    nsub = jnp.clip((cnt[e_of] - k_in * rows + sub - 1) // sub, 0, rows // sub)
    active = it < total
    nsub = jnp.where(active, nsub, 0)
    first_idle = it == total
    o_row = jnp.where(active, x_row, jnp.where(first_idle, used_end, 0))
    n_zero = jnp.where(first_idle, (cap - used_end) // sub, 0)
    return i32(dest), (e_of, i32(x_row // sub), i32(o_row // sub), i32(nsub), i32(n_zero)), cap


def _layer(layer, xt, mods, cos_t, sin_t, p, expert_params, *, n_batch, seq, ctx_len, ctx_out):
    t, d = xt.shape
    d4 = d // 4
    n_heads = (d - 2 * d4) // V_HEAD
    kv_lora = d // 8
    bf = lambda a: a.astype(BF16)
    row = lambda a: a[None, :]

    win = bf(jnp.pad(p["w_in"], ((0, 0), (0, LANES - QK_ROPE))))
    wuq = p["w_uq"].reshape(-1, n_heads, QK_HEAD)
    wuq = bf(jnp.pad(wuq, ((0, 0), (0, 0), (0, QK_PAD - QK_HEAD))).reshape(-1, n_heads * QK_PAD))
    wukv = p["w_ukv"].reshape(kv_lora, n_heads, QK_NOPE + V_HEAD)
    wk = bf(wukv[:, :, :QK_NOPE].reshape(kv_lora, n_heads * QK_NOPE))
    wv = bf(wukv[:, :, QK_NOPE:].reshape(kv_lora, n_heads * V_HEAD))
    qn = _pad_head_vec(p["q_norm"]) * (QK_HEAD ** -0.5)
    kn = _pad_head_vec(p["k_norm"])

    u_pool, u_ssm, q, k, v = _proj(xt, mods, cos_t, sin_t, row(p["norm_mix"]), win, row(p["q_a_norm"]), wuq,
                                   row(p["kv_a_norm"]), wk, wv, qn, kn, n_batch=n_batch, seq=seq)

    y_pool = _pool(u_pool, bf(p["w_pool"]), row(p["pool_scale"]), n_batch=n_batch, seq=seq, ctx_len=ctx_len)

    prep = _s5_prep(p["ssm_a_re"], p["ssm_a_im"], p["ssm_log_step"], p["ssm_b_re"], p["ssm_b_im"],
                    p["ssm_c_re"], p["ssm_c_im"])
    y_s5 = _s5(u_ssm, prep, n_batch=n_batch, seq=seq, ctx_len=ctx_len)

    y_mla_lat, y_mla_ctx = _attention(q, k, v, n_batch=n_batch, seq=seq, ctx_len=ctx_len, ctx_queries=ctx_out)

    n_rows = t if ctx_out else n_batch * seq
    n_exp = p["w_router"].shape[1]
    wr = bf(jnp.pad(p["w_router"], ((0, 0), (0, LANES - n_exp))))
    br = jnp.concatenate([p["b_router"], jnp.full((LANES - n_exp,), NEG_BIG, F32)])[None, :]
    x1, h2, idx, gates, rank, counts = _merge(
        xt, y_pool, y_s5, u_ssm, y_mla_lat, y_mla_ctx, mods, row(p["ssm_d"]), bf(p["w_glu"]), row(p["b_glu"]),
        row(p["out_norm"]), bf(p["w_out"]), row(p["norm_ffn"]), wr, br, n_rows=n_rows, n_batch=n_batch, seq=seq)

    dest, items, cap = _routing(idx[:, :TOP_K], rank[:, :TOP_K], counts, n_rows, n_exp)
    xs = _dispatch(dest.reshape(n_rows // ROW_TILE, ROW_TILE * TOP_K), h2, cap)
    out_sorted = _experts(layer, items, xs, *expert_params)
    return _combine(dest.reshape(n_rows // COMBINE_TILE, COMBINE_TILE * TOP_K), out_sorted, gates, x1, mods,
                    n_batch=n_batch, seq=seq)


_PARAM_NAMES = ("norm_mix", "norm_ffn", "w_in", "w_pool", "pool_scale", "ssm_a_re", "ssm_a_im", "ssm_log_step",
                "ssm_b_re", "ssm_b_im", "ssm_c_re", "ssm_c_im", "ssm_d", "w_glu", "b_glu", "q_a_norm", "w_uq",
                "kv_a_norm", "w_ukv", "q_norm", "k_norm", "out_norm", "w_out", "w_router", "b_router")


def kernel(x, c, ctx, c_ctx, w_ada, b_ada, norm_mix, norm_ffn, w_in, w_pool, pool_scale, ssm_a_re, ssm_a_im, ssm_log_step, ssm_b_re, ssm_b_im, ssm_c_re, ssm_c_im, ssm_d, w_glu, b_glu, q_a_norm, w_uq, kv_a_norm, w_ukv, q_norm, k_norm, out_norm, w_out, w_router, b_router, w_gate, b_gate, w_up, b_up, w_down, b_down):
    n_batch, seq, d = x.shape
    ctx_len = ctx.shape[1]
    depth = w_ada.shape[0]
    assert seq % ROW_TILE == 0 and ctx_len % ROW_TILE == 0 and seq % GRID_W == 0 and n_batch + 1 <= 8
    stacked = dict(zip(_PARAM_NAMES, (norm_mix, norm_ffn, w_in, w_pool, pool_scale, ssm_a_re, ssm_a_im,
                                      ssm_log_step, ssm_b_re, ssm_b_im, ssm_c_re, ssm_c_im, ssm_d, w_glu, b_glu,
                                      q_a_norm, w_uq, kv_a_norm, w_ukv, q_norm, k_norm, out_norm, w_out, w_router,
                                      b_router)))
    expert_params = (w_gate, b_gate, w_up, b_up, w_down, b_down)
    cvecs = jnp.concatenate([c, c_ctx[None, :], jnp.zeros((8 - n_batch - 1, d), F32)], axis=0)
    mods = _ada(cvecs, w_ada, b_ada)
    cos_t, sin_t = _rope_tables(n_batch, seq, ctx_len)
    n_lat = n_batch * seq
    xt = jnp.concatenate([x.reshape(n_lat, d), ctx.reshape(n_batch * ctx_len, d)], axis=0)
    for layer in range(depth):
        p = {name: val[layer] for name, val in stacked.items()}
        xt = _layer(layer, xt, mods[layer], cos_t, sin_t, p, expert_params, n_batch=n_batch, seq=seq,
                    ctx_len=ctx_len, ctx_out=layer < depth - 1)
    return xt[:n_lat].reshape(n_batch, seq, d)
```

```python
import functools
import math

import jax
import jax.numpy as jnp
from jax import lax
from jax.experimental import pallas as pl
from jax.experimental.pallas import tpu as pltpu

F32 = jnp.float32
BF16 = jnp.bfloat16

EPS = 1e-6
GRID_W = 64
POOL_WINDOWS = (2, 4, 8, 16)
POOL_HALO = 8
SSM_GROUP = 16
SSM_STATE = 64
S5_CHUNK = 16
S5_TILE_STATES = (128 // SSM_GROUP) * SSM_STATE
QK_NOPE = 128
QK_ROPE = 64
QK_HEAD = QK_NOPE + QK_ROPE
QK_PAD = 256
V_HEAD = 128
ROPE_BASE = 10000.0
TOP_K = 4
SWIGLU_ALPHA = 1.702
SWIGLU_LIMIT = 7.0

LANES = 128
ROW_TILE = 256
ATTN_Q_TILE = 1024
ATTN_KEY_CHUNK = 1024
EXPERT_ITEM_ROWS = 1280
EXPERT_SUB_ROWS = 128
EXPERT_ROW_VARIANTS = (8, 9, 10)
EXPERT_TILE = 512
COMBINE_TILE = 256
VMEM_LIMIT = 52 * 1024 * 1024
EXPERT_VMEM_LIMIT = 57 * 1024 * 1024
NEG_BIG = -1e30


def _cparams(sem, vmem=VMEM_LIMIT):
    return pltpu.CompilerParams(dimension_semantics=sem, vmem_limit_bytes=vmem)


def _rms(x, w):
    return x * lax.rsqrt(jnp.mean(x * x, axis=-1, keepdims=True) + EPS) * w


def _full(shape):
    nd = len(shape)
    return pl.BlockSpec(shape, lambda *_: (0,) * nd)


def _ada_kernel(cv_ref, w_ref, b_ref, o_ref):
    cv = cv_ref[...]
    s = (cv * jax.nn.sigmoid(cv)).astype(BF16)
    o_ref[0] = jnp.dot(s, w_ref[0].astype(BF16), preferred_element_type=F32) + b_ref[0]


def _ada(cvecs, w_ada, b_ada):
    n_layers, d, n6 = w_ada.shape
    tn = 512
    return pl.pallas_call(
        _ada_kernel,
        grid=(n_layers, n6 // tn),
        in_specs=[
            pl.BlockSpec((8, d), lambda l, j: (0, 0)),
            pl.BlockSpec((1, d, tn), lambda l, j: (l, 0, j)),
            pl.BlockSpec((1, 1, tn), lambda l, j: (l, 0, j)),
        ],
        out_specs=pl.BlockSpec((1, 8, tn), lambda l, j: (l, 0, j)),
        out_shape=jax.ShapeDtypeStruct((n_layers, 8, n6), F32),
        compiler_params=_cparams(("arbitrary", "arbitrary")),
        name="ada",
    )(cvecs, w_ada, b_ada.reshape(n_layers, 1, n6))


def _rope128(r, cos, sin):
    lane = lax.broadcasted_iota(jnp.int32, r.shape, 1)
    sw = jnp.where((lane % QK_ROPE) < QK_ROPE // 2, pltpu.roll(r, LANES - 32, 1), pltpu.roll(r, 32, 1))
    return r * cos + sw * sin


def _proj_kernel(x_ref, mod_ref, cos_ref, sin_ref, nmix_ref, win_ref, qan_ref, wuq_ref, kvan_ref,
                 wk_ref, wv_ref, qn_ref, kn_ref,
                 upool_ref, ussm_ref, q_ref, k_ref, v_ref, *, tiles_per_seq, n_batch, n_heads, d_model):
    d = d_model
    r = jnp.minimum(pl.program_id(0) // tiles_per_seq, n_batch)
    sh = mod_ref[pl.ds(r, 1), pl.ds(0, d)]
    sc = mod_ref[pl.ds(r, 1), pl.ds(d, d)]
    h = _rms(x_ref[...], nmix_ref[...]) * (1.0 + sc) + sh
    proj = jnp.dot(h.astype(BF16), win_ref[...], preferred_element_type=F32)
    d4 = d // 4
    upool_ref[...] = proj[:, 0:d4]
    ussm_ref[...] = proj[:, 2 * d4:3 * d4]
    cos = cos_ref[...]
    sin = sin_ref[...]

    qa = _rms(proj[:, d4:2 * d4], qan_ref[...]).astype(BF16)
    qf = jnp.dot(qa, wuq_ref[...], preferred_element_type=F32)
    qn_w = qn_ref[...]
    for hd in range(n_heads):
        blk = qf[:, hd * QK_PAD:(hd + 1) * QK_PAD]
        ss = jnp.sum(blk * blk, axis=-1, keepdims=True) * (1.0 / QK_HEAD)
        qn = blk * lax.rsqrt(ss + EPS) * qn_w
        q_ref[:, hd * QK_PAD:hd * QK_PAD + LANES] = qn[:, :LANES].astype(BF16)
        q_ref[:, hd * QK_PAD + LANES:(hd + 1) * QK_PAD] = _rope128(qn[:, LANES:], cos, sin).astype(BF16)

    kv_lo = 3 * d4
    kv_w = d // 8
    ka = _rms(proj[:, kv_lo:kv_lo + kv_w], kvan_ref[...]).astype(BF16)
    kn_all = jnp.dot(ka, wk_ref[...], preferred_element_type=F32)
    v_ref[...] = jnp.dot(ka, wv_ref[...], preferred_element_type=F32).astype(BF16)
    krp = proj[:, kv_lo + kv_w:kv_lo + kv_w + LANES]
    kr_ss = jnp.sum(krp * krp, axis=-1, keepdims=True)
    kn_w = kn_ref[...]
    for hd in range(n_heads):
        kn = kn_all[:, hd * QK_NOPE:(hd + 1) * QK_NOPE]
        ss = (jnp.sum(kn * kn, axis=-1, keepdims=True) + kr_ss) * (1.0 / QK_HEAD)
        rinv = lax.rsqrt(ss + EPS)
        k_ref[:, hd * QK_PAD:hd * QK_PAD + LANES] = (kn * rinv * kn_w[:, :LANES]).astype(BF16)
        k_ref[:, hd * QK_PAD + LANES:(hd + 1) * QK_PAD] = _rope128(
            krp * rinv * kn_w[:, LANES:], cos, sin).astype(BF16)


def _proj(xt, mods, cos_t, sin_t, nmix, win, qan, wuq, kvan, wk, wv, qn, kn, *, n_batch, seq):
    t, d = xt.shape
    tm = ROW_TILE
    n_heads = wv.shape[1] // V_HEAD
    row = lambda w: pl.BlockSpec((tm, w), lambda i: (i, 0))
    kern = functools.partial(_proj_kernel, tiles_per_seq=seq // tm, n_batch=n_batch,
                             n_heads=n_heads, d_model=d)
    return pl.pallas_call(
        kern,
        grid=(t // tm,),
        in_specs=[row(d), _full(mods.shape), row(LANES), row(LANES), _full(nmix.shape), _full(win.shape),
                  _full(qan.shape), _full(wuq.shape), _full(kvan.shape), _full(wk.shape), _full(wv.shape),
                  _full(qn.shape), _full(kn.shape)],
        out_specs=[row(d // 4), row(d // 4), row(n_heads * QK_PAD), row(n_heads * QK_PAD),
                   row(n_heads * V_HEAD)],
        out_shape=[jax.ShapeDtypeStruct((t, d // 4), F32), jax.ShapeDtypeStruct((t, d // 4), F32),
                   jax.ShapeDtypeStruct((t, n_heads * QK_PAD), BF16),
                   jax.ShapeDtypeStruct((t, n_heads * QK_PAD), BF16),
                   jax.ShapeDtypeStruct((t, n_heads * V_HEAD), BF16)],
        compiler_params=_cparams(("arbitrary",)),
        name="mixer_in",
    )(xt, mods, cos_t, sin_t, nmix, win, qan, wuq, kvan, wk, wv, qn, kn)


def _pool_kernel(prev_ref, cur_ref, next_ref, w_ref, scale_ref, o_ref, pad_ref, *,
                 tm, n_lat_tiles, seq, ctx_len):
    i = pl.program_id(0)
    hw = POOL_HALO
    pad_ref[0:hw, :] = prev_ref[...]
    pad_ref[hw:hw + tm, :] = cur_ref[...]
    pad_ref[hw + tm:2 * hw + tm, :] = next_ref[...]
    is_lat = i < n_lat_tiles
    row0 = i * tm
    s0 = jnp.where(is_lat, (row0 // seq) * seq,
                   n_lat_tiles * tm + ((row0 - n_lat_tiles * tm) // ctx_len) * ctx_len)
    slen = jnp.where(is_lat, seq, ctx_len)
    t = row0 - s0 + lax.broadcasted_iota(jnp.int32, (tm, 1), 0)
    for g, w in enumerate(POOL_WINDOWS):
        lanes = slice(g * LANES, (g + 1) * LANES)
        acc = jnp.zeros((tm, LANES), F32)
        for kk in range(-(w // 2), w - w // 2):
            valid = (t + kk >= 0) & (t + kk < slen)
            acc = acc + jnp.where(valid, pad_ref[hw + kk:hw + kk + tm, lanes], 0.0)
        lo = jnp.maximum(t - w // 2, 0)
        hi = jnp.minimum(t + (w - w // 2), slen)
        mean = acc / (hi - lo).astype(F32)
        dlt = (mean - cur_ref[:, lanes]).astype(BF16)
        o_ref[:, lanes] = jnp.dot(dlt, w_ref[g], preferred_element_type=F32) * scale_ref[:, lanes]


def _pool(u_pool, w_pool, pool_scale, *, n_batch, seq, ctx_len):
    t, dp = u_pool.shape
    tm = ROW_TILE
    hb = tm // POOL_HALO
    n_hblocks = t // POOL_HALO
    kern = functools.partial(_pool_kernel, tm=tm, n_lat_tiles=n_batch * seq // tm, seq=seq, ctx_len=ctx_len)
    return pl.pallas_call(
        kern,
        grid=(t // tm,),
        in_specs=[
            pl.BlockSpec((POOL_HALO, dp), lambda i: (jnp.maximum(i * hb - 1, 0), 0)),
            pl.BlockSpec((tm, dp), lambda i: (i, 0)),
            pl.BlockSpec((POOL_HALO, dp), lambda i: (jnp.minimum((i + 1) * hb, n_hblocks - 1), 0)),
            _full(w_pool.shape), _full(pool_scale.shape),
        ],
        out_specs=pl.BlockSpec((tm, dp), lambda i: (i, 0)),
        out_shape=jax.ShapeDtypeStruct((t, dp), F32),
        scratch_shapes=[pltpu.VMEM((tm + 2 * POOL_HALO, dp), F32)],
        compiler_params=_cparams(("arbitrary",)),
        name="pool",
    )(u_pool, u_pool, u_pool, w_pool, pool_scale)


def _s5_prep_kernel(are_ref, aim_ref, ls_ref, bbr_ref, bbi_ref, cbr_ref, cbi_ref,
                    ks_ref, bb_ref, cb_ref, we_ref, wc_ref, lt_ref):
    tc = S5_CHUNK
    fwd = pl.program_id(0) == 0
    step = jnp.exp(ls_ref[0, 0])
    ar = are_ref[0, 0]
    ai = aim_ref[0, 0]
    la = ar * step
    th = ai * step

    def powtab(m):
        mg = jnp.exp(m * la)
        an = m * th
        return mg * jnp.cos(an), mg * jnp.sin(an)

    lam_r, lam_i = powtab(1.0)
    den = ar * ar + ai * ai
    nr = lam_r - 1.0
    coef_r = (nr * ar + lam_i * ai) / den
    coef_i = (lam_i * ar - nr * ai) / den
    bp_r, bp_i = _cmul(coef_r, coef_i, bbr_ref[0, 0], bbi_ref[0, 0])
    bb_ref[0, 0] = jnp.concatenate([bp_r, bp_i], axis=-1).astype(BF16)
    cbr = cbr_ref[0, 0]
    cbi = cbi_ref[0, 0]
    cb_ref[0, 0] = jnp.concatenate([cbr, -cbi], axis=0).astype(BF16)
    jcol = lax.broadcasted_iota(jnp.int32, (tc, 1), 0).astype(F32)
    lag_r, lag_i = powtab(jcol)
    cbr_b = cbr.astype(BF16)
    cbi_b = cbi.astype(BF16)
    for m in range(tc):
        bm_r, bm_i = _cmul(bp_r, bp_i, lag_r[m:m + 1, :], lag_i[m:m + 1, :])
        k_m = (jnp.dot(bm_r.astype(BF16), cbr_b, preferred_element_type=F32)
               - jnp.dot(bm_i.astype(BF16), cbi_b, preferred_element_type=F32))
        ks_ref[0, 0, m * LANES:(m + 1) * LANES, :] = k_m.astype(BF16)
    e_r, e_i = powtab(jnp.where(fwd, (tc - 1.0) - jcol, jcol))
    c_r, c_i = powtab(jnp.where(fwd, jcol + 1.0, tc - jcol))
    we_ref[0, 0] = jnp.concatenate([e_r, e_i], axis=-1)
    wc_ref[0, 0] = jnp.concatenate([c_r, c_i], axis=-1)
    t_r, t_i = powtab(float(tc))
    lt_ref[0, 0] = jnp.concatenate([t_r, t_i], axis=-1)


def _s5_prep(a_re, a_im, log_step, b_re, b_im, c_re, c_im):
    _, g, p = a_re.shape
    c = b_re.shape[-1]
    gq = LANES // c
    nq = g // gq
    sq = gq * p
    tc = S5_CHUNK
    eye = jnp.eye(gq, dtype=F32)
    lanes_of = lambda a: a.reshape(2, nq, 1, sq)
    ls = jnp.broadcast_to(log_step[:, :, None], (2, g, p))
    bbd = lambda b: jnp.einsum("dqgpc,gh->dqgchp", b.reshape(2, nq, gq, p, c), eye).reshape(2, nq, gq * c, sq)
    cbd = lambda m: jnp.einsum("dqgcp,gh->dqgphc", m.reshape(2, nq, gq, c, p), eye).reshape(2, nq, sq, gq * c)
    blk = lambda *s: pl.BlockSpec((1, 1) + s, lambda d, q: (d, q, 0, 0))
    shp = lambda s, dt: jax.ShapeDtypeStruct((2, nq) + s, dt)
    return pl.pallas_call(
        _s5_prep_kernel,
        grid=(2, nq),
        in_specs=[blk(1, sq), blk(1, sq), blk(1, sq), blk(LANES, sq), blk(LANES, sq), blk(sq, LANES),
                  blk(sq, LANES)],
        out_specs=[blk(tc * LANES, LANES), blk(LANES, 2 * sq), blk(2 * sq, LANES), blk(tc, 2 * sq),
                   blk(tc, 2 * sq), blk(1, 2 * sq)],
        out_shape=[shp((tc * LANES, LANES), BF16), shp((LANES, 2 * sq), BF16), shp((2 * sq, LANES), BF16),
                   shp((tc, 2 * sq), F32), shp((tc, 2 * sq), F32), shp((1, 2 * sq), F32)],
        compiler_params=_cparams(("arbitrary", "arbitrary")),
        name="s5_prep",
    )(lanes_of(a_re), lanes_of(a_im), lanes_of(ls), bbd(b_re), bbd(b_im), cbd(c_re), cbd(c_im))


def _cmul(ar, ai, br, bi):
    return ar * br - ai * bi, ar * bi + ai * br


def _s5_kernel(*refs, tm, rev, has_prev):
    if has_prev:
        u_ref, prev_ref, ks_ref, bb_ref, cb_ref, we_ref, wc_ref, lt_ref, y_ref, pad_ref, st_ref = refs
    else:
        u_ref, ks_ref, bb_ref, cb_ref, we_ref, wc_ref, lt_ref, y_ref, pad_ref, st_ref = refs
    tc = S5_CHUNK
    nck = tm // tc
    nq = u_ref.shape[1] // LANES
    sq = S5_TILE_STATES

    @pl.when(pl.program_id(1) == 0)
    def _():
        st_ref[...] = jnp.zeros_like(st_ref)

    pad_ref[0:tc, :] = jnp.zeros((tc, LANES), F32)
    pad_ref[tc + tm:, :] = jnp.zeros((tc, LANES), F32)
    jpos = lax.broadcasted_iota(jnp.int32, (tm, 1), 0) % tc
    order = range(nck - 1, -1, -1) if rev else range(nck)
    for q in range(nq):
        lanes = slice(q * LANES, (q + 1) * LANES)
        u = u_ref[:, lanes]
        pad_ref[tc:tc + tm, :] = u
        cols = []
        for m in range(tc):
            if rev:
                sh = pad_ref[tc + m:tc + m + tm, :]
                keep = jpos + m < tc
            else:
                sh = pad_ref[tc - m:tc - m + tm, :]
                keep = jpos >= m
            cols.append(jnp.where(keep, sh, 0.0).astype(BF16))
        y = jnp.dot(jnp.concatenate(cols, axis=-1), ks_ref[0, q], preferred_element_type=F32)

        x = jnp.dot(u.astype(BF16), bb_ref[0, q], preferred_element_type=F32)
        we = jnp.tile(we_ref[0, q], (nck, 1))
        er, ei = _cmul(x[:, :sq], x[:, sq:], we[:, :sq], we[:, sq:])
        er = jnp.sum(er.reshape(nck, tc, sq), axis=1)
        ei = jnp.sum(ei.reshape(nck, tc, sq), axis=1)
        lam_r = lt_ref[0, q][:, :sq]
        lam_i = lt_ref[0, q][:, sq:]
        s_r = st_ref[2 * q:2 * q + 1, :]
        s_i = st_ref[2 * q + 1:2 * q + 2, :]
        ent_r, ent_i = [None] * nck, [None] * nck
        for k in order:
            ent_r[k] = jnp.broadcast_to(s_r, (tc, sq))
            ent_i[k] = jnp.broadcast_to(s_i, (tc, sq))
            n_r, n_i = _cmul(lam_r, lam_i, s_r, s_i)
            s_r = n_r + er[k:k + 1, :]
            s_i = n_i + ei[k:k + 1, :]
        st_ref[2 * q:2 * q + 1, :] = s_r
        st_ref[2 * q + 1:2 * q + 2, :] = s_i
        wc = jnp.tile(wc_ref[0, q], (nck, 1))
        z_r, z_i = _cmul(wc[:, :sq], wc[:, sq:], jnp.concatenate(ent_r, axis=0), jnp.concatenate(ent_i, axis=0))
        y = y + jnp.dot(jnp.concatenate([z_r, z_i], axis=-1).astype(BF16), cb_ref[0, q],
                        preferred_element_type=F32)
        y_ref[:, lanes] = (prev_ref[:, lanes] + y) if has_prev else y


def _s5(u_ssm, prep, *, n_batch, seq, ctx_len):
    t, dw = u_ssm.shape
    tm = ROW_TILE
    nq = dw // LANES
    tc = S5_CHUNK
    sq2 = 2 * S5_TILE_STATES
    nct, nlt = ctx_len // tm, seq // tm
    lat_tiles = n_batch * nlt
    ks, bb, cb, we, wc, lt = prep

    def tile_of(rev):
        def f(b, s):
            cpos = (nct - 1 - s) if rev else s
            lpos = (nlt - 1 - (s - nct)) if rev else (s - nct)
            return jnp.where(s < nct, lat_tiles + b * nct + cpos, b * nlt + lpos)
        return f

    y = None
    for d, rev in enumerate((False, True)):
        tile = tile_of(rev)
        row = pl.BlockSpec((tm, dw), lambda b, s: (tile(b, s), 0))
        par = lambda *shape: pl.BlockSpec((1, nq) + shape, lambda b, s: (d, 0, 0, 0))
        has_prev = y is not None
        kern = functools.partial(_s5_kernel, tm=tm, rev=rev, has_prev=has_prev)
        y = pl.pallas_call(
            kern,
            grid=(n_batch, nct + nlt),
            in_specs=[row] * (2 if has_prev else 1) + [par(tc * LANES, LANES), par(LANES, sq2), par(sq2, LANES),
                                                      par(tc, sq2), par(tc, sq2), par(1, sq2)],
            out_specs=row,
            out_shape=jax.ShapeDtypeStruct((t, dw), F32),
            scratch_shapes=[pltpu.VMEM((tm + 2 * tc, LANES), F32), pltpu.VMEM((2 * nq, S5_TILE_STATES), F32)],
            compiler_params=_cparams(("arbitrary", "arbitrary")),
            name="s5_scan",
        )(*([u_ssm] + ([y] if has_prev else []) + [ks, bb, cb, we, wc, lt]))
    return y


_NT = (((1,), (1,)), ((), ()))


def _attn_lat_kernel(q_ref, kl_ref, vl_ref, kc_ref, vc_ref, o_ref):
    q = q_ref[...]
    tq = q.shape[0]
    seq = kl_ref.shape[0]
    kc = min(ATTN_KEY_CHUNK, seq)
    chunks = [(kl_ref, vl_ref, c * kc, kc) for c in range(seq // kc)] + [(kc_ref, vc_ref, 0, kc_ref.shape[0])]
    m = jnp.full((tq, 1), NEG_BIG, F32)
    l = jnp.zeros((tq, 1), F32)
    acc = jnp.zeros((tq, V_HEAD), F32)
    for k_ref, v_ref, off, n in chunks:
        s = lax.dot_general(q, k_ref[off:off + n, :], _NT, preferred_element_type=F32)
        m_new = jnp.maximum(m, jnp.max(s, axis=-1, keepdims=True))
        alpha = jnp.exp(m - m_new)
        p = jnp.exp(s - m_new)
        l = alpha * l + jnp.sum(p, axis=-1, keepdims=True)
        acc = alpha * acc + jnp.dot(p.astype(BF16), v_ref[off:off + n, :], preferred_element_type=F32)
        m = m_new
    o_ref[...] = acc / l


def _attn_ctx_kernel(q_ref, kc_ref, vc_ref, o_ref):
    s = lax.dot_general(q_ref[...], kc_ref[...], _NT, preferred_element_type=F32)
    p = jnp.exp(s - jnp.max(s, axis=-1, keepdims=True))
    l = jnp.sum(p, axis=-1, keepdims=True)
    o_ref[...] = jnp.dot(p.astype(BF16), vc_ref[...], preferred_element_type=F32) / l


def _attention(q, k, v, *, n_batch, seq, ctx_len, ctx_queries):
    n_heads = v.shape[1] // V_HEAD
    tq = min(ATTN_Q_TILE, seq)
    nq = seq // tq
    cblk = n_batch * seq // ctx_len
    y_lat = pl.pallas_call(
        _attn_lat_kernel,
        grid=(n_batch, n_heads, nq),
        in_specs=[
            pl.BlockSpec((tq, QK_PAD), lambda b, h, i: (b * nq + i, h)),
            pl.BlockSpec((seq, QK_PAD), lambda b, h, i: (b, h)),
            pl.BlockSpec((seq, V_HEAD), lambda b, h, i: (b, h)),
            pl.BlockSpec((ctx_len, QK_PAD), lambda b, h, i: (cblk + b, h)),
            pl.BlockSpec((ctx_len, V_HEAD), lambda b, h, i: (cblk + b, h)),
        ],
        out_specs=pl.BlockSpec((tq, V_HEAD), lambda b, h, i: (b * nq + i, h)),
        out_shape=jax.ShapeDtypeStruct((n_batch * seq, n_heads * V_HEAD), F32),
        compiler_params=_cparams(("arbitrary", "arbitrary", "arbitrary")),
        name="attn_latent",
    )(q, k, v, k, v)
    if not ctx_queries:
        return y_lat, None
    y_ctx = pl.pallas_call(
        _attn_ctx_kernel,
        grid=(n_batch, n_heads),
        in_specs=[
            pl.BlockSpec((ctx_len, QK_PAD), lambda b, h: (cblk + b, h)),
            pl.BlockSpec((ctx_len, QK_PAD), lambda b, h: (cblk + b, h)),
            pl.BlockSpec((ctx_len, V_HEAD), lambda b, h: (cblk + b, h)),
        ],
        out_specs=pl.BlockSpec((ctx_len, V_HEAD), lambda b, h: (b, h)),
        out_shape=jax.ShapeDtypeStruct((n_batch * ctx_len, n_heads * V_HEAD), F32),
        compiler_params=_cparams(("arbitrary", "arbitrary")),
        name="attn_context",
    )(q, k, v)
    return y_lat, y_ctx


def _merge_kernel(x_ref, yp_ref, ys_ref, us_ref, yml_ref, *rest, tiles_per_seq, n_batch, d_model, has_ctx):
    if has_ctx:
        ymc_ref, rest = rest[0], rest[1:]
    (mod_ref, sd_ref, wglu_ref, bglu_ref, on_ref, wout_ref, nffn_ref, wr_ref, br_ref,
     x1_ref, h2_ref, idx_ref, gate_ref, rank_ref, cnt_ref, carry_ref) = rest
    d = d_model
    d4 = d // 4
    i = pl.program_id(0)
    tm = x_ref.shape[0]
    r = jnp.minimum(i // tiles_per_seq, n_batch)
    g1 = mod_ref[pl.ds(r, 1), pl.ds(2 * d, d)]
    sh2 = mod_ref[pl.ds(r, 1), pl.ds(3 * d, d)]
    sc2 = mod_ref[pl.ds(r, 1), pl.ds(4 * d, d)]

    z = ys_ref[...] + sd_ref[...] * us_ref[...]
    g = jax.nn.gelu(z, approximate=True)
    glu = jnp.dot(g.astype(BF16), wglu_ref[...], preferred_element_type=F32) + bglu_ref[...]
    y_ssm = g * jax.nn.sigmoid(glu)
    on = on_ref[...]
    y_mla = yml_ref[...]
    if has_ctx:
        y_mla = jnp.where(i < n_batch * tiles_per_seq, y_mla, ymc_ref[...])
    cat = jnp.concatenate([_rms(yp_ref[...], on[:, 0:d4]), _rms(y_ssm, on[:, d4:2 * d4]),
                           _rms(y_mla, on[:, 2 * d4:])], axis=-1).astype(BF16)
    x1 = x_ref[...] + g1 * jnp.dot(cat, wout_ref[...], preferred_element_type=F32)
    x1_ref[...] = x1
    h2 = (_rms(x1, nffn_ref[...]) * (1.0 + sc2) + sh2).astype(BF16)
    h2_bits = lax.bitcast_convert_type(h2.astype(F32), jnp.uint32)
    h2_ref[...] = (h2_bits[:, :d // 2] >> 16) | (h2_bits[:, d // 2:] & jnp.uint32(0xFFFF0000))

    logits = jnp.dot(h2, wr_ref[...], preferred_element_type=F32) + br_ref[...]
    lane = lax.broadcasted_iota(jnp.int32, (tm, LANES), 1)
    vals, hots = [], []
    idx_out = jnp.zeros((tm, LANES), jnp.int32)
    for kk in range(TOP_K):
        mx = jnp.max(logits, axis=-1, keepdims=True)
        sel = jnp.min(jnp.where(logits == mx, lane, LANES), axis=-1, keepdims=True)
        hot = lane == sel
        vals.append(mx)
        hots.append(hot)
        idx_out = jnp.where(lane == kk, sel, idx_out)
        logits = jnp.where(hot, NEG_BIG * 2.0, logits)
    es = [jnp.exp(vv - vals[0]) for vv in vals]
    den = es[0] + es[1] + es[2] + es[3]
    gate_out = jnp.zeros((tm, LANES), F32)
    for kk in range(TOP_K):
        gate_out = jnp.where(lane == kk, es[kk] / den, gate_out)
    idx_ref[...] = idx_out
    gate_ref[...] = gate_out

    @pl.when(i == 0)
    def _():
        carry_ref[...] = jnp.zeros_like(carry_ref)

    onehot = (hots[0] | hots[1] | hots[2] | hots[3]).astype(BF16)
    rr = lax.broadcasted_iota(jnp.int32, (tm, tm), 0)
    cc = lax.broadcasted_iota(jnp.int32, (tm, tm), 1)
    tri = (cc < rr).astype(BF16)
    pos = jnp.dot(tri, onehot, preferred_element_type=F32) + carry_ref[...]
    rank_out = jnp.zeros((tm, LANES), jnp.int32)
    for kk in range(TOP_K):
        rk = jnp.sum(jnp.where(hots[kk], pos, 0.0), axis=-1, keepdims=True).astype(jnp.int32)
        rank_out = jnp.where(lane == kk, rk, rank_out)
    rank_ref[...] = rank_out
    carry = carry_ref[...] + jnp.sum(onehot.astype(F32), axis=0, keepdims=True)
    carry_ref[...] = carry
    cnt_ref[...] = carry.astype(jnp.int32)


def _merge(xt, y_pool, y_s5, u_ssm, y_mla_lat, y_mla_ctx, mods, ssm_d, wglu, bglu, out_norm, wout, nffn, wr, br, *,
           n_rows, n_batch, seq):
    d = xt.shape[1]
    tm = ROW_TILE
    n_lat_tiles = n_batch * seq // tm
    has_ctx = y_mla_ctx is not None
    row = lambda w: pl.BlockSpec((tm, w), lambda i: (i, 0))
    kern = functools.partial(_merge_kernel, tiles_per_seq=seq // tm, n_batch=n_batch, d_model=d, has_ctx=has_ctx)
    small = [mods, ssm_d, wglu, bglu, out_norm, wout, nffn, wr, br]
    y_mla = [y_mla_lat] + ([y_mla_ctx] if has_ctx else [])
    y_specs = [pl.BlockSpec((tm, d // 2), lambda i: (jnp.minimum(i, n_lat_tiles - 1), 0))]
    if has_ctx:
        y_specs.append(pl.BlockSpec((tm, d // 2), lambda i: (jnp.maximum(i - n_lat_tiles, 0), 0)))
    return pl.pallas_call(
        kern,
        grid=(n_rows // tm,),
        in_specs=[row(d), row(d // 4), row(d // 4), row(d // 4)] + y_specs + [_full(a.shape) for a in small],
        out_specs=[row(d), row(d // 2), row(LANES), row(LANES), row(LANES), _full((1, LANES))],
        out_shape=[jax.ShapeDtypeStruct((n_rows, d), F32), jax.ShapeDtypeStruct((n_rows, d // 2), jnp.uint32),
                   jax.ShapeDtypeStruct((n_rows, LANES), jnp.int32), jax.ShapeDtypeStruct((n_rows, LANES), F32),
                   jax.ShapeDtypeStruct((n_rows, LANES), jnp.int32), jax.ShapeDtypeStruct((1, LANES), jnp.int32)],
        scratch_shapes=[pltpu.VMEM((1, LANES), F32)],
        compiler_params=_cparams(("arbitrary",)),
        name="merge_router",
    )(xt, y_pool, y_s5, u_ssm, *y_mla, *small)


def _row_copy(src, dst, sem, src_row, dst_row):
    return pltpu.make_async_copy(src.at[pl.ds(src_row, 1), :], dst.at[pl.ds(dst_row, 1), :], sem)


def _dispatch_kernel(dest_hbm, h2_ref, zeros_hbm, xs_hbm, dest_smem, sem_idx, sem_rows, *, tm):
    del zeros_hbm
    i = pl.program_id(0)
    n = tm * TOP_K
    idx_copy = pltpu.make_async_copy(dest_hbm.at[i], dest_smem, sem_idx)
    idx_copy.start()
    idx_copy.wait()

    def issue(r, _):
        for kk in range(TOP_K):
            _row_copy(h2_ref, xs_hbm, sem_rows, r, dest_smem[r * TOP_K + kk]).start(priority=kk % 2)
        return 0

    lax.fori_loop(0, tm, issue, 0, unroll=2)

    def drain(r, _):
        for kk in range(TOP_K):
            _row_copy(h2_ref, xs_hbm, sem_rows, r, dest_smem[r * TOP_K + kk]).wait()
        return 0

    lax.fori_loop(0, tm, drain, 0, unroll=2)


def _dispatch(dest_tiles, h2_packed, n_sorted_rows):
    n_tok, w = h2_packed.shape
    tm = ROW_TILE
    zeros = jnp.zeros((n_sorted_rows, w), h2_packed.dtype)
    return pl.pallas_call(
        functools.partial(_dispatch_kernel, tm=tm),
        grid=(n_tok // tm,),
        in_specs=[pl.BlockSpec(memory_space=pl.ANY), pl.BlockSpec((tm, w), lambda i: (i, 0)),
                  pl.BlockSpec(memory_space=pl.ANY)],
        out_specs=pl.BlockSpec(memory_space=pl.ANY),
        out_shape=jax.ShapeDtypeStruct(zeros.shape, zeros.dtype),
        input_output_aliases={2: 0},
        scratch_shapes=[pltpu.SMEM((tm * TOP_K,), jnp.int32), pltpu.SemaphoreType.DMA, pltpu.SemaphoreType.DMA],
        compiler_params=_cparams(("arbitrary",)),
        name="moe_dispatch",
    )(dest_tiles, h2_packed, zeros)


def _expert_kernel(ie_ref, ix_ref, io_ref, ns_ref, nz_ref, x_ref, wg_ref, bg_ref, wu_ref, bu_ref, wd_ref, bd_ref,
                   o_hbm, act_ref, xb_ref, res_ref, zero_ref, sem_out, sem_zero, *, nf):
    del ie_ref, ix_ref
    it = pl.program_id(0)
    j = pl.program_id(1)
    ns = ns_ref[it]
    rows_max = x_ref.shape[0]
    half = x_ref.shape[1]
    tw = res_ref.shape[2]
    sub = EXPERT_SUB_ROWS
    o_row = pl.multiple_of(io_ref[it] * sub, sub)
    n = j - nf

    def out_copy(step, mr):
        col = pl.multiple_of(step * tw, tw)
        return pltpu.make_async_copy(res_ref.at[step % 2, 0:mr, :], o_hbm.at[pl.ds(o_row, mr), pl.ds(col, tw)],
                                     sem_out.at[step % 2])

    @pl.when((ns > 0) & (j == 0))
    def _():
        words = x_ref[...]
        xb_ref[:, 0:half] = lax.bitcast_convert_type(words << 16, F32).astype(BF16)
        xb_ref[:, half:] = lax.bitcast_convert_type(words & jnp.uint32(0xFFFF0000), F32).astype(BF16)

    for m in EXPERT_ROW_VARIANTS:
        mr = m * EXPERT_SUB_ROWS

        @pl.when((ns == m) & (j < nf))
        def _():
            x = xb_ref[0:mr, :]
            g = jnp.dot(x, wg_ref[0, 0].astype(BF16), preferred_element_type=F32) + bg_ref[0, 0]
            u = jnp.dot(x, wu_ref[0, 0].astype(BF16), preferred_element_type=F32) + bu_ref[0, 0]
            g = jnp.minimum(g, SWIGLU_LIMIT)
            u = jnp.clip(u, -SWIGLU_LIMIT, SWIGLU_LIMIT)
            act_ref[j, 0:mr, :] = (g * jax.nn.sigmoid(SWIGLU_ALPHA * g) * (u + 1.0)).astype(BF16)

        @pl.when((ns == m) & (j >= nf))
        def _():
            @pl.when(n >= 2)
            def _():
                out_copy(n - 2, mr).wait()

            act = jnp.concatenate([act_ref[f, 0:mr, :] for f in range(nf)], axis=-1)
            res = jnp.dot(act, wd_ref[0, 0].astype(BF16), preferred_element_type=F32) + bd_ref[0, 0]
            bits = lax.bitcast_convert_type(res.astype(BF16).astype(F32), jnp.uint32)
            res_ref[n % 2, 0:mr, :] = (bits[:, :tw] >> 16) | (bits[:, tw:] & jnp.uint32(0xFFFF0000))
            out_copy(n, mr).start()

            @pl.when(n == nf - 1)
            def _():
                out_copy(n - 1, mr).wait()
                out_copy(n, mr).wait()

    nz = nz_ref[it]

    @pl.when((nz > 0) & (j == 2 * nf - 1))
    def _():
        zero_ref[...] = jnp.zeros(zero_ref.shape, jnp.uint32)

        def zero_copy(k):
            rows0 = pl.multiple_of(o_row + k * sub, sub)
            return pltpu.make_async_copy(zero_ref, o_hbm.at[pl.ds(rows0, sub), :], sem_zero)

        def start(k, _):
            zero_copy(k).start()
            return 0

        def wait(k, _):
            zero_copy(k).wait()
            return 0

        lax.fori_loop(0, nz, start, 0)
        lax.fori_loop(0, nz, wait, 0)


def _experts(layer, items, xs, w_gate, b_gate, w_up, b_up, w_down, b_down):
    item_e, item_x, item_o, item_ns, item_nz = items
    n_items = item_e.shape[0]
    n_layers, n_exp, d, f_dim = w_gate.shape
    tf = EXPERT_TILE
    nf = f_dim // tf
    rows = EXPERT_ITEM_ROWS
    assert d // tf == nf and nf >= 2 and max(EXPERT_ROW_VARIANTS) * EXPERT_SUB_ROWS == rows

    def fi(j, ns, it):
        return jnp.where(ns[it] > 0, jnp.minimum(j, nf - 1), nf - 1)

    def ni(j, ns, it):
        return jnp.where(ns[it] > 0, jnp.maximum(j - nf, 0), nf - 1)

    return pl.pallas_call(
        functools.partial(_expert_kernel, nf=nf),
        grid_spec=pltpu.PrefetchScalarGridSpec(
            num_scalar_prefetch=5,
            grid=(n_items, 2 * nf),
            in_specs=[
                pl.BlockSpec((pl.Element(rows), pl.Element(d // 2)),
                             lambda it, j, ie, ix, io, ns, wk: (ix[it] * EXPERT_SUB_ROWS, 0)),
                pl.BlockSpec((1, 1, d, tf), lambda it, j, ie, ix, io, ns, wk: (layer, ie[it], 0, fi(j, ns, it))),
                pl.BlockSpec((1, 1, 1, tf), lambda it, j, ie, ix, io, ns, wk: (layer, ie[it], 0, fi(j, ns, it))),
                pl.BlockSpec((1, 1, d, tf), lambda it, j, ie, ix, io, ns, wk: (layer, ie[it], 0, fi(j, ns, it))),
                pl.BlockSpec((1, 1, 1, tf), lambda it, j, ie, ix, io, ns, wk: (layer, ie[it], 0, fi(j, ns, it))),
                pl.BlockSpec((1, 1, f_dim, tf), lambda it, j, ie, ix, io, ns, wk: (layer, ie[it], 0, ni(j, ns, it))),
                pl.BlockSpec((1, 1, 1, tf), lambda it, j, ie, ix, io, ns, wk: (layer, ie[it], 0, ni(j, ns, it))),
            ],
            out_specs=pl.BlockSpec(memory_space=pl.ANY),
            scratch_shapes=[pltpu.VMEM((nf, rows, tf), BF16), pltpu.VMEM((rows, d), BF16),
                            pltpu.VMEM((2, rows, tf // 2), jnp.uint32),
                            pltpu.VMEM((EXPERT_SUB_ROWS, d // 2), jnp.uint32),
                            pltpu.SemaphoreType.DMA((2,)), pltpu.SemaphoreType.DMA],
        ),
        out_shape=jax.ShapeDtypeStruct((xs.shape[0], d // 2), jnp.uint32),
        compiler_params=_cparams(("arbitrary", "arbitrary"), EXPERT_VMEM_LIMIT),
        name="moe_experts",
    )(item_e, item_x, item_o, item_ns, item_nz, xs, w_gate, b_gate.reshape(n_layers, n_exp, 1, f_dim), w_up,
      b_up.reshape(n_layers, n_exp, 1, f_dim), w_down, b_down.reshape(n_layers, n_exp, 1, d))


def _combine_kernel(dest_hbm, os_hbm, gate_ref, x_ref, mod_ref, o_ref, dest_smem, buf, sem_idx, sem_rows,
                    *, tc, n_tiles, tiles_per_seq, n_batch):
    i = pl.program_id(0)
    n = tc * TOP_K
    slot = i % 2
    d = x_ref.shape[1]

    def gather(s, r, kk):
        return _row_copy(os_hbm, buf.at[s, kk], sem_rows.at[s], dest_smem[s, r * TOP_K + kk], r)

    def fetch(tile, s):
        idx_copy = pltpu.make_async_copy(dest_hbm.at[tile], dest_smem.at[s], sem_idx)
        idx_copy.start()
        idx_copy.wait()

        def issue(r, _):
            for kk in range(TOP_K):
                gather(s, r, kk).start(priority=kk % 2)
            return 0

        lax.fori_loop(0, tc, issue, 0, unroll=2)

    @pl.when(i == 0)
    def _():
        fetch(0, 0)

    @pl.when(i + 1 < n_tiles)
    def _():
        fetch(i + 1, 1 - slot)

    def drain(r, _):
        for kk in range(TOP_K):
            gather(slot, r, kk).wait()
        return 0

    lax.fori_loop(0, tc, drain, 0, unroll=2)
    g2 = mod_ref[pl.ds(jnp.minimum(i // tiles_per_seq, n_batch), 1), pl.ds(5 * d, d)]
    gates = gate_ref[...]
    lo = hi = None
    for kk in range(TOP_K):
        words = buf[slot, kk]
        g = gates[:, kk:kk + 1]
        lo_k = g * lax.bitcast_convert_type(words << 16, F32)
        hi_k = g * lax.bitcast_convert_type(words & jnp.uint32(0xFFFF0000), F32)
        lo = lo_k if lo is None else lo + lo_k
        hi = hi_k if hi is None else hi + hi_k
    tw = EXPERT_TILE // 2
    pieces = []
    for c in range(0, d // 2, tw):
        pieces += [lo[:, c:c + tw], hi[:, c:c + tw]]
    o_ref[...] = x_ref[...] + g2 * jnp.concatenate(pieces, axis=-1)


def _combine(dest_tiles, out_sorted, gates, x1, mods, *, n_batch, seq):
    n_tok, d = x1.shape
    tc = COMBINE_TILE
    kern = functools.partial(_combine_kernel, tc=tc, n_tiles=n_tok // tc, tiles_per_seq=seq // tc, n_batch=n_batch)
    return pl.pallas_call(
        kern,
        grid=(n_tok // tc,),
        in_specs=[pl.BlockSpec(memory_space=pl.ANY)] * 2 + [
            pl.BlockSpec((tc, LANES), lambda i: (i, 0)), pl.BlockSpec((tc, d), lambda i: (i, 0)),
            _full(mods.shape)],
        out_specs=pl.BlockSpec((tc, d), lambda i: (i, 0)),
        out_shape=jax.ShapeDtypeStruct(x1.shape, F32),
        scratch_shapes=[pltpu.SMEM((2, tc * TOP_K), jnp.int32), pltpu.VMEM((2, TOP_K, tc, d // 2), jnp.uint32),
                        pltpu.SemaphoreType.DMA, pltpu.SemaphoreType.DMA((2,))],
        compiler_params=_cparams(("arbitrary",)),
        name="moe_combine",
    )(dest_tiles, out_sorted, gates, x1, mods)


def _rope_tables(n_batch, seq, ctx_len):
    rows = seq // GRID_W
    row = jnp.broadcast_to(jnp.arange(rows)[:, None], (rows, GRID_W)).reshape(-1).astype(F32)
    col = jnp.broadcast_to(jnp.arange(GRID_W)[None, :], (rows, GRID_W)).reshape(-1).astype(F32)
    half = QK_ROPE // 2
    inv_freq = ROPE_BASE ** (-jnp.arange(0, half, 2, dtype=F32) / half)
    ang = jnp.concatenate([row[:, None] * inv_freq, col[:, None] * inv_freq], axis=-1)
    cos, sin = jnp.cos(ang), jnp.sin(ang)
    zpad = jnp.zeros((seq, LANES - QK_ROPE), F32)
    cos_l = jnp.concatenate([cos, cos, zpad], axis=-1)
    sin_l = jnp.concatenate([-sin, sin, zpad], axis=-1)
    n_ctx = n_batch * ctx_len
    cos_c = jnp.concatenate([jnp.ones((n_ctx, QK_ROPE), F32), jnp.zeros((n_ctx, LANES - QK_ROPE), F32)], axis=-1)
    cos_t = jnp.concatenate([jnp.tile(cos_l, (n_batch, 1)), cos_c], axis=0)
    sin_t = jnp.concatenate([jnp.tile(sin_l, (n_batch, 1)), jnp.zeros((n_ctx, LANES), F32)], axis=0)
    return cos_t, sin_t


def _pad_head_vec(vec):
    return jnp.concatenate([vec, jnp.zeros((QK_PAD - QK_HEAD,), vec.dtype)])[None, :]


def _routing(idx, rank, counts, n_tok, n_exp):
    rows, sub = EXPERT_ITEM_ROWS, EXPERT_SUB_ROWS
    n = n_tok * TOP_K
    cap = n + n_exp * sub + rows
    n_items = -(-n // rows) + n_exp + 1
    i32 = lambda a: a.astype(jnp.int32)
    cnt = counts[0, :n_exp]
    seg = ((cnt + sub - 1) // sub) * sub
    seg_end = jnp.cumsum(seg)
    seg_start = seg_end - seg
    dest = seg_start[idx] + rank
    used_end = seg_end[-1]
    per = (cnt + rows - 1) // rows
    it_end = jnp.cumsum(per)
    it_start = it_end - per
    total = it_end[-1]
    it = jnp.arange(n_items, dtype=jnp.int32)
    it_c = jnp.minimum(it, total - 1)
    e_of = i32(jnp.minimum(jnp.searchsorted(it_end, it_c, side="right"), n_exp - 1))
    k_in = it_c - it_start[e_of]
    x_row = seg_start[e_of] + k_in * rows
    nsub = jnp.clip((cnt[e_of] - k_in * rows + sub - 1) // sub, min(EXPERT_ROW_VARIANTS), max(EXPERT_ROW_VARIANTS))
    active = it < total
    nsub = jnp.where(active, nsub, 0)
    first_idle = it == total
    o_row = jnp.where(active, x_row, jnp.where(first_idle, used_end, 0))
    n_zero = jnp.where(first_idle, (cap - used_end) // sub, 0)
    return i32(dest), (e_of, i32(x_row // sub), i32(o_row // sub), i32(nsub), i32(n_zero)), cap


def _layer(layer, xt, mods, cos_t, sin_t, p, expert_params, *, n_batch, seq, ctx_len, ctx_out):
    t, d = xt.shape
    d4 = d // 4
    n_heads = (d - 2 * d4) // V_HEAD
    kv_lora = d // 8
    bf = lambda a: a.astype(BF16)
    row = lambda a: a[None, :]

    win = bf(jnp.pad(p["w_in"], ((0, 0), (0, LANES - QK_ROPE))))
    wuq = p["w_uq"].reshape(-1, n_heads, QK_HEAD)
    wuq = bf(jnp.pad(wuq, ((0, 0), (0, 0), (0, QK_PAD - QK_HEAD))).reshape(-1, n_heads * QK_PAD))
    wukv = p["w_ukv"].reshape(kv_lora, n_heads, QK_NOPE + V_HEAD)
    wk = bf(wukv[:, :, :QK_NOPE].reshape(kv_lora, n_heads * QK_NOPE))
    wv = bf(wukv[:, :, QK_NOPE:].reshape(kv_lora, n_heads * V_HEAD))
    qn = _pad_head_vec(p["q_norm"]) * (QK_HEAD ** -0.5)
    kn = _pad_head_vec(p["k_norm"])

    u_pool, u_ssm, q, k, v = _proj(xt, mods, cos_t, sin_t, row(p["norm_mix"]), win, row(p["q_a_norm"]), wuq,
                                   row(p["kv_a_norm"]), wk, wv, qn, kn, n_batch=n_batch, seq=seq)

    y_pool = _pool(u_pool, bf(p["w_pool"]), row(p["pool_scale"]), n_batch=n_batch, seq=seq, ctx_len=ctx_len)

    prep = _s5_prep(p["ssm_a_re"], p["ssm_a_im"], p["ssm_log_step"], p["ssm_b_re"], p["ssm_b_im"],
                    p["ssm_c_re"], p["ssm_c_im"])
    y_s5 = _s5(u_ssm, prep, n_batch=n_batch, seq=seq, ctx_len=ctx_len)

    y_mla_lat, y_mla_ctx = _attention(q, k, v, n_batch=n_batch, seq=seq, ctx_len=ctx_len, ctx_queries=ctx_out)

    n_rows = t if ctx_out else n_batch * seq
    n_exp = p["w_router"].shape[1]
    wr = bf(jnp.pad(p["w_router"], ((0, 0), (0, LANES - n_exp))))
    br = jnp.concatenate([p["b_router"], jnp.full((LANES - n_exp,), NEG_BIG, F32)])[None, :]
    x1, h2, idx, gates, rank, counts = _merge(
        xt, y_pool, y_s5, u_ssm, y_mla_lat, y_mla_ctx, mods, row(p["ssm_d"]), bf(p["w_glu"]), row(p["b_glu"]),
        row(p["out_norm"]), bf(p["w_out"]), row(p["norm_ffn"]), wr, br, n_rows=n_rows, n_batch=n_batch, seq=seq)

    dest, items, cap = _routing(idx[:, :TOP_K], rank[:, :TOP_K], counts, n_rows, n_exp)
    xs = _dispatch(dest.reshape(n_rows // ROW_TILE, ROW_TILE * TOP_K), h2, cap)
    out_sorted = _experts(layer, items, xs, *expert_params)
    return _combine(dest.reshape(n_rows // COMBINE_TILE, COMBINE_TILE * TOP_K), out_sorted, gates, x1, mods,
                    n_batch=n_batch, seq=seq)


_PARAM_NAMES = ("norm_mix", "norm_ffn", "w_in", "w_pool", "pool_scale", "ssm_a_re", "ssm_a_im", "ssm_log_step",
                "ssm_b_re", "ssm_b_im", "ssm_c_re", "ssm_c_im", "ssm_d", "w_glu", "b_glu", "q_a_norm", "w_uq",
                "kv_a_norm", "w_ukv", "q_norm", "k_norm", "out_norm", "w_out", "w_router", "b_router")


def kernel(x, c, ctx, c_ctx, w_ada, b_ada, norm_mix, norm_ffn, w_in, w_pool, pool_scale, ssm_a_re, ssm_a_im, ssm_log_step, ssm_b_re, ssm_b_im, ssm_c_re, ssm_c_im, ssm_d, w_glu, b_glu, q_a_norm, w_uq, kv_a_norm, w_ukv, q_norm, k_norm, out_norm, w_out, w_router, b_router, w_gate, b_gate, w_up, b_up, w_down, b_down):
    n_batch, seq, d = x.shape
    ctx_len = ctx.shape[1]
    depth = w_ada.shape[0]
    assert seq % ROW_TILE == 0 and ctx_len % ROW_TILE == 0 and seq % GRID_W == 0 and n_batch + 1 <= 8
    stacked = dict(zip(_PARAM_NAMES, (norm_mix, norm_ffn, w_in, w_pool, pool_scale, ssm_a_re, ssm_a_im,
                                      ssm_log_step, ssm_b_re, ssm_b_im, ssm_c_re, ssm_c_im, ssm_d, w_glu, b_glu,
                                      q_a_norm, w_uq, kv_a_norm, w_ukv, q_norm, k_norm, out_norm, w_out, w_router,
                                      b_router)))
    expert_params = (w_gate, b_gate, w_up, b_up, w_down, b_down)
    cvecs = jnp.concatenate([c, c_ctx[None, :], jnp.zeros((8 - n_batch - 1, d), F32)], axis=0)
    mods = _ada(cvecs, w_ada, b_ada)
    cos_t, sin_t = _rope_tables(n_batch, seq, ctx_len)
    n_lat = n_batch * seq
    xt = jnp.concatenate([x.reshape(n_lat, d), ctx.reshape(n_batch * ctx_len, d)], axis=0)
    for layer in range(depth):
        p = {name: val[layer] for name, val in stacked.items()}
        xt = _layer(layer, xt, mods[layer], cos_t, sin_t, p, expert_params, n_batch=n_batch, seq=seq,
                    ctx_len=ctx_len, ctx_out=layer < depth - 1)
    return xt[:n_lat].reshape(n_batch, seq, d)
```

```python
import functools
import math

import jax
import jax.numpy as jnp
from jax import lax
from jax.experimental import pallas as pl
from jax.experimental.pallas import tpu as pltpu

F32 = jnp.float32
BF16 = jnp.bfloat16

EPS = 1e-6
GRID_W = 64
POOL_WINDOWS = (2, 4, 8, 16)
POOL_HALO = 8
SSM_GROUP = 16
SSM_STATE = 64
S5_CHUNK = 16
S5_TILE_STATES = (128 // SSM_GROUP) * SSM_STATE
QK_NOPE = 128
QK_ROPE = 64
QK_HEAD = QK_NOPE + QK_ROPE
QK_PAD = 256
V_HEAD = 128
ROPE_BASE = 10000.0
TOP_K = 4
SWIGLU_ALPHA = 1.702
SWIGLU_LIMIT = 7.0

LANES = 128
ROW_TILE = 256
ATTN_Q_TILE = 1024
ATTN_KEY_CHUNK = 1024
EXPERT_ITEM_ROWS = 1280
EXPERT_SUB_ROWS = 256
EXPERT_ROW_VARIANTS = (1, 2, 3, 4, 5)
EXPERT_TILE = 512
COMBINE_TILE = 256
VMEM_LIMIT = 52 * 1024 * 1024
EXPERT_VMEM_LIMIT = 57 * 1024 * 1024
NEG_BIG = -1e30


def _cparams(sem, vmem=VMEM_LIMIT):
    return pltpu.CompilerParams(dimension_semantics=sem, vmem_limit_bytes=vmem)


def _rms(x, w):
    return x * lax.rsqrt(jnp.mean(x * x, axis=-1, keepdims=True) + EPS) * w


def _full(shape):
    nd = len(shape)
    return pl.BlockSpec(shape, lambda *_: (0,) * nd)


def _ada_kernel(cv_ref, w_ref, b_ref, o_ref):
    cv = cv_ref[...]
    s = (cv * jax.nn.sigmoid(cv)).astype(BF16)
    o_ref[0] = jnp.dot(s, w_ref[0].astype(BF16), preferred_element_type=F32) + b_ref[0]


def _ada(cvecs, w_ada, b_ada):
    n_layers, d, n6 = w_ada.shape
    tn = 512
    return pl.pallas_call(
        _ada_kernel,
        grid=(n_layers, n6 // tn),
        in_specs=[
            pl.BlockSpec((8, d), lambda l, j: (0, 0)),
            pl.BlockSpec((1, d, tn), lambda l, j: (l, 0, j)),
            pl.BlockSpec((1, 1, tn), lambda l, j: (l, 0, j)),
        ],
        out_specs=pl.BlockSpec((1, 8, tn), lambda l, j: (l, 0, j)),
        out_shape=jax.ShapeDtypeStruct((n_layers, 8, n6), F32),
        compiler_params=_cparams(("arbitrary", "arbitrary")),
        name="ada",
    )(cvecs, w_ada, b_ada.reshape(n_layers, 1, n6))


def _rope128(r, cos, sin):
    lane = lax.broadcasted_iota(jnp.int32, r.shape, 1)
    sw = jnp.where((lane % QK_ROPE) < QK_ROPE // 2, pltpu.roll(r, LANES - 32, 1), pltpu.roll(r, 32, 1))
    return r * cos + sw * sin


def _proj_kernel(x_ref, mod_ref, cos_ref, sin_ref, nmix_ref, win_ref, qan_ref, wuq_ref, kvan_ref,
                 wk_ref, wv_ref, qn_ref, kn_ref,
                 upool_ref, ussm_ref, q_ref, k_ref, v_ref, *, tiles_per_seq, n_batch, n_heads, d_model):
    d = d_model
    r = jnp.minimum(pl.program_id(0) // tiles_per_seq, n_batch)
    sh = mod_ref[pl.ds(r, 1), pl.ds(0, d)]
    sc = mod_ref[pl.ds(r, 1), pl.ds(d, d)]
    h = _rms(x_ref[...], nmix_ref[...]) * (1.0 + sc) + sh
    proj = jnp.dot(h.astype(BF16), win_ref[...], preferred_element_type=F32)
    d4 = d // 4
    upool_ref[...] = proj[:, 0:d4]
    ussm_ref[...] = proj[:, 2 * d4:3 * d4]
    cos = cos_ref[...]
    sin = sin_ref[...]

    qa = _rms(proj[:, d4:2 * d4], qan_ref[...]).astype(BF16)
    qf = jnp.dot(qa, wuq_ref[...], preferred_element_type=F32)
    qn_w = qn_ref[...]
    for hd in range(n_heads):
        blk = qf[:, hd * QK_PAD:(hd + 1) * QK_PAD]
        ss = jnp.sum(blk * blk, axis=-1, keepdims=True) * (1.0 / QK_HEAD)
        qn = blk * lax.rsqrt(ss + EPS) * qn_w
        q_ref[:, hd * QK_PAD:hd * QK_PAD + LANES] = qn[:, :LANES].astype(BF16)
        q_ref[:, hd * QK_PAD + LANES:(hd + 1) * QK_PAD] = _rope128(qn[:, LANES:], cos, sin).astype(BF16)

    kv_lo = 3 * d4
    kv_w = d // 8
    ka = _rms(proj[:, kv_lo:kv_lo + kv_w], kvan_ref[...]).astype(BF16)
    kn_all = jnp.dot(ka, wk_ref[...], preferred_element_type=F32)
    v_ref[...] = jnp.dot(ka, wv_ref[...], preferred_element_type=F32).astype(BF16)
    krp = proj[:, kv_lo + kv_w:kv_lo + kv_w + LANES]
    kr_ss = jnp.sum(krp * krp, axis=-1, keepdims=True)
    kn_w = kn_ref[...]
    for hd in range(n_heads):
        kn = kn_all[:, hd * QK_NOPE:(hd + 1) * QK_NOPE]
        ss = (jnp.sum(kn * kn, axis=-1, keepdims=True) + kr_ss) * (1.0 / QK_HEAD)
        rinv = lax.rsqrt(ss + EPS)
        k_ref[:, hd * QK_PAD:hd * QK_PAD + LANES] = (kn * rinv * kn_w[:, :LANES]).astype(BF16)
        k_ref[:, hd * QK_PAD + LANES:(hd + 1) * QK_PAD] = _rope128(
            krp * rinv * kn_w[:, LANES:], cos, sin).astype(BF16)


def _proj(xt, mods, cos_t, sin_t, nmix, win, qan, wuq, kvan, wk, wv, qn, kn, *, n_batch, seq):
    t, d = xt.shape
    tm = ROW_TILE
    n_heads = wv.shape[1] // V_HEAD
    row = lambda w: pl.BlockSpec((tm, w), lambda i: (i, 0))
    kern = functools.partial(_proj_kernel, tiles_per_seq=seq // tm, n_batch=n_batch,
                             n_heads=n_heads, d_model=d)
    return pl.pallas_call(
        kern,
        grid=(t // tm,),
        in_specs=[row(d), _full(mods.shape), row(LANES), row(LANES), _full(nmix.shape), _full(win.shape),
                  _full(qan.shape), _full(wuq.shape), _full(kvan.shape), _full(wk.shape), _full(wv.shape),
                  _full(qn.shape), _full(kn.shape)],
        out_specs=[row(d // 4), row(d // 4), row(n_heads * QK_PAD), row(n_heads * QK_PAD),
                   row(n_heads * V_HEAD)],
        out_shape=[jax.ShapeDtypeStruct((t, d // 4), F32), jax.ShapeDtypeStruct((t, d // 4), F32),
                   jax.ShapeDtypeStruct((t, n_heads * QK_PAD), BF16),
                   jax.ShapeDtypeStruct((t, n_heads * QK_PAD), BF16),
                   jax.ShapeDtypeStruct((t, n_heads * V_HEAD), BF16)],
        compiler_params=_cparams(("arbitrary",)),
        name="mixer_in",
    )(xt, mods, cos_t, sin_t, nmix, win, qan, wuq, kvan, wk, wv, qn, kn)


def _pool_kernel(prev_ref, cur_ref, next_ref, w_ref, scale_ref, o_ref, pad_ref, *,
                 tm, n_lat_tiles, seq, ctx_len):
    i = pl.program_id(0)
    hw = POOL_HALO
    pad_ref[0:hw, :] = prev_ref[...]
    pad_ref[hw:hw + tm, :] = cur_ref[...]
    pad_ref[hw + tm:2 * hw + tm, :] = next_ref[...]
    is_lat = i < n_lat_tiles
    row0 = i * tm
    s0 = jnp.where(is_lat, (row0 // seq) * seq,
                   n_lat_tiles * tm + ((row0 - n_lat_tiles * tm) // ctx_len) * ctx_len)
    slen = jnp.where(is_lat, seq, ctx_len)
    t = row0 - s0 + lax.broadcasted_iota(jnp.int32, (tm, 1), 0)
    for g, w in enumerate(POOL_WINDOWS):
        lanes = slice(g * LANES, (g + 1) * LANES)
        acc = jnp.zeros((tm, LANES), F32)
        for kk in range(-(w // 2), w - w // 2):
            valid = (t + kk >= 0) & (t + kk < slen)
            acc = acc + jnp.where(valid, pad_ref[hw + kk:hw + kk + tm, lanes], 0.0)
        lo = jnp.maximum(t - w // 2, 0)
        hi = jnp.minimum(t + (w - w // 2), slen)
        mean = acc / (hi - lo).astype(F32)
        dlt = (mean - cur_ref[:, lanes]).astype(BF16)
        o_ref[:, lanes] = jnp.dot(dlt, w_ref[g], preferred_element_type=F32) * scale_ref[:, lanes]


def _pool(u_pool, w_pool, pool_scale, *, n_batch, seq, ctx_len):
    t, dp = u_pool.shape
    tm = ROW_TILE
    hb = tm // POOL_HALO
    n_hblocks = t // POOL_HALO
    kern = functools.partial(_pool_kernel, tm=tm, n_lat_tiles=n_batch * seq // tm, seq=seq, ctx_len=ctx_len)
    return pl.pallas_call(
        kern,
        grid=(t // tm,),
        in_specs=[
            pl.BlockSpec((POOL_HALO, dp), lambda i: (jnp.maximum(i * hb - 1, 0), 0)),
            pl.BlockSpec((tm, dp), lambda i: (i, 0)),
            pl.BlockSpec((POOL_HALO, dp), lambda i: (jnp.minimum((i + 1) * hb, n_hblocks - 1), 0)),
            _full(w_pool.shape), _full(pool_scale.shape),
        ],
        out_specs=pl.BlockSpec((tm, dp), lambda i: (i, 0)),
        out_shape=jax.ShapeDtypeStruct((t, dp), F32),
        scratch_shapes=[pltpu.VMEM((tm + 2 * POOL_HALO, dp), F32)],
        compiler_params=_cparams(("arbitrary",)),
        name="pool",
    )(u_pool, u_pool, u_pool, w_pool, pool_scale)


def _s5_prep_kernel(are_ref, aim_ref, ls_ref, bbr_ref, bbi_ref, cbr_ref, cbi_ref,
                    ks_ref, bb_ref, cb_ref, we_ref, wc_ref, lt_ref):
    tc = S5_CHUNK
    fwd = pl.program_id(0) == 0
    step = jnp.exp(ls_ref[0, 0])
    ar = are_ref[0, 0]
    ai = aim_ref[0, 0]
    la = ar * step
    th = ai * step

    def powtab(m):
        mg = jnp.exp(m * la)
        an = m * th
        return mg * jnp.cos(an), mg * jnp.sin(an)

    lam_r, lam_i = powtab(1.0)
    den = ar * ar + ai * ai
    nr = lam_r - 1.0
    coef_r = (nr * ar + lam_i * ai) / den
    coef_i = (lam_i * ar - nr * ai) / den
    bp_r, bp_i = _cmul(coef_r, coef_i, bbr_ref[0, 0], bbi_ref[0, 0])
    bb_ref[0, 0] = jnp.concatenate([bp_r, bp_i], axis=-1).astype(BF16)
    cbr = cbr_ref[0, 0]
    cbi = cbi_ref[0, 0]
    cb_ref[0, 0] = jnp.concatenate([cbr, -cbi], axis=0).astype(BF16)
    jcol = lax.broadcasted_iota(jnp.int32, (tc, 1), 0).astype(F32)
    lag_r, lag_i = powtab(jcol)
    cbr_b = cbr.astype(BF16)
    cbi_b = cbi.astype(BF16)
    for m in range(tc):
        bm_r, bm_i = _cmul(bp_r, bp_i, lag_r[m:m + 1, :], lag_i[m:m + 1, :])
        k_m = (jnp.dot(bm_r.astype(BF16), cbr_b, preferred_element_type=F32)
               - jnp.dot(bm_i.astype(BF16), cbi_b, preferred_element_type=F32))
        ks_ref[0, 0, m * LANES:(m + 1) * LANES, :] = k_m.astype(BF16)
    e_r, e_i = powtab(jnp.where(fwd, (tc - 1.0) - jcol, jcol))
    c_r, c_i = powtab(jnp.where(fwd, jcol + 1.0, tc - jcol))
    we_ref[0, 0] = jnp.concatenate([e_r, e_i], axis=-1)
    wc_ref[0, 0] = jnp.concatenate([c_r, c_i], axis=-1)
    t_r, t_i = powtab(float(tc))
    lt_ref[0, 0] = jnp.concatenate([t_r, t_i], axis=-1)


def _s5_prep(a_re, a_im, log_step, b_re, b_im, c_re, c_im):
    _, g, p = a_re.shape
    c = b_re.shape[-1]
    gq = LANES // c
    nq = g // gq
    sq = gq * p
    tc = S5_CHUNK
    eye = jnp.eye(gq, dtype=F32)
    lanes_of = lambda a: a.reshape(2, nq, 1, sq)
    ls = jnp.broadcast_to(log_step[:, :, None], (2, g, p))
    bbd = lambda b: jnp.einsum("dqgpc,gh->dqgchp", b.reshape(2, nq, gq, p, c), eye).reshape(2, nq, gq * c, sq)
    cbd = lambda m: jnp.einsum("dqgcp,gh->dqgphc", m.reshape(2, nq, gq, c, p), eye).reshape(2, nq, sq, gq * c)
    blk = lambda *s: pl.BlockSpec((1, 1) + s, lambda d, q: (d, q, 0, 0))
    shp = lambda s, dt: jax.ShapeDtypeStruct((2, nq) + s, dt)
    return pl.pallas_call(
        _s5_prep_kernel,
        grid=(2, nq),
        in_specs=[blk(1, sq), blk(1, sq), blk(1, sq), blk(LANES, sq), blk(LANES, sq), blk(sq, LANES),
                  blk(sq, LANES)],
        out_specs=[blk(tc * LANES, LANES), blk(LANES, 2 * sq), blk(2 * sq, LANES), blk(tc, 2 * sq),
                   blk(tc, 2 * sq), blk(1, 2 * sq)],
        out_shape=[shp((tc * LANES, LANES), BF16), shp((LANES, 2 * sq), BF16), shp((2 * sq, LANES), BF16),
                   shp((tc, 2 * sq), F32), shp((tc, 2 * sq), F32), shp((1, 2 * sq), F32)],
        compiler_params=_cparams(("arbitrary", "arbitrary")),
        name="s5_prep",
    )(lanes_of(a_re), lanes_of(a_im), lanes_of(ls), bbd(b_re), bbd(b_im), cbd(c_re), cbd(c_im))


def _cmul(ar, ai, br, bi):
    return ar * br - ai * bi, ar * bi + ai * br


def _s5_kernel(*refs, tm, rev, has_prev):
    if has_prev:
        u_ref, prev_ref, ks_ref, bb_ref, cb_ref, we_ref, wc_ref, lt_ref, y_ref, pad_ref, st_ref = refs
    else:
        u_ref, ks_ref, bb_ref, cb_ref, we_ref, wc_ref, lt_ref, y_ref, pad_ref, st_ref = refs
    tc = S5_CHUNK
    nck = tm // tc
    nq = u_ref.shape[1] // LANES
    sq = S5_TILE_STATES

    @pl.when(pl.program_id(1) == 0)
    def _():
        st_ref[...] = jnp.zeros_like(st_ref)

    pad_ref[0:tc, :] = jnp.zeros((tc, LANES), F32)
    pad_ref[tc + tm:, :] = jnp.zeros((tc, LANES), F32)
    jpos = lax.broadcasted_iota(jnp.int32, (tm, 1), 0) % tc
    order = range(nck - 1, -1, -1) if rev else range(nck)
    for q in range(nq):
        lanes = slice(q * LANES, (q + 1) * LANES)
        u = u_ref[:, lanes]
        pad_ref[tc:tc + tm, :] = u
        cols = []
        for m in range(tc):
            if rev:
                sh = pad_ref[tc + m:tc + m + tm, :]
                keep = jpos + m < tc
            else:
                sh = pad_ref[tc - m:tc - m + tm, :]
                keep = jpos >= m
            cols.append(jnp.where(keep, sh, 0.0).astype(BF16))
        y = jnp.dot(jnp.concatenate(cols, axis=-1), ks_ref[0, q], preferred_element_type=F32)

        x = jnp.dot(u.astype(BF16), bb_ref[0, q], preferred_element_type=F32)
        we = jnp.tile(we_ref[0, q], (nck, 1))
        er, ei = _cmul(x[:, :sq], x[:, sq:], we[:, :sq], we[:, sq:])
        er = jnp.sum(er.reshape(nck, tc, sq), axis=1)
        ei = jnp.sum(ei.reshape(nck, tc, sq), axis=1)
        lam_r = lt_ref[0, q][:, :sq]
        lam_i = lt_ref[0, q][:, sq:]
        s_r = st_ref[2 * q:2 * q + 1, :]
        s_i = st_ref[2 * q + 1:2 * q + 2, :]
        ent_r, ent_i = [None] * nck, [None] * nck
        for k in order:
            ent_r[k] = jnp.broadcast_to(s_r, (tc, sq))
            ent_i[k] = jnp.broadcast_to(s_i, (tc, sq))
            n_r, n_i = _cmul(lam_r, lam_i, s_r, s_i)
            s_r = n_r + er[k:k + 1, :]
            s_i = n_i + ei[k:k + 1, :]
        st_ref[2 * q:2 * q + 1, :] = s_r
        st_ref[2 * q + 1:2 * q + 2, :] = s_i
        wc = jnp.tile(wc_ref[0, q], (nck, 1))
        z_r, z_i = _cmul(wc[:, :sq], wc[:, sq:], jnp.concatenate(ent_r, axis=0), jnp.concatenate(ent_i, axis=0))
        y = y + jnp.dot(jnp.concatenate([z_r, z_i], axis=-1).astype(BF16), cb_ref[0, q],
                        preferred_element_type=F32)
        y_ref[:, lanes] = (prev_ref[:, lanes] + y) if has_prev else y


def _s5(u_ssm, prep, *, n_batch, seq, ctx_len):
    t, dw = u_ssm.shape
    tm = ROW_TILE
    nq = dw // LANES
    tc = S5_CHUNK
    sq2 = 2 * S5_TILE_STATES
    nct, nlt = ctx_len // tm, seq // tm
    lat_tiles = n_batch * nlt
    ks, bb, cb, we, wc, lt = prep

    def tile_of(rev):
        def f(b, s):
            cpos = (nct - 1 - s) if rev else s
            lpos = (nlt - 1 - (s - nct)) if rev else (s - nct)
            return jnp.where(s < nct, lat_tiles + b * nct + cpos, b * nlt + lpos)
        return f

    y = None
    for d, rev in enumerate((False, True)):
        tile = tile_of(rev)
        row = pl.BlockSpec((tm, dw), lambda b, s: (tile(b, s), 0))
        par = lambda *shape: pl.BlockSpec((1, nq) + shape, lambda b, s: (d, 0, 0, 0))
        has_prev = y is not None
        kern = functools.partial(_s5_kernel, tm=tm, rev=rev, has_prev=has_prev)
        y = pl.pallas_call(
            kern,
            grid=(n_batch, nct + nlt),
            in_specs=[row] * (2 if has_prev else 1) + [par(tc * LANES, LANES), par(LANES, sq2), par(sq2, LANES),
                                                      par(tc, sq2), par(tc, sq2), par(1, sq2)],
            out_specs=row,
            out_shape=jax.ShapeDtypeStruct((t, dw), F32),
            scratch_shapes=[pltpu.VMEM((tm + 2 * tc, LANES), F32), pltpu.VMEM((2 * nq, S5_TILE_STATES), F32)],
            compiler_params=_cparams(("arbitrary", "arbitrary")),
            name="s5_scan",
        )(*([u_ssm] + ([y] if has_prev else []) + [ks, bb, cb, we, wc, lt]))
    return y


_NT = (((1,), (1,)), ((), ()))


def _attn_lat_kernel(q_ref, kl_ref, vl_ref, kc_ref, vc_ref, o_ref):
    q = q_ref[...]
    tq = q.shape[0]
    seq = kl_ref.shape[0]
    kc = min(ATTN_KEY_CHUNK, seq)
    chunks = [(kl_ref, vl_ref, c * kc, kc) for c in range(seq // kc)] + [(kc_ref, vc_ref, 0, kc_ref.shape[0])]
    m = jnp.full((tq, 1), NEG_BIG, F32)
    l = jnp.zeros((tq, 1), F32)
    acc = jnp.zeros((tq, V_HEAD), F32)
    for k_ref, v_ref, off, n in chunks:
        s = lax.dot_general(q, k_ref[off:off + n, :], _NT, preferred_element_type=F32)
        m_new = jnp.maximum(m, jnp.max(s, axis=-1, keepdims=True))
        alpha = jnp.exp(m - m_new)
        p = jnp.exp(s - m_new)
        l = alpha * l + jnp.sum(p, axis=-1, keepdims=True)
        acc = alpha * acc + jnp.dot(p.astype(BF16), v_ref[off:off + n, :], preferred_element_type=F32)
        m = m_new
    o_ref[...] = acc / l


def _attn_ctx_kernel(q_ref, kc_ref, vc_ref, o_ref):
    s = lax.dot_general(q_ref[...], kc_ref[...], _NT, preferred_element_type=F32)
    p = jnp.exp(s - jnp.max(s, axis=-1, keepdims=True))
    l = jnp.sum(p, axis=-1, keepdims=True)
    o_ref[...] = jnp.dot(p.astype(BF16), vc_ref[...], preferred_element_type=F32) / l


def _attention(q, k, v, *, n_batch, seq, ctx_len, ctx_queries):
    n_heads = v.shape[1] // V_HEAD
    tq = min(ATTN_Q_TILE, seq)
    nq = seq // tq
    cblk = n_batch * seq // ctx_len
    y_lat = pl.pallas_call(
        _attn_lat_kernel,
        grid=(n_batch, n_heads, nq),
        in_specs=[
            pl.BlockSpec((tq, QK_PAD), lambda b, h, i: (b * nq + i, h)),
            pl.BlockSpec((seq, QK_PAD), lambda b, h, i: (b, h)),
            pl.BlockSpec((seq, V_HEAD), lambda b, h, i: (b, h)),
            pl.BlockSpec((ctx_len, QK_PAD), lambda b, h, i: (cblk + b, h)),
            pl.BlockSpec((ctx_len, V_HEAD), lambda b, h, i: (cblk + b, h)),
        ],
        out_specs=pl.BlockSpec((tq, V_HEAD), lambda b, h, i: (b * nq + i, h)),
        out_shape=jax.ShapeDtypeStruct((n_batch * seq, n_heads * V_HEAD), F32),
        compiler_params=_cparams(("arbitrary", "arbitrary", "arbitrary")),
        name="attn_latent",
    )(q, k, v, k, v)
    if not ctx_queries:
        return y_lat, None
    y_ctx = pl.pallas_call(
        _attn_ctx_kernel,
        grid=(n_batch, n_heads),
        in_specs=[
            pl.BlockSpec((ctx_len, QK_PAD), lambda b, h: (cblk + b, h)),
            pl.BlockSpec((ctx_len, QK_PAD), lambda b, h: (cblk + b, h)),
            pl.BlockSpec((ctx_len, V_HEAD), lambda b, h: (cblk + b, h)),
        ],
        out_specs=pl.BlockSpec((ctx_len, V_HEAD), lambda b, h: (b, h)),
        out_shape=jax.ShapeDtypeStruct((n_batch * ctx_len, n_heads * V_HEAD), F32),
        compiler_params=_cparams(("arbitrary", "arbitrary")),
        name="attn_context",
    )(q, k, v)
    return y_lat, y_ctx


def _merge_kernel(x_ref, yp_ref, ys_ref, us_ref, yml_ref, *rest, tiles_per_seq, n_batch, d_model, has_ctx):
    if has_ctx:
        ymc_ref, rest = rest[0], rest[1:]
    (mod_ref, sd_ref, wglu_ref, bglu_ref, on_ref, wout_ref, nffn_ref, wr_ref, br_ref,
     x1_ref, h2_ref, idx_ref, gate_ref, rank_ref, cnt_ref, carry_ref) = rest
    d = d_model
    d4 = d // 4
    i = pl.program_id(0)
    tm = x_ref.shape[0]
    r = jnp.minimum(i // tiles_per_seq, n_batch)
    g1 = mod_ref[pl.ds(r, 1), pl.ds(2 * d, d)]
    sh2 = mod_ref[pl.ds(r, 1), pl.ds(3 * d, d)]
    sc2 = mod_ref[pl.ds(r, 1), pl.ds(4 * d, d)]

    z = ys_ref[...] + sd_ref[...] * us_ref[...]
    g = jax.nn.gelu(z, approximate=True)
    glu = jnp.dot(g.astype(BF16), wglu_ref[...], preferred_element_type=F32) + bglu_ref[...]
    y_ssm = g * jax.nn.sigmoid(glu)
    on = on_ref[...]
    y_mla = yml_ref[...]
    if has_ctx:
        y_mla = jnp.where(i < n_batch * tiles_per_seq, y_mla, ymc_ref[...])
    cat = jnp.concatenate([_rms(yp_ref[...], on[:, 0:d4]), _rms(y_ssm, on[:, d4:2 * d4]),
                           _rms(y_mla, on[:, 2 * d4:])], axis=-1).astype(BF16)
    x1 = x_ref[...] + g1 * jnp.dot(cat, wout_ref[...], preferred_element_type=F32)
    x1_ref[...] = x1
    h2 = (_rms(x1, nffn_ref[...]) * (1.0 + sc2) + sh2).astype(BF16)
    h2_bits = lax.bitcast_convert_type(h2.astype(F32), jnp.uint32)
    h2_ref[...] = (h2_bits[:, :d // 2] >> 16) | (h2_bits[:, d // 2:] & jnp.uint32(0xFFFF0000))

    logits = jnp.dot(h2, wr_ref[...], preferred_element_type=F32) + br_ref[...]
    lane = lax.broadcasted_iota(jnp.int32, (tm, LANES), 1)
    vals, hots = [], []
    idx_out = jnp.zeros((tm, LANES), jnp.int32)
    for kk in range(TOP_K):
        mx = jnp.max(logits, axis=-1, keepdims=True)
        sel = jnp.min(jnp.where(logits == mx, lane, LANES), axis=-1, keepdims=True)
        hot = lane == sel
        vals.append(mx)
        hots.append(hot)
        idx_out = jnp.where(lane == kk, sel, idx_out)
        logits = jnp.where(hot, NEG_BIG * 2.0, logits)
    es = [jnp.exp(vv - vals[0]) for vv in vals]
    den = es[0] + es[1] + es[2] + es[3]
    gate_out = jnp.zeros((tm, LANES), F32)
    for kk in range(TOP_K):
        gate_out = jnp.where(lane == kk, es[kk] / den, gate_out)
    idx_ref[...] = idx_out
    gate_ref[...] = gate_out

    @pl.when(i == 0)
    def _():
        carry_ref[...] = jnp.zeros_like(carry_ref)

    onehot = (hots[0] | hots[1] | hots[2] | hots[3]).astype(BF16)
    rr = lax.broadcasted_iota(jnp.int32, (tm, tm), 0)
    cc = lax.broadcasted_iota(jnp.int32, (tm, tm), 1)
    tri = (cc < rr).astype(BF16)
    pos = jnp.dot(tri, onehot, preferred_element_type=F32) + carry_ref[...]
    rank_out = jnp.zeros((tm, LANES), jnp.int32)
    for kk in range(TOP_K):
        rk = jnp.sum(jnp.where(hots[kk], pos, 0.0), axis=-1, keepdims=True).astype(jnp.int32)
        rank_out = jnp.where(lane == kk, rk, rank_out)
    rank_ref[...] = rank_out
    carry = carry_ref[...] + jnp.sum(onehot.astype(F32), axis=0, keepdims=True)
    carry_ref[...] = carry
    cnt_ref[...] = carry.astype(jnp.int32)


def _merge(xt, y_pool, y_s5, u_ssm, y_mla_lat, y_mla_ctx, mods, ssm_d, wglu, bglu, out_norm, wout, nffn, wr, br, *,
           n_rows, n_batch, seq):
    d = xt.shape[1]
    tm = ROW_TILE
    n_lat_tiles = n_batch * seq // tm
    has_ctx = y_mla_ctx is not None
    row = lambda w: pl.BlockSpec((tm, w), lambda i: (i, 0))
    kern = functools.partial(_merge_kernel, tiles_per_seq=seq // tm, n_batch=n_batch, d_model=d, has_ctx=has_ctx)
    small = [mods, ssm_d, wglu, bglu, out_norm, wout, nffn, wr, br]
    y_mla = [y_mla_lat] + ([y_mla_ctx] if has_ctx else [])
    y_specs = [pl.BlockSpec((tm, d // 2), lambda i: (jnp.minimum(i, n_lat_tiles - 1), 0))]
    if has_ctx:
        y_specs.append(pl.BlockSpec((tm, d // 2), lambda i: (jnp.maximum(i - n_lat_tiles, 0), 0)))
    return pl.pallas_call(
        kern,
        grid=(n_rows // tm,),
        in_specs=[row(d), row(d // 4), row(d // 4), row(d // 4)] + y_specs + [_full(a.shape) for a in small],
        out_specs=[row(d), row(d // 2), row(LANES), row(LANES), row(LANES), _full((1, LANES))],
        out_shape=[jax.ShapeDtypeStruct((n_rows, d), F32), jax.ShapeDtypeStruct((n_rows, d // 2), jnp.uint32),
                   jax.ShapeDtypeStruct((n_rows, LANES), jnp.int32), jax.ShapeDtypeStruct((n_rows, LANES), F32),
                   jax.ShapeDtypeStruct((n_rows, LANES), jnp.int32), jax.ShapeDtypeStruct((1, LANES), jnp.int32)],
        scratch_shapes=[pltpu.VMEM((1, LANES), F32)],
        compiler_params=_cparams(("arbitrary",)),
        name="merge_router",
    )(xt, y_pool, y_s5, u_ssm, *y_mla, *small)


def _row_copy(src, dst, sem, src_row, dst_row):
    return pltpu.make_async_copy(src.at[pl.ds(src_row, 1), :], dst.at[pl.ds(dst_row, 1), :], sem)


def _dispatch_kernel(dest_hbm, h2_ref, zeros_hbm, xs_hbm, dest_smem, sem_idx, sem_rows, *, tm):
    del zeros_hbm
    i = pl.program_id(0)
    n = tm * TOP_K
    idx_copy = pltpu.make_async_copy(dest_hbm.at[i], dest_smem, sem_idx)
    idx_copy.start()
    idx_copy.wait()

    def issue(r, _):
        for kk in range(TOP_K):
            _row_copy(h2_ref, xs_hbm, sem_rows, r, dest_smem[r * TOP_K + kk]).start(priority=kk % 2)
        return 0

    lax.fori_loop(0, tm, issue, 0, unroll=2)

    def drain(r, _):
        for kk in range(TOP_K):
            _row_copy(h2_ref, xs_hbm, sem_rows, r, dest_smem[r * TOP_K + kk]).wait()
        return 0

    lax.fori_loop(0, tm, drain, 0, unroll=2)


def _dispatch(dest_tiles, h2_packed, n_sorted_rows):
    n_tok, w = h2_packed.shape
    tm = ROW_TILE
    zeros = jnp.zeros((n_sorted_rows, w), h2_packed.dtype)
    return pl.pallas_call(
        functools.partial(_dispatch_kernel, tm=tm),
        grid=(n_tok // tm,),
        in_specs=[pl.BlockSpec(memory_space=pl.ANY), pl.BlockSpec((tm, w), lambda i: (i, 0)),
                  pl.BlockSpec(memory_space=pl.ANY)],
        out_specs=pl.BlockSpec(memory_space=pl.ANY),
        out_shape=jax.ShapeDtypeStruct(zeros.shape, zeros.dtype),
        input_output_aliases={2: 0},
        scratch_shapes=[pltpu.SMEM((tm * TOP_K,), jnp.int32), pltpu.SemaphoreType.DMA, pltpu.SemaphoreType.DMA],
        compiler_params=_cparams(("arbitrary",)),
        name="moe_dispatch",
    )(dest_tiles, h2_packed, zeros)


def _expert_kernel(ie_ref, ix_ref, io_ref, ns_ref, nz_ref, x_ref, wg_ref, bg_ref, wu_ref, bu_ref, wd_ref, bd_ref,
                   o_hbm, act_ref, xb_ref, res_ref, zero_ref, sem_out, sem_zero, *, nf):
    del ie_ref, ix_ref
    it = pl.program_id(0)
    j = pl.program_id(1)
    ns = ns_ref[it]
    rows_max = x_ref.shape[0]
    half = x_ref.shape[1]
    tw = res_ref.shape[2]
    sub = EXPERT_SUB_ROWS
    o_row = pl.multiple_of(io_ref[it] * sub, sub)
    n = j - nf

    def out_copy(step, mr):
        col = pl.multiple_of(step * tw, tw)
        return pltpu.make_async_copy(res_ref.at[step % 2, 0:mr, :], o_hbm.at[pl.ds(o_row, mr), pl.ds(col, tw)],
                                     sem_out.at[step % 2])

    @pl.when((ns > 0) & (j == 0))
    def _():
        words = x_ref[...]
        xb_ref[:, 0:half] = lax.bitcast_convert_type(words << 16, F32).astype(BF16)
        xb_ref[:, half:] = lax.bitcast_convert_type(words & jnp.uint32(0xFFFF0000), F32).astype(BF16)

    for m in EXPERT_ROW_VARIANTS:
        mr = m * EXPERT_SUB_ROWS

        @pl.when((ns == m) & (j < nf))
        def _():
            x = xb_ref[0:mr, :]
            g = jnp.dot(x, wg_ref[0, 0].astype(BF16), preferred_element_type=F32) + bg_ref[0, 0]
            u = jnp.dot(x, wu_ref[0, 0].astype(BF16), preferred_element_type=F32) + bu_ref[0, 0]
            g = jnp.minimum(g, SWIGLU_LIMIT)
            u = jnp.clip(u, -SWIGLU_LIMIT, SWIGLU_LIMIT)
            act_ref[j, 0:mr, :] = (g * jax.nn.sigmoid(SWIGLU_ALPHA * g) * (u + 1.0)).astype(BF16)

        @pl.when((ns == m) & (j >= nf))
        def _():
            @pl.when(n >= 2)
            def _():
                out_copy(n - 2, mr).wait()

            act = jnp.concatenate([act_ref[f, 0:mr, :] for f in range(nf)], axis=-1)
            res = jnp.dot(act, wd_ref[0, 0].astype(BF16), preferred_element_type=F32) + bd_ref[0, 0]
            bits = lax.bitcast_convert_type(res.astype(BF16).astype(F32), jnp.uint32)
            res_ref[n % 2, 0:mr, :] = (bits[:, :tw] >> 16) | (bits[:, tw:] & jnp.uint32(0xFFFF0000))
            out_copy(n, mr).start()

            @pl.when(n == nf - 1)
            def _():
                out_copy(n - 1, mr).wait()
                out_copy(n, mr).wait()

    nz = nz_ref[it]

    @pl.when((nz > 0) & (j == 2 * nf - 1))
    def _():
        zero_ref[...] = jnp.zeros(zero_ref.shape, jnp.uint32)

        def zero_copy(k):
            rows0 = pl.multiple_of(o_row + k * sub, sub)
            return pltpu.make_async_copy(zero_ref, o_hbm.at[pl.ds(rows0, sub), :], sem_zero)

        def start(k, _):
            zero_copy(k).start()
            return 0

        def wait(k, _):
            zero_copy(k).wait()
            return 0

        lax.fori_loop(0, nz, start, 0)
        lax.fori_loop(0, nz, wait, 0)


def _experts(layer, items, xs, w_gate, b_gate, w_up, b_up, w_down, b_down):
    item_e, item_x, item_o, item_ns, item_nz = items
    n_items = item_e.shape[0]
    n_layers, n_exp, d, f_dim = w_gate.shape
    tf = EXPERT_TILE
    nf = f_dim // tf
    rows = EXPERT_ITEM_ROWS
    assert d // tf == nf and nf >= 2 and max(EXPERT_ROW_VARIANTS) * EXPERT_SUB_ROWS == rows

    def fi(j, ns, it):
        return jnp.where(ns[it] > 0, jnp.minimum(j, nf - 1), nf - 1)

    def ni(j, ns, it):
        return jnp.where(ns[it] > 0, jnp.maximum(j - nf, 0), nf - 1)

    return pl.pallas_call(
        functools.partial(_expert_kernel, nf=nf),
        grid_spec=pltpu.PrefetchScalarGridSpec(
            num_scalar_prefetch=5,
            grid=(n_items, 2 * nf),
            in_specs=[
                pl.BlockSpec((pl.Element(rows), pl.Element(d // 2)),
                             lambda it, j, ie, ix, io, ns, wk: (ix[it] * EXPERT_SUB_ROWS, 0),
                             pipeline_mode=pl.Buffered(1)),
                pl.BlockSpec((1, 1, d, tf), lambda it, j, ie, ix, io, ns, wk: (layer, ie[it], 0, fi(j, ns, it))),
                pl.BlockSpec((1, 1, 1, tf), lambda it, j, ie, ix, io, ns, wk: (layer, ie[it], 0, fi(j, ns, it))),
                pl.BlockSpec((1, 1, d, tf), lambda it, j, ie, ix, io, ns, wk: (layer, ie[it], 0, fi(j, ns, it))),
                pl.BlockSpec((1, 1, 1, tf), lambda it, j, ie, ix, io, ns, wk: (layer, ie[it], 0, fi(j, ns, it))),
                pl.BlockSpec((1, 1, f_dim, tf), lambda it, j, ie, ix, io, ns, wk: (layer, ie[it], 0, ni(j, ns, it))),
                pl.BlockSpec((1, 1, 1, tf), lambda it, j, ie, ix, io, ns, wk: (layer, ie[it], 0, ni(j, ns, it))),
            ],
            out_specs=pl.BlockSpec(memory_space=pl.ANY),
            scratch_shapes=[pltpu.VMEM((nf, rows, tf), BF16), pltpu.VMEM((rows, d), BF16),
                            pltpu.VMEM((2, rows, tf // 2), jnp.uint32),
                            pltpu.VMEM((EXPERT_SUB_ROWS, d // 2), jnp.uint32),
                            pltpu.SemaphoreType.DMA((2,)), pltpu.SemaphoreType.DMA],
        ),
        out_shape=jax.ShapeDtypeStruct((xs.shape[0], d // 2), jnp.uint32),
        compiler_params=_cparams(("arbitrary", "arbitrary"), EXPERT_VMEM_LIMIT),
        name="moe_experts",
    )(item_e, item_x, item_o, item_ns, item_nz, xs, w_gate, b_gate.reshape(n_layers, n_exp, 1, f_dim), w_up,
      b_up.reshape(n_layers, n_exp, 1, f_dim), w_down, b_down.reshape(n_layers, n_exp, 1, d))


def _combine_kernel(dest_hbm, os_hbm, gate_ref, x_ref, mod_ref, o_ref, dest_smem, buf, sem_idx, sem_rows,
                    *, tc, n_tiles, tiles_per_seq, n_batch):
    i = pl.program_id(0)
    n = tc * TOP_K
    slot = i % 2
    d = x_ref.shape[1]

    def gather(s, r, kk):
        return _row_copy(os_hbm, buf.at[s, kk], sem_rows.at[s], dest_smem[s, r * TOP_K + kk], r)

    def fetch(tile, s):
        idx_copy = pltpu.make_async_copy(dest_hbm.at[tile], dest_smem.at[s], sem_idx)
        idx_copy.start()
        idx_copy.wait()

        def issue(r, _):
            for kk in range(TOP_K):
                gather(s, r, kk).start(priority=kk % 2)
            return 0

        lax.fori_loop(0, tc, issue, 0, unroll=2)

    @pl.when(i == 0)
    def _():
        fetch(0, 0)

    @pl.when(i + 1 < n_tiles)
    def _():
        fetch(i + 1, 1 - slot)

    def drain(r, _):
        for kk in range(TOP_K):
            gather(slot, r, kk).wait()
        return 0

    lax.fori_loop(0, tc, drain, 0, unroll=2)
    g2 = mod_ref[pl.ds(jnp.minimum(i // tiles_per_seq, n_batch), 1), pl.ds(5 * d, d)]
    gates = gate_ref[...]
    lo = hi = None
    for kk in range(TOP_K):
        words = buf[slot, kk]
        g = gates[:, kk:kk + 1]
        lo_k = g * lax.bitcast_convert_type(words << 16, F32)
        hi_k = g * lax.bitcast_convert_type(words & jnp.uint32(0xFFFF0000), F32)
        lo = lo_k if lo is None else lo + lo_k
        hi = hi_k if hi is None else hi + hi_k
    tw = EXPERT_TILE // 2
    pieces = []
    for c in range(0, d // 2, tw):
        pieces += [lo[:, c:c + tw], hi[:, c:c + tw]]
    o_ref[...] = x_ref[...] + g2 * jnp.concatenate(pieces, axis=-1)


def _combine(dest_tiles, out_sorted, gates, x1, mods, *, n_batch, seq):
    n_tok, d = x1.shape
    tc = COMBINE_TILE
    kern = functools.partial(_combine_kernel, tc=tc, n_tiles=n_tok // tc, tiles_per_seq=seq // tc, n_batch=n_batch)
    return pl.pallas_call(
        kern,
        grid=(n_tok // tc,),
        in_specs=[pl.BlockSpec(memory_space=pl.ANY)] * 2 + [
            pl.BlockSpec((tc, LANES), lambda i: (i, 0)), pl.BlockSpec((tc, d), lambda i: (i, 0)),
            _full(mods.shape)],
        out_specs=pl.BlockSpec((tc, d), lambda i: (i, 0)),
        out_shape=jax.ShapeDtypeStruct(x1.shape, F32),
        scratch_shapes=[pltpu.SMEM((2, tc * TOP_K), jnp.int32), pltpu.VMEM((2, TOP_K, tc, d // 2), jnp.uint32),
                        pltpu.SemaphoreType.DMA, pltpu.SemaphoreType.DMA((2,))],
        compiler_params=_cparams(("arbitrary",)),
        name="moe_combine",
    )(dest_tiles, out_sorted, gates, x1, mods)


def _rope_tables(n_batch, seq, ctx_len):
    rows = seq // GRID_W
    row = jnp.broadcast_to(jnp.arange(rows)[:, None], (rows, GRID_W)).reshape(-1).astype(F32)
    col = jnp.broadcast_to(jnp.arange(GRID_W)[None, :], (rows, GRID_W)).reshape(-1).astype(F32)
    half = QK_ROPE // 2
    inv_freq = ROPE_BASE ** (-jnp.arange(0, half, 2, dtype=F32) / half)
    ang = jnp.concatenate([row[:, None] * inv_freq, col[:, None] * inv_freq], axis=-1)
    cos, sin = jnp.cos(ang), jnp.sin(ang)
    zpad = jnp.zeros((seq, LANES - QK_ROPE), F32)
    cos_l = jnp.concatenate([cos, cos, zpad], axis=-1)
    sin_l = jnp.concatenate([-sin, sin, zpad], axis=-1)
    n_ctx = n_batch * ctx_len
    cos_c = jnp.concatenate([jnp.ones((n_ctx, QK_ROPE), F32), jnp.zeros((n_ctx, LANES - QK_ROPE), F32)], axis=-1)
    cos_t = jnp.concatenate([jnp.tile(cos_l, (n_batch, 1)), cos_c], axis=0)
    sin_t = jnp.concatenate([jnp.tile(sin_l, (n_batch, 1)), jnp.zeros((n_ctx, LANES), F32)], axis=0)
    return cos_t, sin_t


def _pad_head_vec(vec):
    return jnp.concatenate([vec, jnp.zeros((QK_PAD - QK_HEAD,), vec.dtype)])[None, :]


def _routing(idx, rank, counts, n_tok, n_exp):
    rows, sub = EXPERT_ITEM_ROWS, EXPERT_SUB_ROWS
    n = n_tok * TOP_K
    cap = n + n_exp * sub + rows
    n_items = -(-n // rows) + n_exp + 1
    i32 = lambda a: a.astype(jnp.int32)
    cnt = counts[0, :n_exp]
    seg = ((cnt + sub - 1) // sub) * sub
    seg_end = jnp.cumsum(seg)
    seg_start = seg_end - seg
    dest = seg_start[idx] + rank
    used_end = seg_end[-1]
    per = (cnt + rows - 1) // rows
    it_end = jnp.cumsum(per)
    it_start = it_end - per
    total = it_end[-1]
    it = jnp.arange(n_items, dtype=jnp.int32)
    it_c = jnp.minimum(it, total - 1)
    e_of = i32(jnp.minimum(jnp.searchsorted(it_end, it_c, side="right"), n_exp - 1))
    k_in = it_c - it_start[e_of]
    x_row = seg_start[e_of] + k_in * rows
    nsub = jnp.clip((cnt[e_of] - k_in * rows + sub - 1) // sub, min(EXPERT_ROW_VARIANTS), max(EXPERT_ROW_VARIANTS))
    active = it < total
    nsub = jnp.where(active, nsub, 0)
    first_idle = it == total
    o_row = jnp.where(active, x_row, jnp.where(first_idle, used_end, 0))
    n_zero = jnp.where(first_idle, (cap - used_end) // sub, 0)
    return i32(dest), (e_of, i32(x_row // sub), i32(o_row // sub), i32(nsub), i32(n_zero)), cap


def _layer(layer, xt, mods, cos_t, sin_t, p, expert_params, *, n_batch, seq, ctx_len, ctx_out):
    t, d = xt.shape
    d4 = d // 4
    n_heads = (d - 2 * d4) // V_HEAD
    kv_lora = d // 8
    bf = lambda a: a.astype(BF16)
    row = lambda a: a[None, :]

    win = bf(jnp.pad(p["w_in"], ((0, 0), (0, LANES - QK_ROPE))))
    wuq = p["w_uq"].reshape(-1, n_heads, QK_HEAD)
    wuq = bf(jnp.pad(wuq, ((0, 0), (0, 0), (0, QK_PAD - QK_HEAD))).reshape(-1, n_heads * QK_PAD))
    wukv = p["w_ukv"].reshape(kv_lora, n_heads, QK_NOPE + V_HEAD)
    wk = bf(wukv[:, :, :QK_NOPE].reshape(kv_lora, n_heads * QK_NOPE))
    wv = bf(wukv[:, :, QK_NOPE:].reshape(kv_lora, n_heads * V_HEAD))
    qn = _pad_head_vec(p["q_norm"]) * (QK_HEAD ** -0.5)
    kn = _pad_head_vec(p["k_norm"])

    u_pool, u_ssm, q, k, v = _proj(xt, mods, cos_t, sin_t, row(p["norm_mix"]), win, row(p["q_a_norm"]), wuq,
                                   row(p["kv_a_norm"]), wk, wv, qn, kn, n_batch=n_batch, seq=seq)

    y_pool = _pool(u_pool, bf(p["w_pool"]), row(p["pool_scale"]), n_batch=n_batch, seq=seq, ctx_len=ctx_len)

    prep = _s5_prep(p["ssm_a_re"], p["ssm_a_im"], p["ssm_log_step"], p["ssm_b_re"], p["ssm_b_im"],
                    p["ssm_c_re"], p["ssm_c_im"])
    y_s5 = _s5(u_ssm, prep, n_batch=n_batch, seq=seq, ctx_len=ctx_len)

    y_mla_lat, y_mla_ctx = _attention(q, k, v, n_batch=n_batch, seq=seq, ctx_len=ctx_len, ctx_queries=ctx_out)

    n_rows = t if ctx_out else n_batch * seq
    n_exp = p["w_router"].shape[1]
    wr = bf(jnp.pad(p["w_router"], ((0, 0), (0, LANES - n_exp))))
    br = jnp.concatenate([p["b_router"], jnp.full((LANES - n_exp,), NEG_BIG, F32)])[None, :]
    x1, h2, idx, gates, rank, counts = _merge(
        xt, y_pool, y_s5, u_ssm, y_mla_lat, y_mla_ctx, mods, row(p["ssm_d"]), bf(p["w_glu"]), row(p["b_glu"]),
        row(p["out_norm"]), bf(p["w_out"]), row(p["norm_ffn"]), wr, br, n_rows=n_rows, n_batch=n_batch, seq=seq)

    dest, items, cap = _routing(idx[:, :TOP_K], rank[:, :TOP_K], counts, n_rows, n_exp)
    xs = _dispatch(dest.reshape(n_rows // ROW_TILE, ROW_TILE * TOP_K), h2, cap)
    out_sorted = _experts(layer, items, xs, *expert_params)
    return _combine(dest.reshape(n_rows // COMBINE_TILE, COMBINE_TILE * TOP_K), out_sorted, gates, x1, mods,
                    n_batch=n_batch, seq=seq)


_PARAM_NAMES = ("norm_mix", "norm_ffn", "w_in", "w_pool", "pool_scale", "ssm_a_re", "ssm_a_im", "ssm_log_step",
                "ssm_b_re", "ssm_b_im", "ssm_c_re", "ssm_c_im", "ssm_d", "w_glu", "b_glu", "q_a_norm", "w_uq",
                "kv_a_norm", "w_ukv", "q_norm", "k_norm", "out_norm", "w_out", "w_router", "b_router")


def kernel(x, c, ctx, c_ctx, w_ada, b_ada, norm_mix, norm_ffn, w_in, w_pool, pool_scale, ssm_a_re, ssm_a_im, ssm_log_step, ssm_b_re, ssm_b_im, ssm_c_re, ssm_c_im, ssm_d, w_glu, b_glu, q_a_norm, w_uq, kv_a_norm, w_ukv, q_norm, k_norm, out_norm, w_out, w_router, b_router, w_gate, b_gate, w_up, b_up, w_down, b_down):
    n_batch, seq, d = x.shape
    ctx_len = ctx.shape[1]
    depth = w_ada.shape[0]
    assert seq % ROW_TILE == 0 and ctx_len % ROW_TILE == 0 and seq % GRID_W == 0 and n_batch + 1 <= 8
    stacked = dict(zip(_PARAM_NAMES, (norm_mix, norm_ffn, w_in, w_pool, pool_scale, ssm_a_re, ssm_a_im,
                                      ssm_log_step, ssm_b_re, ssm_b_im, ssm_c_re, ssm_c_im, ssm_d, w_glu, b_glu,
                                      q_a_norm, w_uq, kv_a_norm, w_ukv, q_norm, k_norm, out_norm, w_out, w_router,
                                      b_router)))
    expert_params = (w_gate, b_gate, w_up, b_up, w_down, b_down)
    cvecs = jnp.concatenate([c, c_ctx[None, :], jnp.zeros((8 - n_batch - 1, d), F32)], axis=0)
    mods = _ada(cvecs, w_ada, b_ada)
    cos_t, sin_t = _rope_tables(n_batch, seq, ctx_len)
    n_lat = n_batch * seq
    xt = jnp.concatenate([x.reshape(n_lat, d), ctx.reshape(n_batch * ctx_len, d)], axis=0)
    for layer in range(depth):
        p = {name: val[layer] for name, val in stacked.items()}
        xt = _layer(layer, xt, mods[layer], cos_t, sin_t, p, expert_params, n_batch=n_batch, seq=seq,
                    ctx_len=ctx_len, ctx_out=layer < depth - 1)
    return xt[:n_lat].reshape(n_batch, seq, d)
```

```python
import functools
import math

import jax
import jax.numpy as jnp
from jax import lax
from jax.experimental import pallas as pl
from jax.experimental.pallas import tpu as pltpu

F32 = jnp.float32
BF16 = jnp.bfloat16

EPS = 1e-6
GRID_W = 64
POOL_WINDOWS = (2, 4, 8, 16)
POOL_HALO = 8
SSM_GROUP = 16
SSM_STATE = 64
S5_CHUNK = 16
S5_TILE_STATES = (128 // SSM_GROUP) * SSM_STATE
QK_NOPE = 128
QK_ROPE = 64
QK_HEAD = QK_NOPE + QK_ROPE
QK_PAD = 256
V_HEAD = 128
ROPE_BASE = 10000.0
TOP_K = 4
SWIGLU_ALPHA = 1.702
SWIGLU_LIMIT = 7.0

LANES = 128
ROW_TILE = 256
ATTN_Q_TILE = 1024
ATTN_KEY_CHUNK = 1024
EXPERT_ITEM_ROWS = 1280
EXPERT_SUB_ROWS = 256
EXPERT_ROW_VARIANTS = (1, 2, 3, 4, 5)
EXPERT_TILE = 512
COMBINE_TILE = 256
VMEM_LIMIT = 52 * 1024 * 1024
EXPERT_VMEM_LIMIT = 57 * 1024 * 1024
NEG_BIG = -1e30


def _cparams(sem, vmem=VMEM_LIMIT):
    return pltpu.CompilerParams(dimension_semantics=sem, vmem_limit_bytes=vmem)


def _rms(x, w):
    return x * lax.rsqrt(jnp.mean(x * x, axis=-1, keepdims=True) + EPS) * w


def _full(shape):
    nd = len(shape)
    return pl.BlockSpec(shape, lambda *_: (0,) * nd)


def _ada_kernel(cv_ref, w_ref, b_ref, o_ref):
    cv = cv_ref[...]
    s = (cv * jax.nn.sigmoid(cv)).astype(BF16)
    o_ref[0] = jnp.dot(s, w_ref[0].astype(BF16), preferred_element_type=F32) + b_ref[0]


def _ada(cvecs, w_ada, b_ada):
    n_layers, d, n6 = w_ada.shape
    tn = 512
    return pl.pallas_call(
        _ada_kernel,
        grid=(n_layers, n6 // tn),
        in_specs=[
            pl.BlockSpec((8, d), lambda l, j: (0, 0)),
            pl.BlockSpec((1, d, tn), lambda l, j: (l, 0, j)),
            pl.BlockSpec((1, 1, tn), lambda l, j: (l, 0, j)),
        ],
        out_specs=pl.BlockSpec((1, 8, tn), lambda l, j: (l, 0, j)),
        out_shape=jax.ShapeDtypeStruct((n_layers, 8, n6), F32),
        compiler_params=_cparams(("arbitrary", "arbitrary")),
        name="ada",
    )(cvecs, w_ada, b_ada.reshape(n_layers, 1, n6))


def _rope128(r, cos, sin):
    lane = lax.broadcasted_iota(jnp.int32, r.shape, 1)
    sw = jnp.where((lane % QK_ROPE) < QK_ROPE // 2, pltpu.roll(r, LANES - 32, 1), pltpu.roll(r, 32, 1))
    return r * cos + sw * sin


def _proj_kernel(x_ref, mod_ref, cos_ref, sin_ref, nmix_ref, win_ref, qan_ref, wuq_ref, kvan_ref,
                 wk_ref, wv_ref, qn_ref, kn_ref,
                 upool_ref, ussm_ref, q_ref, k_ref, v_ref, *, tiles_per_seq, n_batch, n_heads, d_model):
    d = d_model
    r = jnp.minimum(pl.program_id(0) // tiles_per_seq, n_batch)
    sh = mod_ref[pl.ds(r, 1), pl.ds(0, d)]
    sc = mod_ref[pl.ds(r, 1), pl.ds(d, d)]
    h = _rms(x_ref[...], nmix_ref[...]) * (1.0 + sc) + sh
    proj = jnp.dot(h.astype(BF16), win_ref[...], preferred_element_type=F32)
    d4 = d // 4
    upool_ref[...] = proj[:, 0:d4]
    ussm_ref[...] = proj[:, 2 * d4:3 * d4]
    cos = cos_ref[...]
    sin = sin_ref[...]

    qa = _rms(proj[:, d4:2 * d4], qan_ref[...]).astype(BF16)
    qf = jnp.dot(qa, wuq_ref[...], preferred_element_type=F32)
    qn_w = qn_ref[...]
    for hd in range(n_heads):
        blk = qf[:, hd * QK_PAD:(hd + 1) * QK_PAD]
        ss = jnp.sum(blk * blk, axis=-1, keepdims=True) * (1.0 / QK_HEAD)
        qn = blk * lax.rsqrt(ss + EPS) * qn_w
        q_ref[:, hd * QK_PAD:hd * QK_PAD + LANES] = qn[:, :LANES].astype(BF16)
        q_ref[:, hd * QK_PAD + LANES:(hd + 1) * QK_PAD] = _rope128(qn[:, LANES:], cos, sin).astype(BF16)

    kv_lo = 3 * d4
    kv_w = d // 8
    ka = _rms(proj[:, kv_lo:kv_lo + kv_w], kvan_ref[...]).astype(BF16)
    kn_all = jnp.dot(ka, wk_ref[...], preferred_element_type=F32)
    v_ref[...] = jnp.dot(ka, wv_ref[...], preferred_element_type=F32).astype(BF16)
    krp = proj[:, kv_lo + kv_w:kv_lo + kv_w + LANES]
    kr_ss = jnp.sum(krp * krp, axis=-1, keepdims=True)
    kn_w = kn_ref[...]
    for hd in range(n_heads):
        kn = kn_all[:, hd * QK_NOPE:(hd + 1) * QK_NOPE]
        ss = (jnp.sum(kn * kn, axis=-1, keepdims=True) + kr_ss) * (1.0 / QK_HEAD)
        rinv = lax.rsqrt(ss + EPS)
        k_ref[:, hd * QK_PAD:hd * QK_PAD + LANES] = (kn * rinv * kn_w[:, :LANES]).astype(BF16)
        k_ref[:, hd * QK_PAD + LANES:(hd + 1) * QK_PAD] = _rope128(
            krp * rinv * kn_w[:, LANES:], cos, sin).astype(BF16)


def _proj(xt, mods, cos_t, sin_t, nmix, win, qan, wuq, kvan, wk, wv, qn, kn, *, n_batch, seq):
    t, d = xt.shape
    tm = ROW_TILE
    n_heads = wv.shape[1] // V_HEAD
    row = lambda w: pl.BlockSpec((tm, w), lambda i: (i, 0))
    kern = functools.partial(_proj_kernel, tiles_per_seq=seq // tm, n_batch=n_batch,
                             n_heads=n_heads, d_model=d)
    return pl.pallas_call(
        kern,
        grid=(t // tm,),
        in_specs=[row(d), _full(mods.shape), row(LANES), row(LANES), _full(nmix.shape), _full(win.shape),
                  _full(qan.shape), _full(wuq.shape), _full(kvan.shape), _full(wk.shape), _full(wv.shape),
                  _full(qn.shape), _full(kn.shape)],
        out_specs=[row(d // 4), row(d // 4), row(n_heads * QK_PAD), row(n_heads * QK_PAD),
                   row(n_heads * V_HEAD)],
        out_shape=[jax.ShapeDtypeStruct((t, d // 4), F32), jax.ShapeDtypeStruct((t, d // 4), F32),
                   jax.ShapeDtypeStruct((t, n_heads * QK_PAD), BF16),
                   jax.ShapeDtypeStruct((t, n_heads * QK_PAD), BF16),
                   jax.ShapeDtypeStruct((t, n_heads * V_HEAD), BF16)],
        compiler_params=_cparams(("arbitrary",)),
        name="mixer_in",
    )(xt, mods, cos_t, sin_t, nmix, win, qan, wuq, kvan, wk, wv, qn, kn)


def _pool_kernel(prev_ref, cur_ref, next_ref, w_ref, scale_ref, o_ref, pad_ref, *,
                 tm, n_lat_tiles, seq, ctx_len):
    i = pl.program_id(0)
    hw = POOL_HALO
    pad_ref[0:hw, :] = prev_ref[...]
    pad_ref[hw:hw + tm, :] = cur_ref[...]
    pad_ref[hw + tm:2 * hw + tm, :] = next_ref[...]
    is_lat = i < n_lat_tiles
    row0 = i * tm
    s0 = jnp.where(is_lat, (row0 // seq) * seq,
                   n_lat_tiles * tm + ((row0 - n_lat_tiles * tm) // ctx_len) * ctx_len)
    slen = jnp.where(is_lat, seq, ctx_len)
    t = row0 - s0 + lax.broadcasted_iota(jnp.int32, (tm, 1), 0)
    for g, w in enumerate(POOL_WINDOWS):
        lanes = slice(g * LANES, (g + 1) * LANES)
        acc = jnp.zeros((tm, LANES), F32)
        for kk in range(-(w // 2), w - w // 2):
            valid = (t + kk >= 0) & (t + kk < slen)
            acc = acc + jnp.where(valid, pad_ref[hw + kk:hw + kk + tm, lanes], 0.0)
        lo = jnp.maximum(t - w // 2, 0)
        hi = jnp.minimum(t + (w - w // 2), slen)
        mean = acc / (hi - lo).astype(F32)
        dlt = (mean - cur_ref[:, lanes]).astype(BF16)
        o_ref[:, lanes] = jnp.dot(dlt, w_ref[g], preferred_element_type=F32) * scale_ref[:, lanes]


def _pool(u_pool, w_pool, pool_scale, *, n_batch, seq, ctx_len):
    t, dp = u_pool.shape
    tm = ROW_TILE
    hb = tm // POOL_HALO
    n_hblocks = t // POOL_HALO
    kern = functools.partial(_pool_kernel, tm=tm, n_lat_tiles=n_batch * seq // tm, seq=seq, ctx_len=ctx_len)
    return pl.pallas_call(
        kern,
        grid=(t // tm,),
        in_specs=[
            pl.BlockSpec((POOL_HALO, dp), lambda i: (jnp.maximum(i * hb - 1, 0), 0)),
            pl.BlockSpec((tm, dp), lambda i: (i, 0)),
            pl.BlockSpec((POOL_HALO, dp), lambda i: (jnp.minimum((i + 1) * hb, n_hblocks - 1), 0)),
            _full(w_pool.shape), _full(pool_scale.shape),
        ],
        out_specs=pl.BlockSpec((tm, dp), lambda i: (i, 0)),
        out_shape=jax.ShapeDtypeStruct((t, dp), F32),
        scratch_shapes=[pltpu.VMEM((tm + 2 * POOL_HALO, dp), F32)],
        compiler_params=_cparams(("arbitrary",)),
        name="pool",
    )(u_pool, u_pool, u_pool, w_pool, pool_scale)


def _s5_prep_kernel(are_ref, aim_ref, ls_ref, bbr_ref, bbi_ref, cbr_ref, cbi_ref,
                    ks_ref, bb_ref, cb_ref, we_ref, wc_ref, lt_ref):
    tc = S5_CHUNK
    fwd = pl.program_id(0) == 0
    step = jnp.exp(ls_ref[0, 0])
    ar = are_ref[0, 0]
    ai = aim_ref[0, 0]
    la = ar * step
    th = ai * step

    def powtab(m):
        mg = jnp.exp(m * la)
        an = m * th
        return mg * jnp.cos(an), mg * jnp.sin(an)

    lam_r, lam_i = powtab(1.0)
    den = ar * ar + ai * ai
    nr = lam_r - 1.0
    coef_r = (nr * ar + lam_i * ai) / den
    coef_i = (lam_i * ar - nr * ai) / den
    bp_r, bp_i = _cmul(coef_r, coef_i, bbr_ref[0, 0], bbi_ref[0, 0])
    bb_ref[0, 0] = jnp.concatenate([bp_r, bp_i], axis=-1).astype(BF16)
    cbr = cbr_ref[0, 0]
    cbi = cbi_ref[0, 0]
    cb_ref[0, 0] = jnp.concatenate([cbr, -cbi], axis=0).astype(BF16)
    jcol = lax.broadcasted_iota(jnp.int32, (tc, 1), 0).astype(F32)
    lag_r, lag_i = powtab(jcol)
    cbr_b = cbr.astype(BF16)
    cbi_b = cbi.astype(BF16)
    for m in range(tc):
        bm_r, bm_i = _cmul(bp_r, bp_i, lag_r[m:m + 1, :], lag_i[m:m + 1, :])
        k_m = (jnp.dot(bm_r.astype(BF16), cbr_b, preferred_element_type=F32)
               - jnp.dot(bm_i.astype(BF16), cbi_b, preferred_element_type=F32))
        ks_ref[0, 0, m * LANES:(m + 1) * LANES, :] = k_m.astype(BF16)
    e_r, e_i = powtab(jnp.where(fwd, (tc - 1.0) - jcol, jcol))
    c_r, c_i = powtab(jnp.where(fwd, jcol + 1.0, tc - jcol))
    we_ref[0, 0] = jnp.concatenate([e_r, e_i], axis=-1)
    wc_ref[0, 0] = jnp.concatenate([c_r, c_i], axis=-1)
    t_r, t_i = powtab(float(tc))
    lt_ref[0, 0] = jnp.concatenate([t_r, t_i], axis=-1)


def _s5_prep(a_re, a_im, log_step, b_re, b_im, c_re, c_im):
    _, g, p = a_re.shape
    c = b_re.shape[-1]
    gq = LANES // c
    nq = g // gq
    sq = gq * p
    tc = S5_CHUNK
    eye = jnp.eye(gq, dtype=F32)
    lanes_of = lambda a: a.reshape(2, nq, 1, sq)
    ls = jnp.broadcast_to(log_step[:, :, None], (2, g, p))
    bbd = lambda b: jnp.einsum("dqgpc,gh->dqgchp", b.reshape(2, nq, gq, p, c), eye).reshape(2, nq, gq * c, sq)
    cbd = lambda m: jnp.einsum("dqgcp,gh->dqgphc", m.reshape(2, nq, gq, c, p), eye).reshape(2, nq, sq, gq * c)
    blk = lambda *s: pl.BlockSpec((1, 1) + s, lambda d, q: (d, q, 0, 0))
    shp = lambda s, dt: jax.ShapeDtypeStruct((2, nq) + s, dt)
    return pl.pallas_call(
        _s5_prep_kernel,
        grid=(2, nq),
        in_specs=[blk(1, sq), blk(1, sq), blk(1, sq), blk(LANES, sq), blk(LANES, sq), blk(sq, LANES),
                  blk(sq, LANES)],
        out_specs=[blk(tc * LANES, LANES), blk(LANES, 2 * sq), blk(2 * sq, LANES), blk(tc, 2 * sq),
                   blk(tc, 2 * sq), blk(1, 2 * sq)],
        out_shape=[shp((tc * LANES, LANES), BF16), shp((LANES, 2 * sq), BF16), shp((2 * sq, LANES), BF16),
                   shp((tc, 2 * sq), F32), shp((tc, 2 * sq), F32), shp((1, 2 * sq), F32)],
        compiler_params=_cparams(("arbitrary", "arbitrary")),
        name="s5_prep",
    )(lanes_of(a_re), lanes_of(a_im), lanes_of(ls), bbd(b_re), bbd(b_im), cbd(c_re), cbd(c_im))


def _cmul(ar, ai, br, bi):
    return ar * br - ai * bi, ar * bi + ai * br


def _s5_kernel(*refs, tm, rev, has_prev):
    if has_prev:
        u_ref, prev_ref, ks_ref, bb_ref, cb_ref, we_ref, wc_ref, lt_ref, y_ref, pad_ref, st_ref = refs
    else:
        u_ref, ks_ref, bb_ref, cb_ref, we_ref, wc_ref, lt_ref, y_ref, pad_ref, st_ref = refs
    tc = S5_CHUNK
    nck = tm // tc
    nq = u_ref.shape[1] // LANES
    sq = S5_TILE_STATES

    @pl.when(pl.program_id(1) == 0)
    def _():
        st_ref[...] = jnp.zeros_like(st_ref)

    pad_ref[0:tc, :] = jnp.zeros((tc, LANES), F32)
    pad_ref[tc + tm:, :] = jnp.zeros((tc, LANES), F32)
    jpos = lax.broadcasted_iota(jnp.int32, (tm, 1), 0) % tc
    order = range(nck - 1, -1, -1) if rev else range(nck)
    for q in range(nq):
        lanes = slice(q * LANES, (q + 1) * LANES)
        u = u_ref[:, lanes]
        pad_ref[tc:tc + tm, :] = u
        cols = []
        for m in range(tc):
            if rev:
                sh = pad_ref[tc + m:tc + m + tm, :]
                keep = jpos + m < tc
            else:
                sh = pad_ref[tc - m:tc - m + tm, :]
                keep = jpos >= m
            cols.append(jnp.where(keep, sh, 0.0).astype(BF16))
        y = jnp.dot(jnp.concatenate(cols, axis=-1), ks_ref[0, q], preferred_element_type=F32)

        x = jnp.dot(u.astype(BF16), bb_ref[0, q], preferred_element_type=F32)
        we = jnp.tile(we_ref[0, q], (nck, 1))
        er, ei = _cmul(x[:, :sq], x[:, sq:], we[:, :sq], we[:, sq:])
        er = jnp.sum(er.reshape(nck, tc, sq), axis=1)
        ei = jnp.sum(ei.reshape(nck, tc, sq), axis=1)
        lam_r = lt_ref[0, q][:, :sq]
        lam_i = lt_ref[0, q][:, sq:]
        s_r = st_ref[2 * q:2 * q + 1, :]
        s_i = st_ref[2 * q + 1:2 * q + 2, :]
        ent_r, ent_i = [None] * nck, [None] * nck
        for k in order:
            ent_r[k] = jnp.broadcast_to(s_r, (tc, sq))
            ent_i[k] = jnp.broadcast_to(s_i, (tc, sq))
            n_r, n_i = _cmul(lam_r, lam_i, s_r, s_i)
            s_r = n_r + er[k:k + 1, :]
            s_i = n_i + ei[k:k + 1, :]
        st_ref[2 * q:2 * q + 1, :] = s_r
        st_ref[2 * q + 1:2 * q + 2, :] = s_i
        wc = jnp.tile(wc_ref[0, q], (nck, 1))
        z_r, z_i = _cmul(wc[:, :sq], wc[:, sq:], jnp.concatenate(ent_r, axis=0), jnp.concatenate(ent_i, axis=0))
        y = y + jnp.dot(jnp.concatenate([z_r, z_i], axis=-1).astype(BF16), cb_ref[0, q],
                        preferred_element_type=F32)
        y_ref[:, lanes] = (prev_ref[:, lanes] + y) if has_prev else y


def _s5(u_ssm, prep, *, n_batch, seq, ctx_len):
    t, dw = u_ssm.shape
    tm = ROW_TILE
    nq = dw // LANES
    tc = S5_CHUNK
    sq2 = 2 * S5_TILE_STATES
    nct, nlt = ctx_len // tm, seq // tm
    lat_tiles = n_batch * nlt
    ks, bb, cb, we, wc, lt = prep

    def tile_of(rev):
        def f(b, s):
            cpos = (nct - 1 - s) if rev else s
            lpos = (nlt - 1 - (s - nct)) if rev else (s - nct)
            return jnp.where(s < nct, lat_tiles + b * nct + cpos, b * nlt + lpos)
        return f

    y = None
    for d, rev in enumerate((False, True)):
        tile = tile_of(rev)
        row = pl.BlockSpec((tm, dw), lambda b, s: (tile(b, s), 0))
        par = lambda *shape: pl.BlockSpec((1, nq) + shape, lambda b, s: (d, 0, 0, 0))
        has_prev = y is not None
        kern = functools.partial(_s5_kernel, tm=tm, rev=rev, has_prev=has_prev)
        y = pl.pallas_call(
            kern,
            grid=(n_batch, nct + nlt),
            in_specs=[row] * (2 if has_prev else 1) + [par(tc * LANES, LANES), par(LANES, sq2), par(sq2, LANES),
                                                      par(tc, sq2), par(tc, sq2), par(1, sq2)],
            out_specs=row,
            out_shape=jax.ShapeDtypeStruct((t, dw), F32),
            scratch_shapes=[pltpu.VMEM((tm + 2 * tc, LANES), F32), pltpu.VMEM((2 * nq, S5_TILE_STATES), F32)],
            compiler_params=_cparams(("arbitrary", "arbitrary")),
            name="s5_scan",
        )(*([u_ssm] + ([y] if has_prev else []) + [ks, bb, cb, we, wc, lt]))
    return y


_NT = (((1,), (1,)), ((), ()))


def _attn_lat_kernel(q_ref, kl_ref, vl_ref, kc_ref, vc_ref, o_ref):
    q = q_ref[...]
    tq = q.shape[0]
    seq = kl_ref.shape[0]
    kc = min(ATTN_KEY_CHUNK, seq)
    chunks = [(kl_ref, vl_ref, c * kc, kc) for c in range(seq // kc)] + [(kc_ref, vc_ref, 0, kc_ref.shape[0])]
    m = jnp.full((tq, 1), NEG_BIG, F32)
    l = jnp.zeros((tq, 1), F32)
    acc = jnp.zeros((tq, V_HEAD), F32)
    for k_ref, v_ref, off, n in chunks:
        s = lax.dot_general(q, k_ref[off:off + n, :], _NT, preferred_element_type=F32)
        m_new = jnp.maximum(m, jnp.max(s, axis=-1, keepdims=True))
        alpha = jnp.exp(m - m_new)
        p = jnp.exp(s - m_new)
        l = alpha * l + jnp.sum(p, axis=-1, keepdims=True)
        acc = alpha * acc + jnp.dot(p.astype(BF16), v_ref[off:off + n, :], preferred_element_type=F32)
        m = m_new
    o_ref[...] = acc / l


def _attn_ctx_kernel(q_ref, kc_ref, vc_ref, o_ref):
    s = lax.dot_general(q_ref[...], kc_ref[...], _NT, preferred_element_type=F32)
    p = jnp.exp(s - jnp.max(s, axis=-1, keepdims=True))
    l = jnp.sum(p, axis=-1, keepdims=True)
    o_ref[...] = jnp.dot(p.astype(BF16), vc_ref[...], preferred_element_type=F32) / l


def _attention(q, k, v, *, n_batch, seq, ctx_len, ctx_queries):
    n_heads = v.shape[1] // V_HEAD
    tq = min(ATTN_Q_TILE, seq)
    nq = seq // tq
    cblk = n_batch * seq // ctx_len
    y_lat = pl.pallas_call(
        _attn_lat_kernel,
        grid=(n_batch, n_heads, nq),
        in_specs=[
            pl.BlockSpec((tq, QK_PAD), lambda b, h, i: (b * nq + i, h)),
            pl.BlockSpec((seq, QK_PAD), lambda b, h, i: (b, h)),
            pl.BlockSpec((seq, V_HEAD), lambda b, h, i: (b, h)),
            pl.BlockSpec((ctx_len, QK_PAD), lambda b, h, i: (cblk + b, h)),
            pl.BlockSpec((ctx_len, V_HEAD), lambda b, h, i: (cblk + b, h)),
        ],
        out_specs=pl.BlockSpec((tq, V_HEAD), lambda b, h, i: (b * nq + i, h)),
        out_shape=jax.ShapeDtypeStruct((n_batch * seq, n_heads * V_HEAD), F32),
        compiler_params=_cparams(("arbitrary", "arbitrary", "arbitrary")),
        name="attn_latent",
    )(q, k, v, k, v)
    if not ctx_queries:
        return y_lat, None
    y_ctx = pl.pallas_call(
        _attn_ctx_kernel,
        grid=(n_batch, n_heads),
        in_specs=[
            pl.BlockSpec((ctx_len, QK_PAD), lambda b, h: (cblk + b, h)),
            pl.BlockSpec((ctx_len, QK_PAD), lambda b, h: (cblk + b, h)),
            pl.BlockSpec((ctx_len, V_HEAD), lambda b, h: (cblk + b, h)),
        ],
        out_specs=pl.BlockSpec((ctx_len, V_HEAD), lambda b, h: (b, h)),
        out_shape=jax.ShapeDtypeStruct((n_batch * ctx_len, n_heads * V_HEAD), F32),
        compiler_params=_cparams(("arbitrary", "arbitrary")),
        name="attn_context",
    )(q, k, v)
    return y_lat, y_ctx


def _merge_kernel(x_ref, yp_ref, ys_ref, us_ref, yml_ref, *rest, tiles_per_seq, n_batch, d_model, has_ctx):
    if has_ctx:
        ymc_ref, rest = rest[0], rest[1:]
    (mod_ref, sd_ref, wglu_ref, bglu_ref, on_ref, wout_ref, nffn_ref, wr_ref, br_ref,
     x1_ref, h2_ref, idx_ref, gate_ref, rank_ref, cnt_ref, carry_ref) = rest
    d = d_model
    d4 = d // 4
    i = pl.program_id(0)
    tm = x_ref.shape[0]
    r = jnp.minimum(i // tiles_per_seq, n_batch)
    g1 = mod_ref[pl.ds(r, 1), pl.ds(2 * d, d)]
    sh2 = mod_ref[pl.ds(r, 1), pl.ds(3 * d, d)]
    sc2 = mod_ref[pl.ds(r, 1), pl.ds(4 * d, d)]

    z = ys_ref[...] + sd_ref[...] * us_ref[...]
    g = jax.nn.gelu(z, approximate=True)
    glu = jnp.dot(g.astype(BF16), wglu_ref[...], preferred_element_type=F32) + bglu_ref[...]
    y_ssm = g * jax.nn.sigmoid(glu)
    on = on_ref[...]
    y_mla = yml_ref[...]
    if has_ctx:
        y_mla = jnp.where(i < n_batch * tiles_per_seq, y_mla, ymc_ref[...])
    cat = jnp.concatenate([_rms(yp_ref[...], on[:, 0:d4]), _rms(y_ssm, on[:, d4:2 * d4]),
                           _rms(y_mla, on[:, 2 * d4:])], axis=-1).astype(BF16)
    x1 = x_ref[...] + g1 * jnp.dot(cat, wout_ref[...], preferred_element_type=F32)
    x1_ref[...] = x1
    h2 = (_rms(x1, nffn_ref[...]) * (1.0 + sc2) + sh2).astype(BF16)
    h2_bits = lax.bitcast_convert_type(h2.astype(F32), jnp.uint32)
    h2_ref[...] = (h2_bits[:, :d // 2] >> 16) | (h2_bits[:, d // 2:] & jnp.uint32(0xFFFF0000))

    logits = jnp.dot(h2, wr_ref[...], preferred_element_type=F32) + br_ref[...]
    lane = lax.broadcasted_iota(jnp.int32, (tm, LANES), 1)
    vals, hots = [], []
    idx_out = jnp.zeros((tm, LANES), jnp.int32)
    for kk in range(TOP_K):
        mx = jnp.max(logits, axis=-1, keepdims=True)
        sel = jnp.min(jnp.where(logits == mx, lane, LANES), axis=-1, keepdims=True)
        hot = lane == sel
        vals.append(mx)
        hots.append(hot)
        idx_out = jnp.where(lane == kk, sel, idx_out)
        logits = jnp.where(hot, NEG_BIG * 2.0, logits)
    es = [jnp.exp(vv - vals[0]) for vv in vals]
    den = es[0] + es[1] + es[2] + es[3]
    gate_out = jnp.zeros((tm, LANES), F32)
    for kk in range(TOP_K):
        gate_out = jnp.where(lane == kk, es[kk] / den, gate_out)
    idx_ref[...] = idx_out
    gate_ref[...] = gate_out

    @pl.when(i == 0)
    def _():
        carry_ref[...] = jnp.zeros_like(carry_ref)

    onehot = (hots[0] | hots[1] | hots[2] | hots[3]).astype(BF16)
    rr = lax.broadcasted_iota(jnp.int32, (tm, tm), 0)
    cc = lax.broadcasted_iota(jnp.int32, (tm, tm), 1)
    tri = (cc < rr).astype(BF16)
    pos = jnp.dot(tri, onehot, preferred_element_type=F32) + carry_ref[...]
    rank_out = jnp.zeros((tm, LANES), jnp.int32)
    for kk in range(TOP_K):
        rk = jnp.sum(jnp.where(hots[kk], pos, 0.0), axis=-1, keepdims=True).astype(jnp.int32)
        rank_out = jnp.where(lane == kk, rk, rank_out)
    rank_ref[...] = rank_out
    carry = carry_ref[...] + jnp.sum(onehot.astype(F32), axis=0, keepdims=True)
    carry_ref[...] = carry
    cnt_ref[...] = carry.astype(jnp.int32)


def _merge(xt, y_pool, y_s5, u_ssm, y_mla_lat, y_mla_ctx, mods, ssm_d, wglu, bglu, out_norm, wout, nffn, wr, br, *,
           n_rows, n_batch, seq):
    d = xt.shape[1]
    tm = ROW_TILE
    n_lat_tiles = n_batch * seq // tm
    has_ctx = y_mla_ctx is not None
    row = lambda w: pl.BlockSpec((tm, w), lambda i: (i, 0))
    kern = functools.partial(_merge_kernel, tiles_per_seq=seq // tm, n_batch=n_batch, d_model=d, has_ctx=has_ctx)
    small = [mods, ssm_d, wglu, bglu, out_norm, wout, nffn, wr, br]
    y_mla = [y_mla_lat] + ([y_mla_ctx] if has_ctx else [])
    y_specs = [pl.BlockSpec((tm, d // 2), lambda i: (jnp.minimum(i, n_lat_tiles - 1), 0))]
    if has_ctx:
        y_specs.append(pl.BlockSpec((tm, d // 2), lambda i: (jnp.maximum(i - n_lat_tiles, 0), 0)))
    return pl.pallas_call(
        kern,
        grid=(n_rows // tm,),
        in_specs=[row(d), row(d // 4), row(d // 4), row(d // 4)] + y_specs + [_full(a.shape) for a in small],
        out_specs=[row(d), row(d // 2), row(LANES), row(LANES), row(LANES), _full((1, LANES))],
        out_shape=[jax.ShapeDtypeStruct((n_rows, d), F32), jax.ShapeDtypeStruct((n_rows, d // 2), jnp.uint32),
                   jax.ShapeDtypeStruct((n_rows, LANES), jnp.int32), jax.ShapeDtypeStruct((n_rows, LANES), F32),
                   jax.ShapeDtypeStruct((n_rows, LANES), jnp.int32), jax.ShapeDtypeStruct((1, LANES), jnp.int32)],
        scratch_shapes=[pltpu.VMEM((1, LANES), F32)],
        compiler_params=_cparams(("arbitrary",)),
        name="merge_router",
    )(xt, y_pool, y_s5, u_ssm, *y_mla, *small)


def _row_copy(src, dst, sem, src_row, dst_row):
    return pltpu.make_async_copy(src.at[pl.ds(src_row, 1), :], dst.at[pl.ds(dst_row, 1), :], sem)


def _dispatch_kernel(dest_hbm, h2_ref, zeros_hbm, xs_hbm, dest_smem, sem_idx, sem_rows, *, tm):
    del zeros_hbm
    i = pl.program_id(0)
    n = tm * TOP_K
    idx_copy = pltpu.make_async_copy(dest_hbm.at[i], dest_smem, sem_idx)
    idx_copy.start()
    idx_copy.wait()

    def issue(r, _):
        for kk in range(TOP_K):
            _row_copy(h2_ref, xs_hbm, sem_rows, r, dest_smem[r * TOP_K + kk]).start(priority=kk % 2)
        return 0

    lax.fori_loop(0, tm, issue, 0, unroll=2)

    def drain(r, _):
        for kk in range(TOP_K):
            _row_copy(h2_ref, xs_hbm, sem_rows, r, dest_smem[r * TOP_K + kk]).wait()
        return 0

    lax.fori_loop(0, tm, drain, 0, unroll=2)


def _dispatch(dest_tiles, h2_packed, n_sorted_rows):
    n_tok, w = h2_packed.shape
    tm = ROW_TILE
    zeros = jnp.zeros((n_sorted_rows, w), h2_packed.dtype)
    return pl.pallas_call(
        functools.partial(_dispatch_kernel, tm=tm),
        grid=(n_tok // tm,),
        in_specs=[pl.BlockSpec(memory_space=pl.ANY), pl.BlockSpec((tm, w), lambda i: (i, 0)),
                  pl.BlockSpec(memory_space=pl.ANY)],
        out_specs=pl.BlockSpec(memory_space=pl.ANY),
        out_shape=jax.ShapeDtypeStruct(zeros.shape, zeros.dtype),
        input_output_aliases={2: 0},
        scratch_shapes=[pltpu.SMEM((tm * TOP_K,), jnp.int32), pltpu.SemaphoreType.DMA, pltpu.SemaphoreType.DMA],
        compiler_params=_cparams(("arbitrary",)),
        name="moe_dispatch",
    )(dest_tiles, h2_packed, zeros)


def _expert_kernel(ie_ref, ix_ref, io_ref, ns_ref, nz_ref, x_ref, wg_ref, bg_ref, wu_ref, bu_ref, wd_ref, bd_ref,
                   o_hbm, act_ref, xb_ref, res_ref, zero_ref, sem_out, sem_zero, *, nf):
    del ie_ref, ix_ref
    it = pl.program_id(0)
    j = pl.program_id(1)
    ns = ns_ref[it]
    rows_max = x_ref.shape[0]
    half = x_ref.shape[1]
    tw = res_ref.shape[2]
    sub = EXPERT_SUB_ROWS
    o_row = pl.multiple_of(io_ref[it] * sub, sub)
    n = j - nf

    def out_copy(step, mr):
        col = pl.multiple_of(step * tw, tw)
        return pltpu.make_async_copy(res_ref.at[step % 2, 0:mr, :], o_hbm.at[pl.ds(o_row, mr), pl.ds(col, tw)],
                                     sem_out.at[step % 2])

    @pl.when((ns > 0) & (j == 0))
    def _():
        words = x_ref[...]
        xb_ref[:, 0:half] = lax.bitcast_convert_type(words << 16, F32).astype(BF16)
        xb_ref[:, half:] = lax.bitcast_convert_type(words & jnp.uint32(0xFFFF0000), F32).astype(BF16)

    for m in EXPERT_ROW_VARIANTS:
        mr = m * EXPERT_SUB_ROWS

        @pl.when((ns == m) & (j < nf))
        def _():
            x = xb_ref[0:mr, :]
            g = jnp.dot(x, wg_ref[0, 0].astype(BF16), preferred_element_type=F32) + bg_ref[0, 0]
            u = jnp.dot(x, wu_ref[0, 0].astype(BF16), preferred_element_type=F32) + bu_ref[0, 0]
            g = jnp.minimum(g, SWIGLU_LIMIT)
            u = jnp.clip(u, -SWIGLU_LIMIT, SWIGLU_LIMIT)
            act_ref[j, 0:mr, :] = (g * jax.nn.sigmoid(SWIGLU_ALPHA * g) * (u + 1.0)).astype(BF16)

        @pl.when((ns == m) & (j >= nf))
        def _():
            @pl.when(n >= 2)
            def _():
                out_copy(n - 2, mr).wait()

            act = jnp.concatenate([act_ref[f, 0:mr, :] for f in range(nf)], axis=-1)
            res = jnp.dot(act, wd_ref[0, 0].astype(BF16), preferred_element_type=F32) + bd_ref[0, 0]
            bits = lax.bitcast_convert_type(res.astype(BF16).astype(F32), jnp.uint32)
            res_ref[n % 2, 0:mr, :] = (bits[:, :tw] >> 16) | (bits[:, tw:] & jnp.uint32(0xFFFF0000))
            out_copy(n, mr).start()

            @pl.when(n == nf - 1)
            def _():
                out_copy(n - 1, mr).wait()
                out_copy(n, mr).wait()

    nz = nz_ref[it]

    @pl.when((nz > 0) & (j == 2 * nf - 1))
    def _():
        zero_ref[...] = jnp.zeros(zero_ref.shape, jnp.uint32)

        def zero_copy(k):
            rows0 = pl.multiple_of(o_row + k * sub, sub)
            return pltpu.make_async_copy(zero_ref, o_hbm.at[pl.ds(rows0, sub), :], sem_zero)

        def start(k, _):
            zero_copy(k).start()
            return 0

        def wait(k, _):
            zero_copy(k).wait()
            return 0

        lax.fori_loop(0, nz, start, 0)
        lax.fori_loop(0, nz, wait, 0)


def _experts(layer, items, xs, w_gate, b_gate, w_up, b_up, w_down, b_down):
    item_e, item_x, item_o, item_ns, item_nz = items
    n_items = item_e.shape[0]
    n_layers, n_exp, d, f_dim = w_gate.shape
    tf = EXPERT_TILE
    nf = f_dim // tf
    rows = EXPERT_ITEM_ROWS
    assert d // tf == nf and nf >= 2 and max(EXPERT_ROW_VARIANTS) * EXPERT_SUB_ROWS == rows

    def fi(j, ns, it):
        return jnp.where(ns[it] > 0, jnp.minimum(j, nf - 1), nf - 1)

    def wd_late(j, ns, it):
        return (ns[it] > 0) & (j >= nf - 1)

    def wd_e(j, ie, ns, it):
        return jnp.where(wd_late(j, ns, it), ie[it], ie[jnp.maximum(it - 1, 0)])

    def ni(j, ns, it):
        return jnp.where(wd_late(j, ns, it), jnp.maximum(j - nf, 0), nf - 1)

    def x_row(j, ix, it):
        return ix[jnp.where(j == 0, it, jnp.minimum(it + 1, n_items - 1))] * EXPERT_SUB_ROWS

    return pl.pallas_call(
        functools.partial(_expert_kernel, nf=nf),
        grid_spec=pltpu.PrefetchScalarGridSpec(
            num_scalar_prefetch=5,
            grid=(n_items, 2 * nf),
            in_specs=[
                pl.BlockSpec((pl.Element(rows), pl.Element(d // 2)),
                             lambda it, j, ie, ix, io, ns, wk: (x_row(j, ix, it), 0)),
                pl.BlockSpec((1, 1, d, tf), lambda it, j, ie, ix, io, ns, wk: (layer, ie[it], 0, fi(j, ns, it))),
                pl.BlockSpec((1, 1, 1, tf), lambda it, j, ie, ix, io, ns, wk: (layer, ie[it], 0, fi(j, ns, it))),
                pl.BlockSpec((1, 1, d, tf), lambda it, j, ie, ix, io, ns, wk: (layer, ie[it], 0, fi(j, ns, it))),
                pl.BlockSpec((1, 1, 1, tf), lambda it, j, ie, ix, io, ns, wk: (layer, ie[it], 0, fi(j, ns, it))),
                pl.BlockSpec((1, 1, f_dim, tf),
                             lambda it, j, ie, ix, io, ns, wk: (layer, wd_e(j, ie, ns, it), 0, ni(j, ns, it))),
                pl.BlockSpec((1, 1, 1, tf),
                             lambda it, j, ie, ix, io, ns, wk: (layer, wd_e(j, ie, ns, it), 0, ni(j, ns, it))),
            ],
            out_specs=pl.BlockSpec(memory_space=pl.ANY),
            scratch_shapes=[pltpu.VMEM((nf, rows, tf), BF16), pltpu.VMEM((rows, d), BF16),
                            pltpu.VMEM((2, rows, tf // 2), jnp.uint32),
                            pltpu.VMEM((EXPERT_SUB_ROWS, d // 2), jnp.uint32),
                            pltpu.SemaphoreType.DMA((2,)), pltpu.SemaphoreType.DMA],
        ),
        out_shape=jax.ShapeDtypeStruct((xs.shape[0], d // 2), jnp.uint32),
        compiler_params=_cparams(("arbitrary", "arbitrary"), EXPERT_VMEM_LIMIT),
        name="moe_experts",
    )(item_e, item_x, item_o, item_ns, item_nz, xs, w_gate, b_gate.reshape(n_layers, n_exp, 1, f_dim), w_up,
      b_up.reshape(n_layers, n_exp, 1, f_dim), w_down, b_down.reshape(n_layers, n_exp, 1, d))


def _combine_kernel(dest_hbm, os_hbm, gate_ref, x_ref, mod_ref, o_ref, dest_smem, buf, sem_idx, sem_rows,
                    *, tc, n_tiles, tiles_per_seq, n_batch):
    i = pl.program_id(0)
    n = tc * TOP_K
    slot = i % 2
    d = x_ref.shape[1]

    def gather(s, r, kk):
        return _row_copy(os_hbm, buf.at[s, kk], sem_rows.at[s], dest_smem[s, r * TOP_K + kk], r)

    def fetch(tile, s):
        idx_copy = pltpu.make_async_copy(dest_hbm.at[tile], dest_smem.at[s], sem_idx)
        idx_copy.start()
        idx_copy.wait()

        def issue(r, _):
            for kk in range(TOP_K):
                gather(s, r, kk).start(priority=kk % 2)
            return 0

        lax.fori_loop(0, tc, issue, 0, unroll=2)

    @pl.when(i == 0)
    def _():
        fetch(0, 0)

    @pl.when(i + 1 < n_tiles)
    def _():
        fetch(i + 1, 1 - slot)

    def drain(r, _):
        for kk in range(TOP_K):
            gather(slot, r, kk).wait()
        return 0

    lax.fori_loop(0, tc, drain, 0, unroll=2)
    g2 = mod_ref[pl.ds(jnp.minimum(i // tiles_per_seq, n_batch), 1), pl.ds(5 * d, d)]
    gates = gate_ref[...]
    lo = hi = None
    for kk in range(TOP_K):
        words = buf[slot, kk]
        g = gates[:, kk:kk + 1]
        lo_k = g * lax.bitcast_convert_type(words << 16, F32)
        hi_k = g * lax.bitcast_convert_type(words & jnp.uint32(0xFFFF0000), F32)
        lo = lo_k if lo is None else lo + lo_k
        hi = hi_k if hi is None else hi + hi_k
    tw = EXPERT_TILE // 2
    pieces = []
    for c in range(0, d // 2, tw):
        pieces += [lo[:, c:c + tw], hi[:, c:c + tw]]
    o_ref[...] = x_ref[...] + g2 * jnp.concatenate(pieces, axis=-1)


def _combine(dest_tiles, out_sorted, gates, x1, mods, *, n_batch, seq):
    n_tok, d = x1.shape
    tc = COMBINE_TILE
    kern = functools.partial(_combine_kernel, tc=tc, n_tiles=n_tok // tc, tiles_per_seq=seq // tc, n_batch=n_batch)
    return pl.pallas_call(
        kern,
        grid=(n_tok // tc,),
        in_specs=[pl.BlockSpec(memory_space=pl.ANY)] * 2 + [
            pl.BlockSpec((tc, LANES), lambda i: (i, 0)), pl.BlockSpec((tc, d), lambda i: (i, 0)),
            _full(mods.shape)],
        out_specs=pl.BlockSpec((tc, d), lambda i: (i, 0)),
        out_shape=jax.ShapeDtypeStruct(x1.shape, F32),
        scratch_shapes=[pltpu.SMEM((2, tc * TOP_K), jnp.int32), pltpu.VMEM((2, TOP_K, tc, d // 2), jnp.uint32),
                        pltpu.SemaphoreType.DMA, pltpu.SemaphoreType.DMA((2,))],
        compiler_params=_cparams(("arbitrary",)),
        name="moe_combine",
    )(dest_tiles, out_sorted, gates, x1, mods)


def _rope_tables(n_batch, seq, ctx_len):
    rows = seq // GRID_W
    row = jnp.broadcast_to(jnp.arange(rows)[:, None], (rows, GRID_W)).reshape(-1).astype(F32)
    col = jnp.broadcast_to(jnp.arange(GRID_W)[None, :], (rows, GRID_W)).reshape(-1).astype(F32)
    half = QK_ROPE // 2
    inv_freq = ROPE_BASE ** (-jnp.arange(0, half, 2, dtype=F32) / half)
    ang = jnp.concatenate([row[:, None] * inv_freq, col[:, None] * inv_freq], axis=-1)
    cos, sin = jnp.cos(ang), jnp.sin(ang)
    zpad = jnp.zeros((seq, LANES - QK_ROPE), F32)
    cos_l = jnp.concatenate([cos, cos, zpad], axis=-1)
    sin_l = jnp.concatenate([-sin, sin, zpad], axis=-1)
    n_ctx = n_batch * ctx_len
    cos_c = jnp.concatenate([jnp.ones((n_ctx, QK_ROPE), F32), jnp.zeros((n_ctx, LANES - QK_ROPE), F32)], axis=-1)
    cos_t = jnp.concatenate([jnp.tile(cos_l, (n_batch, 1)), cos_c], axis=0)
    sin_t = jnp.concatenate([jnp.tile(sin_l, (n_batch, 1)), jnp.zeros((n_ctx, LANES), F32)], axis=0)
    return cos_t, sin_t


def _pad_head_vec(vec):
    return jnp.concatenate([vec, jnp.zeros((QK_PAD - QK_HEAD,), vec.dtype)])[None, :]


def _routing(idx, rank, counts, n_tok, n_exp):
    rows, sub = EXPERT_ITEM_ROWS, EXPERT_SUB_ROWS
    n = n_tok * TOP_K
    cap = n + n_exp * sub + rows
    n_items = -(-n // rows) + n_exp + 1
    i32 = lambda a: a.astype(jnp.int32)
    cnt = counts[0, :n_exp]
    seg = ((cnt + sub - 1) // sub) * sub
    seg_end = jnp.cumsum(seg)
    seg_start = seg_end - seg
    dest = seg_start[idx] + rank
    used_end = seg_end[-1]
    per = (cnt + rows - 1) // rows
    it_end = jnp.cumsum(per)
    it_start = it_end - per
    total = it_end[-1]
    it = jnp.arange(n_items, dtype=jnp.int32)
    it_c = jnp.minimum(it, total - 1)
    e_of = i32(jnp.minimum(jnp.searchsorted(it_end, it_c, side="right"), n_exp - 1))
    k_in = it_c - it_start[e_of]
    x_row = seg_start[e_of] + k_in * rows
    nsub = jnp.clip((cnt[e_of] - k_in * rows + sub - 1) // sub, min(EXPERT_ROW_VARIANTS), max(EXPERT_ROW_VARIANTS))
    active = it < total
    nsub = jnp.where(active, nsub, 0)
    first_idle = it == total
    o_row = jnp.where(active, x_row, jnp.where(first_idle, used_end, 0))
    n_zero = jnp.where(first_idle, (cap - used_end) // sub, 0)
    return i32(dest), (e_of, i32(x_row // sub), i32(o_row // sub), i32(nsub), i32(n_zero)), cap


def _layer(layer, xt, mods, cos_t, sin_t, p, expert_params, *, n_batch, seq, ctx_len, ctx_out):
    t, d = xt.shape
    d4 = d // 4
    n_heads = (d - 2 * d4) // V_HEAD
    kv_lora = d // 8
    bf = lambda a: a.astype(BF16)
    row = lambda a: a[None, :]

    win = bf(jnp.pad(p["w_in"], ((0, 0), (0, LANES - QK_ROPE))))
    wuq = p["w_uq"].reshape(-1, n_heads, QK_HEAD)
    wuq = bf(jnp.pad(wuq, ((0, 0), (0, 0), (0, QK_PAD - QK_HEAD))).reshape(-1, n_heads * QK_PAD))
    wukv = p["w_ukv"].reshape(kv_lora, n_heads, QK_NOPE + V_HEAD)
    wk = bf(wukv[:, :, :QK_NOPE].reshape(kv_lora, n_heads * QK_NOPE))
    wv = bf(wukv[:, :, QK_NOPE:].reshape(kv_lora, n_heads * V_HEAD))
    qn = _pad_head_vec(p["q_norm"]) * (QK_HEAD ** -0.5)
    kn = _pad_head_vec(p["k_norm"])

    u_pool, u_ssm, q, k, v = _proj(xt, mods, cos_t, sin_t, row(p["norm_mix"]), win, row(p["q_a_norm"]), wuq,
                                   row(p["kv_a_norm"]), wk, wv, qn, kn, n_batch=n_batch, seq=seq)

    y_pool = _pool(u_pool, bf(p["w_pool"]), row(p["pool_scale"]), n_batch=n_batch, seq=seq, ctx_len=ctx_len)

    prep = _s5_prep(p["ssm_a_re"], p["ssm_a_im"], p["ssm_log_step"], p["ssm_b_re"], p["ssm_b_im"],
                    p["ssm_c_re"], p["ssm_c_im"])
    y_s5 = _s5(u_ssm, prep, n_batch=n_batch, seq=seq, ctx_len=ctx_len)

    y_mla_lat, y_mla_ctx = _attention(q, k, v, n_batch=n_batch, seq=seq, ctx_len=ctx_len, ctx_queries=ctx_out)

    n_rows = t if ctx_out else n_batch * seq
    n_exp = p["w_router"].shape[1]
    wr = bf(jnp.pad(p["w_router"], ((0, 0), (0, LANES - n_exp))))
    br = jnp.concatenate([p["b_router"], jnp.full((LANES - n_exp,), NEG_BIG, F32)])[None, :]
    x1, h2, idx, gates, rank, counts = _merge(
        xt, y_pool, y_s5, u_ssm, y_mla_lat, y_mla_ctx, mods, row(p["ssm_d"]), bf(p["w_glu"]), row(p["b_glu"]),
        row(p["out_norm"]), bf(p["w_out"]), row(p["norm_ffn"]), wr, br, n_rows=n_rows, n_batch=n_batch, seq=seq)

    dest, items, cap = _routing(idx[:, :TOP_K], rank[:, :TOP_K], counts, n_rows, n_exp)
    xs = _dispatch(dest.reshape(n_rows // ROW_TILE, ROW_TILE * TOP_K), h2, cap)
    out_sorted = _experts(layer, items, xs, *expert_params)
    return _combine(dest.reshape(n_rows // COMBINE_TILE, COMBINE_TILE * TOP_K), out_sorted, gates, x1, mods,
                    n_batch=n_batch, seq=seq)


_PARAM_NAMES = ("norm_mix", "norm_ffn", "w_in", "w_pool", "pool_scale", "ssm_a_re", "ssm_a_im", "ssm_log_step",
                "ssm_b_re", "ssm_b_im", "ssm_c_re", "ssm_c_im", "ssm_d", "w_glu", "b_glu", "q_a_norm", "w_uq",
                "kv_a_norm", "w_ukv", "q_norm", "k_norm", "out_norm", "w_out", "w_router", "b_router")


def kernel(x, c, ctx, c_ctx, w_ada, b_ada, norm_mix, norm_ffn, w_in, w_pool, pool_scale, ssm_a_re, ssm_a_im, ssm_log_step, ssm_b_re, ssm_b_im, ssm_c_re, ssm_c_im, ssm_d, w_glu, b_glu, q_a_norm, w_uq, kv_a_norm, w_ukv, q_norm, k_norm, out_norm, w_out, w_router, b_router, w_gate, b_gate, w_up, b_up, w_down, b_down):
    n_batch, seq, d = x.shape
    ctx_len = ctx.shape[1]
    depth = w_ada.shape[0]
    assert seq % ROW_TILE == 0 and ctx_len % ROW_TILE == 0 and seq % GRID_W == 0 and n_batch + 1 <= 8
    stacked = dict(zip(_PARAM_NAMES, (norm_mix, norm_ffn, w_in, w_pool, pool_scale, ssm_a_re, ssm_a_im,
                                      ssm_log_step, ssm_b_re, ssm_b_im, ssm_c_re, ssm_c_im, ssm_d, w_glu, b_glu,
                                      q_a_norm, w_uq, kv_a_norm, w_ukv, q_norm, k_norm, out_norm, w_out, w_router,
                                      b_router)))
    expert_params = (w_gate, b_gate, w_up, b_up, w_down, b_down)
    cvecs = jnp.concatenate([c, c_ctx[None, :], jnp.zeros((8 - n_batch - 1, d), F32)], axis=0)
    mods = _ada(cvecs, w_ada, b_ada)
    cos_t, sin_t = _rope_tables(n_batch, seq, ctx_len)
    n_lat = n_batch * seq
    xt = jnp.concatenate([x.reshape(n_lat, d), ctx.reshape(n_batch * ctx_len, d)], axis=0)
    for layer in range(depth):
        p = {name: val[layer] for name, val in stacked.items()}
        xt = _layer(layer, xt, mods[layer], cos_t, sin_t, p, expert_params, n_batch=n_batch, seq=seq,
                    ctx_len=ctx_len, ctx_out=layer < depth - 1)
    return xt[:n_lat].reshape(n_batch, seq, d)
```
